```python
import jax, jax.numpy as jnp
from jax import lax
import numpy as np

D_MODEL = 1024
BATCH = 8
SEQ = 2048
DEPTH = 1

MLA_HEADS = 8
MLA_NOPE = 64
MLA_ROPE = 32
MLA_V = 64
Q_LORA = 384
KV_LORA = 256
ROPE_THETA = 10000.0
SWA_HEADS = 8
SWA_KV_HEADS = 2
SWA_HEAD_DIM = 64
WINDOW = 128
Q_BLOCK = 128
D_FF = 4 * D_MODEL
EPS = 1e-6
MIX_WIDTH = MLA_HEADS * MLA_V + SWA_HEADS * SWA_HEAD_DIM
SWA_Q_COLS = SWA_HEADS * SWA_HEAD_DIM
SWA_KV_COLS = SWA_KV_HEADS * SWA_HEAD_DIM
IN_COLS = Q_LORA + KV_LORA + MLA_ROPE + SWA_Q_COLS + 2 * SWA_KV_COLS
N_MOD = 6

kernel_name = "hybrid_mla_swa_sink_alibi_sqrelu_adaln"


def rmsnorm(x, g):
    xf = x.astype(jnp.float32)
    y = xf * lax.rsqrt(jnp.mean(xf * xf, axis=-1, keepdims=True) + EPS)
    return (y * g.astype(jnp.float32)).astype(x.dtype)


def rope(x, pos):
    r = x.shape[-1]
    freqs = ROPE_THETA ** (-jnp.arange(0, r, 2, dtype=jnp.float32) / r)
    ang = pos[:, None] * freqs[None, :]
    cos = jnp.cos(ang)[None, :, None, :]
    sin = jnp.sin(ang)[None, :, None, :]
    xf = x.astype(jnp.float32)
    x1, x2 = xf[..., : r // 2], xf[..., r // 2:]
    out = jnp.concatenate([x1 * cos - x2 * sin, x1 * sin + x2 * cos], axis=-1)
    return out.astype(x.dtype)


def alibi_slopes(n):
    return jnp.asarray([2.0 ** (-8.0 * (h + 1) / n) for h in range(n)], dtype=jnp.float32)


def mla_group(q_lat, kv_lat, k_rope, g_qa, w_qb, g_kva, w_kvb):
    b, s, _ = q_lat.shape
    q = jnp.einsum('bsr,rhd->bshd', rmsnorm(q_lat, g_qa), w_qb)
    kv = jnp.einsum('bsr,rhd->bshd', rmsnorm(kv_lat, g_kva), w_kvb)
    pos = jnp.arange(s, dtype=jnp.float32)
    q_nope = q[..., :MLA_NOPE]
    q_pe = rope(q[..., MLA_NOPE:], pos)
    k_nope = kv[..., :MLA_NOPE]
    v = kv[..., MLA_NOPE:]
    k_pe = rope(k_rope[:, :, None, :], pos)[:, :, 0, :]
    scale = (MLA_NOPE + MLA_ROPE) ** -0.5
    nb = s // Q_BLOCK
    qn_b = q_nope.reshape(b, nb, Q_BLOCK, MLA_HEADS, MLA_NOPE).transpose(1, 0, 2, 3, 4)
    qp_b = q_pe.reshape(b, nb, Q_BLOCK, MLA_HEADS, MLA_ROPE).transpose(1, 0, 2, 3, 4)
    kpos = jnp.arange(s)

    def block(args):
        i, qn, qp = args
        sc = (jnp.einsum('bqhd,bkhd->bhqk', qn, k_nope).astype(jnp.float32)
              + jnp.einsum('bqhd,bkd->bhqk', qp, k_pe).astype(jnp.float32)) * scale
        qpos = i * Q_BLOCK + jnp.arange(Q_BLOCK)
        causal = kpos[None, :] <= qpos[:, None]
        sc = jnp.where(causal[None, None], sc, -jnp.inf)
        p = jax.nn.softmax(sc, axis=-1).astype(v.dtype)
        return jnp.einsum('bhqk,bkhd->bqhd', p, v)

    out = lax.map(block, (jnp.arange(nb), qn_b, qp_b))
    return out.transpose(1, 0, 2, 3, 4).reshape(b, s, MLA_HEADS * MLA_V)


def swa_group(q, k, v, sinks):
    b, s, _ = q.shape
    nb = s // Q_BLOCK
    grp = SWA_HEADS // SWA_KV_HEADS
    qb = q.reshape(b, nb, Q_BLOCK, SWA_KV_HEADS, grp, SWA_HEAD_DIM)
    kb = k.reshape(b, nb, Q_BLOCK, SWA_KV_HEADS, SWA_HEAD_DIM)
    vb = v.reshape(b, nb, Q_BLOCK, SWA_KV_HEADS, SWA_HEAD_DIM)
    zpad = jnp.zeros_like(kb[:, :1])
    kw = jnp.concatenate([jnp.concatenate([zpad, kb[:, :-1]], axis=1), kb], axis=2)
    vw = jnp.concatenate([jnp.concatenate([zpad, vb[:, :-1]], axis=1), vb], axis=2)
    sc = jnp.einsum('bnqkgd,bnjkd->bnkgqj', qb, kw).astype(jnp.float32) * (SWA_HEAD_DIM ** -0.5)
    qi = jnp.arange(Q_BLOCK)[:, None] + Q_BLOCK
    kj = jnp.arange(2 * Q_BLOCK)[None, :]
    dist = qi - kj
    blk = jnp.arange(nb)
    valid = ((dist >= 0) & (dist < WINDOW))[None] & ((blk[:, None, None] > 0) | (kj[None] >= Q_BLOCK))
    slopes = alibi_slopes(SWA_HEADS).reshape(SWA_KV_HEADS, grp)
    sc = sc - slopes[:, :, None, None] * dist.astype(jnp.float32)[None, None]
    sc = jnp.where(valid[None, :, None, None], sc, -jnp.inf)
    sink = jnp.broadcast_to(sinks.astype(jnp.float32).reshape(1, 1, SWA_KV_HEADS, grp, 1, 1),
                            sc.shape[:-1] + (1,))
    p = jax.nn.softmax(jnp.concatenate([sc, sink], axis=-1), axis=-1)[..., :-1]
    out = jnp.einsum('bnkgqj,bnjkd->bnqkgd', p.astype(v.dtype), vw)
    return out.reshape(b, s, SWA_Q_COLS)


def setup_inputs(seed: int = 0) -> dict:
    key = jax.random.key(seed)
    ks = jax.random.split(key, 20)
    n = jax.random.normal
    f32 = jnp.float32
    return {
        "x": n(ks[0], (BATCH, SEQ, D_MODEL), f32),
        "c": n(ks[1], (BATCH, D_MODEL), f32),
        "w_ada": n(ks[2], (DEPTH, D_MODEL, N_MOD * D_MODEL), f32) * D_MODEL ** -0.5,
        "b_ada": n(ks[3], (DEPTH, N_MOD * D_MODEL), f32) * 0.02,
        "norm_mix_g": 1.0 + 0.05 * n(ks[4], (DEPTH, D_MODEL), f32),
        "w_in": n(ks[5], (DEPTH, D_MODEL, IN_COLS), f32) * D_MODEL ** -0.5,
        "g_qa": 1.0 + 0.05 * n(ks[6], (DEPTH, Q_LORA), f32),
        "w_qb": n(ks[7], (DEPTH, Q_LORA, MLA_HEADS, MLA_NOPE + MLA_ROPE), f32) * Q_LORA ** -0.5,
        "g_kva": 1.0 + 0.05 * n(ks[8], (DEPTH, KV_LORA), f32),
        "w_kvb": n(ks[9], (DEPTH, KV_LORA, MLA_HEADS, MLA_NOPE + MLA_V), f32) * KV_LORA ** -0.5,
        "sinks": 0.5 * n(ks[10], (DEPTH, SWA_HEADS), f32),
        "w_o": n(ks[11], (DEPTH, MIX_WIDTH, D_MODEL), f32) * MIX_WIDTH ** -0.5,
        "norm_mlp_g": 1.0 + 0.05 * n(ks[12], (DEPTH, D_MODEL), f32),
        "w_up": n(ks[13], (DEPTH, D_MODEL, D_FF), f32) * D_MODEL ** -0.5,
        "w_down": n(ks[14], (DEPTH, D_FF, D_MODEL), f32) * D_FF ** -0.5,
        "final_g": 1.0 + 0.05 * n(ks[15], (D_MODEL,), f32),
    }


def reference(x, c, w_ada, b_ada, norm_mix_g, w_in, g_qa, w_qb, g_kva, w_kvb, sinks,
              w_o, norm_mlp_g, w_up, w_down, final_g):
    o1 = Q_LORA
    o2 = o1 + KV_LORA
    o3 = o2 + MLA_ROPE
    o4 = o3 + SWA_Q_COLS
    o5 = o4 + SWA_KV_COLS
    silu_c = jax.nn.silu(c)
    for l in range(DEPTH):
        mod = jnp.einsum('bd,de->be', silu_c, w_ada[l]) + b_ada[l]
        sh1, sc1, g1, sh2, sc2, g2 = jnp.split(mod[:, None, :], N_MOD, axis=-1)
        h = rmsnorm(x, norm_mix_g[l]) * (1.0 + sc1) + sh1
        proj = jnp.einsum('bsd,de->bse', h, w_in[l])
        y_mla = mla_group(proj[..., :o1], proj[..., o1:o2], proj[..., o2:o3],
                          g_qa[l], w_qb[l], g_kva[l], w_kvb[l])
        y_swa = swa_group(proj[..., o3:o4], proj[..., o4:o5], proj[..., o5:], sinks[l])
        mix = jnp.concatenate([y_mla, y_swa], axis=-1)
        x = x + g1 * jnp.einsum('bse,ed->bsd', mix, w_o[l])
        h = rmsnorm(x, norm_mlp_g[l]) * (1.0 + sc2) + sh2
        u = jnp.square(jax.nn.relu(jnp.einsum('bsd,df->bsf', h, w_up[l])))
        x = x + g2 * jnp.einsum('bsf,fd->bsd', u, w_down[l])
    return rmsnorm(x, final_g)
```

```python
import functools

import numpy as np
import jax
import jax.numpy as jnp
from jax import lax
from jax.experimental import pallas as pl
from jax.experimental.pallas import tpu as pltpu

D_MODEL = 1024
SEQ = 2048
MLA_HEADS = 8
MLA_NOPE = 64
MLA_ROPE = 32
MLA_V = 64
Q_LORA = 384
KV_LORA = 256
ROPE_THETA = 10000.0
SWA_HEADS = 8
SWA_KV_HEADS = 2
SWA_GROUP = SWA_HEADS // SWA_KV_HEADS
SWA_HEAD_DIM = 64
WINDOW = 128
D_FF = 4 * D_MODEL
EPS = 1e-6
N_MOD = 6

LANES = 128
HALF = LANES // 2
MLA_QK_COLS = MLA_HEADS * LANES
MLA_V_COLS = MLA_HEADS * MLA_V
SWA_Q_COLS = SWA_HEADS * SWA_HEAD_DIM
SWA_KV_COLS = SWA_KV_HEADS * SWA_HEAD_DIM
C_QLAT = 0
C_KVLAT = C_QLAT + Q_LORA
C_KPE = C_KVLAT + KV_LORA
C_SQ = C_KPE + LANES
C_SK = C_SQ + SWA_Q_COLS
C_SV = C_SK + SWA_KV_COLS
IN_COLS_PAD = C_SV + SWA_KV_COLS

VMEM_LIMIT_BYTES = 56 * 1024 * 1024

BF16 = jnp.bfloat16
F32 = jnp.float32


def _const_spec(shape):
    nd = len(shape)
    return pl.BlockSpec(shape, lambda *_: (0,) * nd, pipeline_mode=pl.Buffered(1))


def _rms(x):
    return x * lax.rsqrt(jnp.mean(x * x, axis=-1, keepdims=True) + EPS)


def _ada_kernel(c_ref, w_ref, b_ref, o_ref):
    c = c_ref[...]
    s = c / (1.0 + jnp.exp(-c))
    o_ref[...] = jnp.dot(s.astype(BF16), w_ref[...].astype(BF16), preferred_element_type=F32) + b_ref[...]


def _ada_call(c, w_ada, b_ada):
    b = c.shape[0]
    n = w_ada.shape[1]
    tn = D_MODEL
    return pl.pallas_call(
        _ada_kernel,
        grid=(n // tn,),
        in_specs=[
            pl.BlockSpec((b, D_MODEL), lambda j: (0, 0)),
            pl.BlockSpec((D_MODEL, tn), lambda j: (0, j)),
            pl.BlockSpec((1, tn), lambda j: (0, j)),
        ],
        out_specs=pl.BlockSpec((b, tn), lambda j: (0, j)),
        out_shape=jax.ShapeDtypeStruct((b, n), F32),
        compiler_params=pltpu.CompilerParams(dimension_semantics=("parallel",)),
        name="ada_mod",
    )(c, w_ada, b_ada.reshape(1, n))


def _rope_group(xg, rope_c, rope_s1, rope_s2):
    up = pltpu.roll(xg, LANES - MLA_ROPE // 2, 1)
    down = pltpu.roll(xg, MLA_ROPE // 2, 1)
    return xg * rope_c + up * rope_s1 + down * rope_s2


def _pre_kernel(x_ref, mod_ref, gmix_ref, win_ref, gqa_ref, wq_ref, gkva_ref, wk_ref, wv_ref,
                ropeq_ref, ropek_ref,
                qf_ref, kf_ref, v_ref, sq_ref, sk_ref, sv_ref):
    x = x_ref[0]
    mod = mod_ref[0]
    sh1 = mod[0:1]
    sc1 = mod[1:2]
    h = _rms(x) * gmix_ref[...] * (1.0 + sc1) + sh1
    proj = jnp.dot(h.astype(BF16), win_ref[...], preferred_element_type=F32)

    sq_ref[0] = (proj[:, C_SQ:C_SK] * (SWA_HEAD_DIM ** -0.5)).astype(BF16)
    swa_k = proj[:, C_SK:C_SV]
    low = lax.broadcasted_iota(jnp.int32, swa_k.shape, 1) < HALF
    sk_ref[0, :, :LANES] = jnp.where(low, swa_k, 0.0).astype(BF16)
    sk_ref[0, :, LANES:] = jnp.where(low, 0.0, swa_k).astype(BF16)
    sv_ref[0] = proj[:, C_SV:IN_COLS_PAD].astype(BF16)

    qn = (_rms(proj[:, C_QLAT:C_KVLAT]) * gqa_ref[...]).astype(BF16)
    kvn = (_rms(proj[:, C_KVLAT:C_KPE]) * gkva_ref[...]).astype(BF16)
    q = jnp.dot(qn, wq_ref[...], preferred_element_type=F32)
    kn = jnp.dot(kvn, wk_ref[...], preferred_element_type=F32)
    v_ref[0] = jnp.dot(kvn, wv_ref[...], preferred_element_type=F32).astype(BF16)

    qc, qs1, qs2 = ropeq_ref[0], ropeq_ref[1], ropeq_ref[2]
    kc, ks1, ks2 = ropek_ref[0], ropek_ref[1], ropek_ref[2]
    kpe = _rope_group(proj[:, C_KPE:C_SQ], kc, ks1, ks2)
    for hd in range(MLA_HEADS):
        sl = slice(hd * LANES, (hd + 1) * LANES)
        qf_ref[0, :, sl] = _rope_group(q[:, sl], qc, qs1, qs2).astype(BF16)
        kf_ref[0, :, sl] = (kn[:, sl] + kpe).astype(BF16)


def _rope_tables(scale_q):
    half = MLA_ROPE // 2
    freqs = ROPE_THETA ** (-np.arange(0, MLA_ROPE, 2, dtype=np.float64) / MLA_ROPE)
    ang = np.arange(SEQ, dtype=np.float64)[:, None] * freqs[None, :]
    cos, sin = np.cos(ang), np.sin(ang)
    keep = np.zeros((SEQ, LANES))
    up = np.zeros((SEQ, LANES))
    down = np.zeros((SEQ, LANES))
    keep[:, :MLA_NOPE] = 1.0
    keep[:, MLA_NOPE:MLA_NOPE + half] = cos
    keep[:, MLA_NOPE + half:MLA_NOPE + MLA_ROPE] = cos
    up[:, MLA_NOPE:MLA_NOPE + half] = -sin
    down[:, MLA_NOPE + half:MLA_NOPE + MLA_ROPE] = sin
    return (np.stack([keep, up, down]) * scale_q).astype(np.float32)


def _pre_call(x, mod, gmix, win, gqa, wq, gkva, wk, wv, tm):
    b, s, d = x.shape
    ropeq = jnp.asarray(_rope_tables((MLA_NOPE + MLA_ROPE) ** -0.5))
    ropek = jnp.asarray(_rope_tables(1.0))
    row = lambda bi, i: (bi, i, 0)
    out_shapes = (
        jax.ShapeDtypeStruct((b, s, MLA_QK_COLS), BF16),
        jax.ShapeDtypeStruct((b, s, MLA_QK_COLS), BF16),
        jax.ShapeDtypeStruct((b, s, MLA_V_COLS), BF16),
        jax.ShapeDtypeStruct((b, s, SWA_Q_COLS), BF16),
        jax.ShapeDtypeStruct((b, s, SWA_KV_HEADS * LANES), BF16),
        jax.ShapeDtypeStruct((b, s, SWA_KV_COLS), BF16),
    )
    return pl.pallas_call(
        _pre_kernel,
        grid=(b, s // tm),
        in_specs=[
            pl.BlockSpec((1, tm, d), row),
            pl.BlockSpec((1, N_MOD, d), lambda bi, i: (bi, 0, 0)),
            _const_spec((1, d)),
            _const_spec(win.shape),
            _const_spec((1, Q_LORA)),
            _const_spec(wq.shape),
            _const_spec((1, KV_LORA)),
            _const_spec(wk.shape),
            _const_spec(wv.shape),
            pl.BlockSpec((3, tm, LANES), lambda bi, i: (0, i, 0)),
            pl.BlockSpec((3, tm, LANES), lambda bi, i: (0, i, 0)),
        ],
        out_specs=tuple(pl.BlockSpec((1, tm, sh.shape[-1]), row) for sh in out_shapes),
        out_shape=out_shapes,
        compiler_params=pltpu.CompilerParams(
            dimension_semantics=("parallel", "parallel"), vmem_limit_bytes=VMEM_LIMIT_BYTES),
        name="pre_attn",
    )(x, mod, gmix, win, gqa, wq, gkva, wk, wv, ropeq, ropek)


def _mla_kernel(q_ref, k_ref, v_ref, o_ref, *, tq):
    i = pl.program_id(2)
    lane = lax.broadcasted_iota(jnp.int32, (tq, LANES), 1)
    qi = lax.broadcasted_iota(jnp.int32, (tq, tq), 0)
    kj = lax.broadcasted_iota(jnp.int32, (tq, tq), 1)
    causal = kj <= qi
    outs = []
    for hh in range(2):
        cols = slice(hh * LANES, (hh + 1) * LANES)
        q = q_ref[0, :, cols]

        def step(j, carry, masked):
            m, l, acc = carry
            rows = pl.ds(pl.multiple_of(j * tq, tq), tq)
            k = k_ref[0, rows, cols]
            v = v_ref[0, rows, :]
            s = lax.dot_general(q, k, (((1,), (1,)), ((), ())), preferred_element_type=F32)
            if masked:
                s = jnp.where(causal, s, -jnp.inf)
            m_new = jnp.maximum(m, jnp.max(s, axis=-1, keepdims=True))
            alpha = jnp.exp(m - m_new)
            p = jnp.exp(s - m_new)
            l_new = alpha * l + jnp.sum(p, axis=-1, keepdims=True)
            acc_new = alpha * acc + jnp.dot(p.astype(BF16), v, preferred_element_type=F32)
            return m_new, l_new, acc_new

        init = (jnp.full((tq, 1), -jnp.inf, F32), jnp.zeros((tq, 1), F32), jnp.zeros((tq, LANES), F32))
        carry = lax.fori_loop(0, i, functools.partial(step, masked=False), init)
        m, l, acc = step(i, carry, masked=True)
        outs.append(acc / l)
    o_ref[0] = jnp.where(lane < HALF, outs[0], outs[1]).astype(BF16)


def _mla_call(qf, kf, v, tq):
    b, s, _ = qf.shape
    pairs = MLA_HEADS // 2
    return pl.pallas_call(
        functools.partial(_mla_kernel, tq=tq),
        grid=(b, pairs, s // tq),
        in_specs=[
            pl.BlockSpec((1, tq, 2 * LANES), lambda bi, hp, i: (bi, i, hp)),
            pl.BlockSpec((1, s, 2 * LANES), lambda bi, hp, i: (bi, 0, hp)),
            pl.BlockSpec((1, s, LANES), lambda bi, hp, i: (bi, 0, hp)),
        ],
        out_specs=pl.BlockSpec((1, tq, LANES), lambda bi, hp, i: (bi, i, hp)),
        out_shape=jax.ShapeDtypeStruct((b, s, MLA_V_COLS), BF16),
        compiler_params=pltpu.CompilerParams(
            dimension_semantics=("parallel", "parallel", "arbitrary"), vmem_limit_bytes=VMEM_LIMIT_BYTES),
        name="mla_attn",
    )(qf, kf, v)


def _alibi_slope(h):
    return 2.0 ** (-8.0 * (h + 1) / SWA_HEADS)


def _swa_kernel(sink_ref, q_ref, k_ref, v_ref, o_ref):
    i = pl.program_id(1)
    tq = WINDOW
    start = pl.multiple_of(jnp.maximum(i - 1, 0) * tq, tq)
    off = i * tq - start
    v = v_ref[0, pl.ds(start, 2 * tq), :]
    qi = lax.broadcasted_iota(jnp.int32, (tq, 2 * tq), 0)
    kj = lax.broadcasted_iota(jnp.int32, (tq, 2 * tq), 1)
    dist = qi - kj + off
    valid = (dist >= 0) & (dist < WINDOW)
    distf = dist.astype(F32)
    lane = lax.broadcasted_iota(jnp.int32, (tq, LANES), 1)
    low = lane < HALF
    for g in range(SWA_GROUP):
        cols = slice(g * LANES, (g + 1) * LANES)
        qg = q_ref[0, :, cols]
        outs = []
        for kv in range(SWA_KV_HEADS):
            h = kv * SWA_GROUP + g
            k = k_ref[0, pl.ds(start, 2 * tq), kv * LANES:(kv + 1) * LANES]
            s = lax.dot_general(qg, k, (((1,), (1,)), ((), ())), preferred_element_type=F32)
            s = s - _alibi_slope(h) * distf
            s = jnp.where(valid, s, -jnp.inf)
            sink = sink_ref[0, h]
            m = jnp.maximum(jnp.max(s, axis=-1, keepdims=True), sink)
            p = jnp.exp(s - m)
            denom = jnp.sum(p, axis=-1, keepdims=True) + jnp.exp(sink - m)
            outs.append(jnp.dot(p.astype(BF16), v, preferred_element_type=F32) / denom)
        o_ref[0, :, cols] = jnp.where(low, outs[0], outs[1]).astype(BF16)


def _swa_call(sinks, sq, sk, sv):
    b, s, _ = sq.shape
    tq = WINDOW
    return pl.pallas_call(
        _swa_kernel,
        grid=(b, s // tq),
        in_specs=[
            pl.BlockSpec(memory_space=pltpu.SMEM),
            pl.BlockSpec((1, tq, SWA_Q_COLS), lambda bi, i: (bi, i, 0)),
            pl.BlockSpec((1, s, SWA_KV_HEADS * LANES), lambda bi, i: (bi, 0, 0)),
            pl.BlockSpec((1, s, SWA_KV_COLS), lambda bi, i: (bi, 0, 0)),
        ],
        out_specs=pl.BlockSpec((1, tq, SWA_Q_COLS), lambda bi, i: (bi, i, 0)),
        out_shape=jax.ShapeDtypeStruct((b, s, SWA_Q_COLS), BF16),
        compiler_params=pltpu.CompilerParams(dimension_semantics=("parallel", "arbitrary")),
        name="swa_attn",
    )(sinks, sq, sk, sv)


def _post_kernel(x_ref, mla_ref, swa_ref, mod_ref, woa_ref, wob_ref, gmlp_ref, wup_ref, wdown_ref, gfin_ref,
                 o_ref, *, ff_chunk):
    x = x_ref[0]
    mod = mod_ref[0]
    g1, sh2, sc2, g2 = mod[2:3], mod[3:4], mod[4:5], mod[5:6]
    attn = (jnp.dot(mla_ref[0], woa_ref[...], preferred_element_type=F32)
            + jnp.dot(swa_ref[0], wob_ref[...], preferred_element_type=F32))
    x1 = x + g1 * attn
    h2 = (_rms(x1) * gmlp_ref[...] * (1.0 + sc2) + sh2).astype(BF16)
    acc = jnp.zeros_like(x1)
    for c in range(D_FF // ff_chunk):
        cs = slice(c * ff_chunk, (c + 1) * ff_chunk)
        u = jnp.maximum(jnp.dot(h2, wup_ref[:, cs], preferred_element_type=F32), 0.0)
        acc = acc + jnp.dot((u * u).astype(BF16), wdown_ref[cs, :], preferred_element_type=F32)
    x2 = x1 + g2 * acc
    o_ref[0] = _rms(x2) * gfin_ref[...]


def _post_call(x, mla, swa, mod, woa, wob, gmlp, wup, wdown, gfin, tm, ff_chunk):
    b, s, d = x.shape
    row = lambda bi, i: (bi, i, 0)
    return pl.pallas_call(
        functools.partial(_post_kernel, ff_chunk=ff_chunk),
        grid=(b, s // tm),
        in_specs=[
            pl.BlockSpec((1, tm, d), row),
            pl.BlockSpec((1, tm, MLA_V_COLS), row),
            pl.BlockSpec((1, tm, SWA_Q_COLS), row),
            pl.BlockSpec((1, N_MOD, d), lambda bi, i: (bi, 0, 0)),
            _const_spec(woa.shape),
            _const_spec(wob.shape),
            _const_spec((1, d)),
            _const_spec(wup.shape),
            _const_spec(wdown.shape),
            _const_spec((1, d)),
        ],
        out_specs=pl.BlockSpec((1, tm, d), row),
        out_shape=jax.ShapeDtypeStruct((b, s, d), F32),
        compiler_params=pltpu.CompilerParams(
            dimension_semantics=("parallel", "parallel"), vmem_limit_bytes=VMEM_LIMIT_BYTES),
        name="post_attn",
    )(x, mla, swa, mod, woa, wob, gmlp, wup, wdown, gfin)


def _prep_layer_weights(w_in, w_qb, w_kvb, w_o):
    o1 = Q_LORA
    o2 = o1 + KV_LORA
    o3 = o2 + MLA_ROPE
    o4 = o3 + SWA_Q_COLS
    o5 = o4 + SWA_KV_COLS
    d = w_in.shape[0]
    kpe = jnp.zeros((d, LANES), w_in.dtype).at[:, MLA_NOPE:MLA_NOPE + MLA_ROPE].set(w_in[:, o2:o3])
    sq = w_in[:, o3:o4].reshape(d, SWA_KV_HEADS, SWA_GROUP, SWA_HEAD_DIM)
    sq = sq.transpose(0, 2, 1, 3).reshape(d, SWA_Q_COLS)
    win = jnp.concatenate([w_in[:, :o2], kpe, sq, w_in[:, o4:]], axis=1).astype(BF16)

    wq = jnp.pad(w_qb, ((0, 0), (0, 0), (0, LANES - MLA_NOPE - MLA_ROPE)))
    wq = wq.reshape(Q_LORA, MLA_QK_COLS).astype(BF16)
    wk = jnp.pad(w_kvb[:, :, :MLA_NOPE], ((0, 0), (0, 0), (0, LANES - MLA_NOPE)))
    wk = wk.reshape(KV_LORA, MLA_QK_COLS).astype(BF16)
    wv = w_kvb[:, :, MLA_NOPE:].reshape(KV_LORA, MLA_V_COLS).astype(BF16)

    woa = w_o[:MLA_V_COLS].astype(BF16)
    wob = w_o[MLA_V_COLS:].reshape(SWA_KV_HEADS, SWA_GROUP, SWA_HEAD_DIM, -1)
    wob = wob.transpose(1, 0, 2, 3).reshape(SWA_Q_COLS, -1).astype(BF16)
    return win, wq, wk, wv, woa, wob


def kernel(x, c, w_ada, b_ada, norm_mix_g, w_in, g_qa, w_qb, g_kva, w_kvb, sinks,
           w_o, norm_mlp_g, w_up, w_down, final_g):
    depth = w_ada.shape[0]
    b = x.shape[0]
    assert depth == 1, "the final rmsnorm is fused into the single layer's post-attention call"
    for l in range(depth):
        mod = _ada_call(c, w_ada[l], b_ada[l]).reshape(b, N_MOD, D_MODEL)
        win, wq, wk, wv, woa, wob = _prep_layer_weights(w_in[l], w_qb[l], w_kvb[l], w_o[l])
        qf, kf, v, sq, sk, sv = _pre_call(
            x, mod, norm_mix_g[l][None], win, g_qa[l][None], wq, g_kva[l][None], wk, wv, tm=512)
        mla = _mla_call(qf, kf, v, tq=256)
        swa = _swa_call(sinks[l][None], sq, sk, sv)
        x = _post_call(x, mla, swa, mod, woa, wob, norm_mlp_g[l][None],
                       w_up[l].astype(BF16), w_down[l].astype(BF16), final_g[None], tm=512, ff_chunk=1024)
    return x
```

```python
import functools

import numpy as np
import jax
import jax.numpy as jnp
from jax import lax
from jax.experimental import pallas as pl
from jax.experimental.pallas import tpu as pltpu

D_MODEL = 1024
SEQ = 2048
MLA_HEADS = 8
MLA_NOPE = 64
MLA_ROPE = 32
MLA_V = 64
Q_LORA = 384
KV_LORA = 256
ROPE_THETA = 10000.0
SWA_HEADS = 8
SWA_KV_HEADS = 2
SWA_GROUP = SWA_HEADS // SWA_KV_HEADS
SWA_HEAD_DIM = 64
WINDOW = 128
D_FF = 4 * D_MODEL
EPS = 1e-6
N_MOD = 6

LANES = 128
HALF = LANES // 2
MLA_QK_COLS = MLA_HEADS * LANES
MLA_V_COLS = MLA_HEADS * MLA_V
SWA_Q_COLS = SWA_HEADS * SWA_HEAD_DIM
SWA_KV_COLS = SWA_KV_HEADS * SWA_HEAD_DIM
C_QLAT = 0
C_KVLAT = C_QLAT + Q_LORA
C_KPE = C_KVLAT + KV_LORA
C_SQ = C_KPE + LANES
C_SK = C_SQ + SWA_Q_COLS
C_SV = C_SK + SWA_KV_COLS
IN_COLS_PAD = C_SV + SWA_KV_COLS

VMEM_LIMIT_BYTES = 56 * 1024 * 1024

BF16 = jnp.bfloat16
F32 = jnp.float32


def _const_spec(shape):
    nd = len(shape)
    return pl.BlockSpec(shape, lambda *_: (0,) * nd, pipeline_mode=pl.Buffered(1))


def _rms(x):
    return x * lax.rsqrt(jnp.mean(x * x, axis=-1, keepdims=True) + EPS)


def _ada_kernel(c_ref, w_ref, b_ref, o_ref):
    c = c_ref[...]
    s = c / (1.0 + jnp.exp(-c))
    o_ref[...] = jnp.dot(s.astype(BF16), w_ref[...].astype(BF16), preferred_element_type=F32) + b_ref[...]


def _ada_call(c, w_ada, b_ada):
    b = c.shape[0]
    n = w_ada.shape[1]
    tn = D_MODEL
    return pl.pallas_call(
        _ada_kernel,
        grid=(n // tn,),
        in_specs=[
            pl.BlockSpec((b, D_MODEL), lambda j: (0, 0)),
            pl.BlockSpec((D_MODEL, tn), lambda j: (0, j)),
            pl.BlockSpec((1, tn), lambda j: (0, j)),
        ],
        out_specs=pl.BlockSpec((b, tn), lambda j: (0, j)),
        out_shape=jax.ShapeDtypeStruct((b, n), F32),
        compiler_params=pltpu.CompilerParams(dimension_semantics=("parallel",)),
        name="ada_mod",
    )(c, w_ada, b_ada.reshape(1, n))


def _rope_group(xg, rope_c, rope_s1, rope_s2):
    up = pltpu.roll(xg, LANES - MLA_ROPE // 2, 1)
    down = pltpu.roll(xg, MLA_ROPE // 2, 1)
    return xg * rope_c + up * rope_s1 + down * rope_s2


def _pre_kernel(x_ref, mod_ref, gmix_ref, win_ref, gqa_ref, wq_ref, gkva_ref, wk_ref, wv_ref,
                ropeq_ref, ropek_ref,
                qf_ref, kf_ref, v_ref, sq_ref, sk_ref, sv_ref):
    x = x_ref[0]
    mod = mod_ref[0]
    sh1 = mod[0:1]
    sc1 = mod[1:2]
    h = _rms(x) * gmix_ref[...] * (1.0 + sc1) + sh1
    proj = jnp.dot(h.astype(BF16), win_ref[...], preferred_element_type=F32)

    sq_ref[0] = (proj[:, C_SQ:C_SK] * (SWA_HEAD_DIM ** -0.5)).astype(BF16)
    swa_k = proj[:, C_SK:C_SV]
    low = lax.broadcasted_iota(jnp.int32, swa_k.shape, 1) < HALF
    sk_ref[0, :, :LANES] = jnp.where(low, swa_k, 0.0).astype(BF16)
    sk_ref[0, :, LANES:] = jnp.where(low, 0.0, swa_k).astype(BF16)
    sv_ref[0] = proj[:, C_SV:IN_COLS_PAD].astype(BF16)

    qn = (_rms(proj[:, C_QLAT:C_KVLAT]) * gqa_ref[...]).astype(BF16)
    kvn = (_rms(proj[:, C_KVLAT:C_KPE]) * gkva_ref[...]).astype(BF16)
    q = jnp.dot(qn, wq_ref[...], preferred_element_type=F32)
    kn = jnp.dot(kvn, wk_ref[...], preferred_element_type=F32)
    vt = lax.dot_general(wv_ref[...], kvn, (((1,), (1,)), ((), ())), preferred_element_type=F32).astype(BF16)
    tk = v_ref.shape[-1]
    for hp in range(MLA_HEADS // 2):
        for jb in range(v_ref.shape[2]):
            v_ref[0, hp, jb] = vt[hp * LANES:(hp + 1) * LANES, jb * tk:(jb + 1) * tk]

    qc, qs1, qs2 = ropeq_ref[0], ropeq_ref[1], ropeq_ref[2]
    kc, ks1, ks2 = ropek_ref[0], ropek_ref[1], ropek_ref[2]
    kpe = _rope_group(proj[:, C_KPE:C_SQ], kc, ks1, ks2)
    for hd in range(MLA_HEADS):
        sl = slice(hd * LANES, (hd + 1) * LANES)
        qf_ref[0, :, sl] = _rope_group(q[:, sl], qc, qs1, qs2).astype(BF16)
        kf_ref[0, :, sl] = (kn[:, sl] + kpe).astype(BF16)


def _rope_tables(scale_q):
    half = MLA_ROPE // 2
    freqs = ROPE_THETA ** (-np.arange(0, MLA_ROPE, 2, dtype=np.float64) / MLA_ROPE)
    ang = np.arange(SEQ, dtype=np.float64)[:, None] * freqs[None, :]
    cos, sin = np.cos(ang), np.sin(ang)
    keep = np.zeros((SEQ, LANES))
    up = np.zeros((SEQ, LANES))
    down = np.zeros((SEQ, LANES))
    keep[:, :MLA_NOPE] = 1.0
    keep[:, MLA_NOPE:MLA_NOPE + half] = cos
    keep[:, MLA_NOPE + half:MLA_NOPE + MLA_ROPE] = cos
    up[:, MLA_NOPE:MLA_NOPE + half] = -sin
    down[:, MLA_NOPE + half:MLA_NOPE + MLA_ROPE] = sin
    return (np.stack([keep, up, down]) * scale_q).astype(np.float32)


def _pre_call(x, mod, gmix, win, gqa, wq, gkva, wk, wv, tm, tk):
    b, s, d = x.shape
    pairs = MLA_HEADS // 2
    ropeq = jnp.asarray(_rope_tables((MLA_NOPE + MLA_ROPE) ** -0.5 * np.log2(np.e)))
    ropek = jnp.asarray(_rope_tables(1.0))
    row = lambda bi, i: (bi, i, 0)
    out_shapes = (
        jax.ShapeDtypeStruct((b, s, MLA_QK_COLS), BF16),
        jax.ShapeDtypeStruct((b, s, MLA_QK_COLS), BF16),
        jax.ShapeDtypeStruct((b, pairs, s // tk, LANES, tk), BF16),
        jax.ShapeDtypeStruct((b, s, SWA_Q_COLS), BF16),
        jax.ShapeDtypeStruct((b, s, SWA_KV_HEADS * LANES), BF16),
        jax.ShapeDtypeStruct((b, s, SWA_KV_COLS), BF16),
    )
    out_specs = [pl.BlockSpec((1, tm, sh.shape[-1]), row) for sh in out_shapes]
    out_specs[2] = pl.BlockSpec((1, pairs, tm // tk, LANES, tk), lambda bi, i: (bi, 0, i, 0, 0))
    return pl.pallas_call(
        _pre_kernel,
        grid=(b, s // tm),
        in_specs=[
            pl.BlockSpec((1, tm, d), row),
            pl.BlockSpec((1, N_MOD, d), lambda bi, i: (bi, 0, 0)),
            _const_spec((1, d)),
            _const_spec(win.shape),
            _const_spec((1, Q_LORA)),
            _const_spec(wq.shape),
            _const_spec((1, KV_LORA)),
            _const_spec(wk.shape),
            _const_spec(wv.shape),
            pl.BlockSpec((3, tm, LANES), lambda bi, i: (0, i, 0)),
            pl.BlockSpec((3, tm, LANES), lambda bi, i: (0, i, 0)),
        ],
        out_specs=tuple(out_specs),
        out_shape=out_shapes,
        compiler_params=pltpu.CompilerParams(
            dimension_semantics=("parallel", "parallel"), vmem_limit_bytes=VMEM_LIMIT_BYTES),
        name="pre_attn",
    )(x, mod, gmix, win, gqa, wq, gkva, wk, wv, ropeq, ropek)


def _mla_kernel(q_ref, k_ref, vt_ref, o_ref, *, tq):
    i = pl.program_id(2)
    for nblk in range(1, k_ref.shape[1] // tq + 1):
        pl.when(i == nblk - 1)(functools.partial(_mla_blocks, q_ref, k_ref, vt_ref, o_ref, nblk=nblk, tq=tq))


def _mla_blocks(q_ref, k_ref, vt_ref, o_ref, *, nblk, tq):
    ki = lax.broadcasted_iota(jnp.int32, (tq, tq), 0)
    qi = lax.broadcasted_iota(jnp.int32, (tq, tq), 1)
    causal = ki <= qi
    heads = range(2)
    qs = [q_ref[0, :, hh * LANES:(hh + 1) * LANES] for hh in heads]

    def scores(j):
        ks = [k_ref[0, j * tq:(j + 1) * tq, hh * LANES:(hh + 1) * LANES] for hh in heads]
        return [lax.dot_general(ks[hh], qs[hh], (((1,), (1,)), ((), ())), preferred_element_type=F32)
                for hh in heads]

    state = [(jnp.full((1, tq), -jnp.inf, F32), jnp.zeros((1, tq), F32), jnp.zeros((MLA_V, tq), F32))
             for _ in heads]
    st_next = scores(0)
    for j in range(nblk):
        st_cur = st_next
        if j + 1 < nblk:
            st_next = scores(j + 1)
        for hh in heads:
            m, l, acc = state[hh]
            st = st_cur[hh]
            if j == nblk - 1:
                st = jnp.where(causal, st, -jnp.inf)
            vt = vt_ref[0, 0, j, hh * MLA_V:(hh + 1) * MLA_V, :]
            m_new = jnp.maximum(m, jnp.max(st, axis=0, keepdims=True))
            alpha = jnp.exp2(m - m_new)
            pt = jnp.exp2(st - m_new)
            l_new = alpha * l + jnp.sum(pt, axis=0, keepdims=True)
            acc_new = alpha * acc + jnp.dot(vt, pt.astype(BF16), preferred_element_type=F32)
            state[hh] = (m_new, l_new, acc_new)
    out_t = jnp.concatenate([acc / l for (_, l, acc) in state], axis=0)
    o_ref[0] = out_t.T.astype(BF16)


def _mla_call(qf, kf, vt, tq):
    b, s, _ = qf.shape
    pairs = MLA_HEADS // 2
    assert vt.shape == (b, pairs, s // tq, LANES, tq)
    return pl.pallas_call(
        functools.partial(_mla_kernel, tq=tq),
        grid=(b, pairs, s // tq),
        in_specs=[
            pl.BlockSpec((1, tq, 2 * LANES), lambda bi, hp, i: (bi, i, hp)),
            pl.BlockSpec((1, s, 2 * LANES), lambda bi, hp, i: (bi, 0, hp)),
            pl.BlockSpec((1, 1, s // tq, LANES, tq), lambda bi, hp, i: (bi, hp, 0, 0, 0)),
        ],
        out_specs=pl.BlockSpec((1, tq, LANES), lambda bi, hp, i: (bi, i, hp)),
        out_shape=jax.ShapeDtypeStruct((b, s, MLA_V_COLS), BF16),
        compiler_params=pltpu.CompilerParams(
            dimension_semantics=("parallel", "parallel", "arbitrary"), vmem_limit_bytes=VMEM_LIMIT_BYTES),
        name="mla_attn",
    )(qf, kf, vt)


def _alibi_slope(h):
    return 2.0 ** (-8.0 * (h + 1) / SWA_HEADS)


def _swa_kernel(sink_ref, q_ref, k_ref, v_ref, o_ref):
    i = pl.program_id(1)
    tq = WINDOW
    start = pl.multiple_of(jnp.maximum(i - 1, 0) * tq, tq)
    off = i * tq - start
    v = v_ref[0, pl.ds(start, 2 * tq), :]
    qi = lax.broadcasted_iota(jnp.int32, (tq, 2 * tq), 0)
    kj = lax.broadcasted_iota(jnp.int32, (tq, 2 * tq), 1)
    dist = qi - kj + off
    valid = (dist >= 0) & (dist < WINDOW)
    distf = dist.astype(F32)
    lane = lax.broadcasted_iota(jnp.int32, (tq, LANES), 1)
    low = lane < HALF
    for g in range(SWA_GROUP):
        cols = slice(g * LANES, (g + 1) * LANES)
        qg = q_ref[0, :, cols]
        outs = []
        for kv in range(SWA_KV_HEADS):
            h = kv * SWA_GROUP + g
            k = k_ref[0, pl.ds(start, 2 * tq), kv * LANES:(kv + 1) * LANES]
            s = lax.dot_general(qg, k, (((1,), (1,)), ((), ())), preferred_element_type=F32)
            s = s - _alibi_slope(h) * distf
            s = jnp.where(valid, s, -jnp.inf)
            sink = sink_ref[0, h]
            m = jnp.maximum(jnp.max(s, axis=-1, keepdims=True), sink)
            p = jnp.exp(s - m)
            denom = jnp.sum(p, axis=-1, keepdims=True) + jnp.exp(sink - m)
            outs.append(jnp.dot(p.astype(BF16), v, preferred_element_type=F32) / denom)
        o_ref[0, :, cols] = jnp.where(low, outs[0], outs[1]).astype(BF16)


def _swa_call(sinks, sq, sk, sv):
    b, s, _ = sq.shape
    tq = WINDOW
    return pl.pallas_call(
        _swa_kernel,
        grid=(b, s // tq),
        in_specs=[
            pl.BlockSpec(memory_space=pltpu.SMEM),
            pl.BlockSpec((1, tq, SWA_Q_COLS), lambda bi, i: (bi, i, 0)),
            pl.BlockSpec((1, s, SWA_KV_HEADS * LANES), lambda bi, i: (bi, 0, 0)),
            pl.BlockSpec((1, s, SWA_KV_COLS), lambda bi, i: (bi, 0, 0)),
        ],
        out_specs=pl.BlockSpec((1, tq, SWA_Q_COLS), lambda bi, i: (bi, i, 0)),
        out_shape=jax.ShapeDtypeStruct((b, s, SWA_Q_COLS), BF16),
        compiler_params=pltpu.CompilerParams(dimension_semantics=("parallel", "arbitrary")),
        name="swa_attn",
    )(sinks, sq, sk, sv)


def _post_kernel(x_ref, mla_ref, swa_ref, mod_ref, woa_ref, wob_ref, gmlp_ref, wup_ref, wdown_ref, gfin_ref,
                 o_ref, *, ff_chunk):
    x = x_ref[0]
    mod = mod_ref[0]
    g1, sh2, sc2, g2 = mod[2:3], mod[3:4], mod[4:5], mod[5:6]
    attn = (jnp.dot(mla_ref[0], woa_ref[...], preferred_element_type=F32)
            + jnp.dot(swa_ref[0], wob_ref[...], preferred_element_type=F32))
    x1 = x + g1 * attn
    h2 = (_rms(x1) * gmlp_ref[...] * (1.0 + sc2) + sh2).astype(BF16)
    acc = jnp.zeros_like(x1)
    for c in range(D_FF // ff_chunk):
        cs = slice(c * ff_chunk, (c + 1) * ff_chunk)
        u = jnp.maximum(jnp.dot(h2, wup_ref[:, cs], preferred_element_type=F32), 0.0)
        acc = acc + jnp.dot((u * u).astype(BF16), wdown_ref[cs, :], preferred_element_type=F32)
    x2 = x1 + g2 * acc
    o_ref[0] = _rms(x2) * gfin_ref[...]


def _post_call(x, mla, swa, mod, woa, wob, gmlp, wup, wdown, gfin, tm, ff_chunk):
    b, s, d = x.shape
    row = lambda bi, i: (bi, i, 0)
    return pl.pallas_call(
        functools.partial(_post_kernel, ff_chunk=ff_chunk),
        grid=(b, s // tm),
        in_specs=[
            pl.BlockSpec((1, tm, d), row),
            pl.BlockSpec((1, tm, MLA_V_COLS), row),
            pl.BlockSpec((1, tm, SWA_Q_COLS), row),
            pl.BlockSpec((1, N_MOD, d), lambda bi, i: (bi, 0, 0)),
            _const_spec(woa.shape),
            _const_spec(wob.shape),
            _const_spec((1, d)),
            _const_spec(wup.shape),
            _const_spec(wdown.shape),
            _const_spec((1, d)),
        ],
        out_specs=pl.BlockSpec((1, tm, d), row),
        out_shape=jax.ShapeDtypeStruct((b, s, d), F32),
        compiler_params=pltpu.CompilerParams(
            dimension_semantics=("parallel", "parallel"), vmem_limit_bytes=VMEM_LIMIT_BYTES),
        name="post_attn",
    )(x, mla, swa, mod, woa, wob, gmlp, wup, wdown, gfin)


def _prep_layer_weights(w_in, w_qb, w_kvb, w_o):
    o1 = Q_LORA
    o2 = o1 + KV_LORA
    o3 = o2 + MLA_ROPE
    o4 = o3 + SWA_Q_COLS
    o5 = o4 + SWA_KV_COLS
    d = w_in.shape[0]
    kpe = jnp.zeros((d, LANES), w_in.dtype).at[:, MLA_NOPE:MLA_NOPE + MLA_ROPE].set(w_in[:, o2:o3])
    sq = w_in[:, o3:o4].reshape(d, SWA_KV_HEADS, SWA_GROUP, SWA_HEAD_DIM)
    sq = sq.transpose(0, 2, 1, 3).reshape(d, SWA_Q_COLS)
    win = jnp.concatenate([w_in[:, :o2], kpe, sq, w_in[:, o4:]], axis=1).astype(BF16)

    wq = jnp.pad(w_qb, ((0, 0), (0, 0), (0, LANES - MLA_NOPE - MLA_ROPE)))
    wq = wq.reshape(Q_LORA, MLA_QK_COLS).astype(BF16)
    wk = jnp.pad(w_kvb[:, :, :MLA_NOPE], ((0, 0), (0, 0), (0, LANES - MLA_NOPE)))
    wk = wk.reshape(KV_LORA, MLA_QK_COLS).astype(BF16)
    wv = w_kvb[:, :, MLA_NOPE:].reshape(KV_LORA, MLA_V_COLS).T.astype(BF16)

    woa = w_o[:MLA_V_COLS].astype(BF16)
    wob = w_o[MLA_V_COLS:].reshape(SWA_KV_HEADS, SWA_GROUP, SWA_HEAD_DIM, -1)
    wob = wob.transpose(1, 0, 2, 3).reshape(SWA_Q_COLS, -1).astype(BF16)
    return win, wq, wk, wv, woa, wob


def kernel(x, c, w_ada, b_ada, norm_mix_g, w_in, g_qa, w_qb, g_kva, w_kvb, sinks,
           w_o, norm_mlp_g, w_up, w_down, final_g):
    depth = w_ada.shape[0]
    b = x.shape[0]
    assert depth == 1, "the final rmsnorm is fused into the single layer's post-attention call"
    for l in range(depth):
        mod = _ada_call(c, w_ada[l], b_ada[l]).reshape(b, N_MOD, D_MODEL)
        win, wq, wk, wv, woa, wob = _prep_layer_weights(w_in[l], w_qb[l], w_kvb[l], w_o[l])
        qf, kf, vt, sq, sk, sv = _pre_call(
            x, mod, norm_mix_g[l][None], win, g_qa[l][None], wq, g_kva[l][None], wk, wv, tm=512, tk=256)
        mla = _mla_call(qf, kf, vt, tq=256)
        swa = _swa_call(sinks[l][None], sq, sk, sv)
        x = _post_call(x, mla, swa, mod, woa, wob, norm_mlp_g[l][None],
                       w_up[l].astype(BF16), w_down[l].astype(BF16), final_g[None], tm=512, ff_chunk=1024)
    return x
```

```python
import functools

import numpy as np
import jax
import jax.numpy as jnp
from jax import lax
from jax.experimental import pallas as pl
from jax.experimental.pallas import tpu as pltpu

D_MODEL = 1024
SEQ = 2048
MLA_HEADS = 8
MLA_NOPE = 64
MLA_ROPE = 32
MLA_V = 64
Q_LORA = 384
KV_LORA = 256
ROPE_THETA = 10000.0
SWA_HEADS = 8
SWA_KV_HEADS = 2
SWA_GROUP = SWA_HEADS // SWA_KV_HEADS
SWA_HEAD_DIM = 64
WINDOW = 128
D_FF = 4 * D_MODEL
EPS = 1e-6
N_MOD = 6

LANES = 128
HALF = LANES // 2
BF16_SUBLANES = 16
MLA_QK_COLS = MLA_HEADS * LANES
MLA_V_COLS = MLA_HEADS * MLA_V
SWA_Q_COLS = SWA_HEADS * SWA_HEAD_DIM
SWA_KV_COLS = SWA_KV_HEADS * SWA_HEAD_DIM
C_QLAT = 0
C_KVLAT = C_QLAT + Q_LORA
C_KPE = C_KVLAT + KV_LORA
C_SQ = C_KPE + LANES
C_SK = C_SQ + SWA_Q_COLS
IN_COLS_PAD = C_SK + SWA_KV_COLS

VMEM_LIMIT_BYTES = 56 * 1024 * 1024

BF16 = jnp.bfloat16
F32 = jnp.float32


def _const_spec(shape):
    nd = len(shape)
    return pl.BlockSpec(shape, lambda *_: (0,) * nd, pipeline_mode=pl.Buffered(1))


def _rms(x):
    return x * lax.rsqrt(jnp.mean(x * x, axis=-1, keepdims=True) + EPS)


def _ada_kernel(c_ref, w_ref, b_ref, o_ref):
    c = c_ref[...]
    s = c / (1.0 + jnp.exp(-c))
    o_ref[...] = jnp.dot(s.astype(BF16), w_ref[...].astype(BF16), preferred_element_type=F32) + b_ref[...]


def _ada_call(c, w_ada, b_ada):
    b = c.shape[0]
    n = w_ada.shape[1]
    tn = D_MODEL
    return pl.pallas_call(
        _ada_kernel,
        grid=(n // tn,),
        in_specs=[
            pl.BlockSpec((b, D_MODEL), lambda j: (0, 0)),
            pl.BlockSpec((D_MODEL, tn), lambda j: (0, j)),
            pl.BlockSpec((1, tn), lambda j: (0, j)),
        ],
        out_specs=pl.BlockSpec((b, tn), lambda j: (0, j)),
        out_shape=jax.ShapeDtypeStruct((b, n), F32),
        compiler_params=pltpu.CompilerParams(dimension_semantics=("parallel",)),
        name="ada_mod",
    )(c, w_ada, b_ada.reshape(1, n))


def _rope_group(xg, rope_c, rope_s1, rope_s2):
    up = pltpu.roll(xg, LANES - MLA_ROPE // 2, 1)
    down = pltpu.roll(xg, MLA_ROPE // 2, 1)
    return xg * rope_c + up * rope_s1 + down * rope_s2


def _pre_kernel(x_ref, mod_ref, gmix_ref, win_ref, wsvt_ref, gqa_ref, wq_ref, gkva_ref, wk_ref, wv_ref,
                ropeq_ref, ropek_ref,
                qf_ref, kf_ref, v_ref, sq_ref, sk_ref, sv_ref):
    x = x_ref[0]
    mod = mod_ref[0]
    sh1 = mod[0:1]
    sc1 = mod[1:2]
    h = (_rms(x) * gmix_ref[...] * (1.0 + sc1) + sh1).astype(BF16)
    proj = jnp.dot(h, win_ref[...], preferred_element_type=F32)

    for g in range(SWA_GROUP):
        sq_ref[0, g] = (proj[:, C_SQ + g * LANES:C_SQ + (g + 1) * LANES] * (SWA_HEAD_DIM ** -0.5)).astype(BF16)
    swa_k = proj[:, C_SK:IN_COLS_PAD]
    low = lax.broadcasted_iota(jnp.int32, swa_k.shape, 1) < HALF
    sk_ref[0, :, :LANES] = jnp.where(low, swa_k, 0.0).astype(BF16)
    sk_ref[0, :, LANES:] = jnp.where(low, 0.0, swa_k).astype(BF16)
    svt = lax.dot_general(wsvt_ref[...], h, (((1,), (1,)), ((), ())), preferred_element_type=F32).astype(BF16)
    for jb in range(sv_ref.shape[1]):
        sv_ref[0, jb] = svt[:, jb * WINDOW:(jb + 1) * WINDOW]

    qn = (_rms(proj[:, C_QLAT:C_KVLAT]) * gqa_ref[...]).astype(BF16)
    kvn = (_rms(proj[:, C_KVLAT:C_KPE]) * gkva_ref[...]).astype(BF16)
    q = jnp.dot(qn, wq_ref[...], preferred_element_type=F32)
    kn = jnp.dot(kvn, wk_ref[...], preferred_element_type=F32)
    vt = lax.dot_general(wv_ref[...], kvn, (((1,), (1,)), ((), ())), preferred_element_type=F32).astype(BF16)
    tk = v_ref.shape[-1]
    for hp in range(MLA_HEADS // 2):
        for jb in range(v_ref.shape[2]):
            v_ref[0, hp, jb] = vt[hp * LANES:(hp + 1) * LANES, jb * tk:(jb + 1) * tk]

    qc, qs1, qs2 = ropeq_ref[0], ropeq_ref[1], ropeq_ref[2]
    kc, ks1, ks2 = ropek_ref[0], ropek_ref[1], ropek_ref[2]
    kpe = _rope_group(proj[:, C_KPE:C_SQ], kc, ks1, ks2)
    for hd in range(MLA_HEADS):
        sl = slice(hd * LANES, (hd + 1) * LANES)
        qf_ref[0, :, sl] = _rope_group(q[:, sl], qc, qs1, qs2).astype(BF16)
        kf_ref[0, :, sl] = (kn[:, sl] + kpe).astype(BF16)


def _rope_tables(scale_q):
    half = MLA_ROPE // 2
    freqs = ROPE_THETA ** (-np.arange(0, MLA_ROPE, 2, dtype=np.float64) / MLA_ROPE)
    ang = np.arange(SEQ, dtype=np.float64)[:, None] * freqs[None, :]
    cos, sin = np.cos(ang), np.sin(ang)
    keep = np.zeros((SEQ, LANES))
    up = np.zeros((SEQ, LANES))
    down = np.zeros((SEQ, LANES))
    keep[:, :MLA_NOPE] = 1.0
    keep[:, MLA_NOPE:MLA_NOPE + half] = cos
    keep[:, MLA_NOPE + half:MLA_NOPE + MLA_ROPE] = cos
    up[:, MLA_NOPE:MLA_NOPE + half] = -sin
    down[:, MLA_NOPE + half:MLA_NOPE + MLA_ROPE] = sin
    return (np.stack([keep, up, down]) * scale_q).astype(np.float32)


def _pre_call(x, mod, gmix, win, wsvt, gqa, wq, gkva, wk, wv, tm, tk):
    b, s, d = x.shape
    pairs = MLA_HEADS // 2
    ropeq = jnp.asarray(_rope_tables((MLA_NOPE + MLA_ROPE) ** -0.5 * np.log2(np.e)))
    ropek = jnp.asarray(_rope_tables(1.0))
    row = lambda bi, i: (bi, i, 0)
    out_shapes = (
        jax.ShapeDtypeStruct((b, s, MLA_QK_COLS), BF16),
        jax.ShapeDtypeStruct((b, s, MLA_QK_COLS), BF16),
        jax.ShapeDtypeStruct((b, pairs, s // tk, LANES, tk), BF16),
        jax.ShapeDtypeStruct((b, SWA_GROUP, s, LANES), BF16),
        jax.ShapeDtypeStruct((b, s, SWA_KV_HEADS * LANES), BF16),
        jax.ShapeDtypeStruct((b, s // WINDOW, SWA_KV_COLS, WINDOW), BF16),
    )
    out_specs = [
        pl.BlockSpec((1, tm, MLA_QK_COLS), row),
        pl.BlockSpec((1, tm, MLA_QK_COLS), row),
        pl.BlockSpec((1, pairs, tm // tk, LANES, tk), lambda bi, i: (bi, 0, i, 0, 0)),
        pl.BlockSpec((1, SWA_GROUP, tm, LANES), lambda bi, i: (bi, 0, i, 0)),
        pl.BlockSpec((1, tm, SWA_KV_HEADS * LANES), row),
        pl.BlockSpec((1, tm // WINDOW, SWA_KV_COLS, WINDOW), lambda bi, i: (bi, i, 0, 0)),
    ]
    return pl.pallas_call(
        _pre_kernel,
        grid=(b, s // tm),
        in_specs=[
            pl.BlockSpec((1, tm, d), row),
            pl.BlockSpec((1, N_MOD, d), lambda bi, i: (bi, 0, 0)),
            _const_spec((1, d)),
            _const_spec(win.shape),
            _const_spec(wsvt.shape),
            _const_spec((1, Q_LORA)),
            _const_spec(wq.shape),
            _const_spec((1, KV_LORA)),
            _const_spec(wk.shape),
            _const_spec(wv.shape),
            pl.BlockSpec((3, tm, LANES), lambda bi, i: (0, i, 0)),
            pl.BlockSpec((3, tm, LANES), lambda bi, i: (0, i, 0)),
        ],
        out_specs=tuple(out_specs),
        out_shape=out_shapes,
        compiler_params=pltpu.CompilerParams(
            dimension_semantics=("parallel", "parallel"), vmem_limit_bytes=VMEM_LIMIT_BYTES),
        name="pre_attn",
    )(x, mod, gmix, win, wsvt, gqa, wq, gkva, wk, wv, ropeq, ropek)


def _mla_kernel(q_ref, k_ref, vt_ref, o_ref, *, tq):
    i = pl.program_id(2)
    for nblk in range(1, k_ref.shape[1] // tq + 1):
        pl.when(i == nblk - 1)(functools.partial(_mla_blocks, q_ref, k_ref, vt_ref, o_ref, nblk=nblk, tq=tq))


def _mla_blocks(q_ref, k_ref, vt_ref, o_ref, *, nblk, tq):
    ki = lax.broadcasted_iota(jnp.int32, (tq, tq), 0)
    qi = lax.broadcasted_iota(jnp.int32, (tq, tq), 1)
    causal = ki <= qi
    heads = range(q_ref.shape[2] // LANES)
    qs = [q_ref[0, :, h * LANES:(h + 1) * LANES] for h in heads]

    def scores(j):
        ks = [k_ref[0, j * tq:(j + 1) * tq, h * LANES:(h + 1) * LANES] for h in heads]
        return [lax.dot_general(ks[h], qs[h], (((1,), (1,)), ((), ())), preferred_element_type=F32)
                for h in heads]

    ones = jnp.ones((BF16_SUBLANES, tq), BF16)
    state = [(jnp.full((1, tq), -jnp.inf, F32), jnp.zeros((MLA_V + BF16_SUBLANES, tq), F32)) for _ in heads]
    st_next = scores(0)
    for j in range(nblk):
        st_cur = st_next
        if j + 1 < nblk:
            st_next = scores(j + 1)
        for h in heads:
            m, acc = state[h]
            st = st_cur[h]
            if j == nblk - 1:
                st = jnp.where(causal, st, -jnp.inf)
            vt = vt_ref[0, h // 2, j, (h % 2) * MLA_V:(h % 2 + 1) * MLA_V, :]
            vt = jnp.concatenate([vt, ones], axis=0)
            m_new = jnp.maximum(m, jnp.max(st, axis=0, keepdims=True))
            alpha = jnp.exp2(m - m_new)
            pt = jnp.exp2(st - m_new).astype(BF16)
            acc_new = alpha * acc + jnp.dot(vt, pt, preferred_element_type=F32)
            state[h] = (m_new, acc_new)
    for p in range(len(heads) // 2):
        pair = [state[2 * p][1], state[2 * p + 1][1]]
        out_t = jnp.concatenate([acc[:MLA_V] / acc[MLA_V:MLA_V + 1] for acc in pair], axis=0)
        o_ref[0, :, p * LANES:(p + 1) * LANES] = out_t.T.astype(BF16)


def _mla_call(qf, kf, vt, tq, hps):
    b, s, _ = qf.shape
    pairs = MLA_HEADS // 2
    assert vt.shape == (b, pairs, s // tq, LANES, tq) and pairs % hps == 0
    return pl.pallas_call(
        functools.partial(_mla_kernel, tq=tq),
        grid=(b, pairs // hps, s // tq),
        in_specs=[
            pl.BlockSpec((1, tq, 2 * hps * LANES), lambda bi, g, i: (bi, i, g)),
            pl.BlockSpec((1, s, 2 * hps * LANES), lambda bi, g, i: (bi, 0, g)),
            pl.BlockSpec((1, hps, s // tq, LANES, tq), lambda bi, g, i: (bi, g, 0, 0, 0)),
        ],
        out_specs=pl.BlockSpec((1, tq, hps * LANES), lambda bi, g, i: (bi, i, g)),
        out_shape=jax.ShapeDtypeStruct((b, s, MLA_V_COLS), BF16),
        compiler_params=pltpu.CompilerParams(
            dimension_semantics=("parallel", "parallel", "arbitrary"), vmem_limit_bytes=VMEM_LIMIT_BYTES),
        name="mla_attn",
    )(qf, kf, vt)


def _alibi_slope(h):
    return 2.0 ** (-8.0 * (h + 1) / SWA_HEADS)


def _swa_bias():
    k = np.arange(2 * WINDOW)[:, None]
    q = np.arange(WINDOW)[None, :]
    dist = q + WINDOW - k
    valid = (dist >= 0) & (dist < WINDOW)
    out = np.empty((SWA_KV_HEADS, 2 * WINDOW, SWA_GROUP * WINDOW), np.float32)
    for kv in range(SWA_KV_HEADS):
        for g in range(SWA_GROUP):
            slope = np.float32(_alibi_slope(kv * SWA_GROUP + g))
            out[kv, :, g * WINDOW:(g + 1) * WINDOW] = np.where(valid, -slope * dist.astype(np.float32), -np.inf)
    return out


def _swa_kernel(sink_ref, bias_ref, q_ref, k_ref, vt_ref, o_ref):
    i = pl.program_id(1)
    nsub = q_ref.shape[2] // WINDOW
    ones = jnp.ones((BF16_SUBLANES, 2 * WINDOW), BF16)
    chains = [(c, kv) for c in range(nsub) for kv in range(SWA_KV_HEADS)]

    def blocks(c):
        qb = i * nsub + c
        return jnp.maximum(qb - 1, 0), qb

    def scores(c, kv):
        blk_a, blk_b = blocks(c)
        cols = slice(kv * LANES, (kv + 1) * LANES)
        k_win = jnp.concatenate([k_ref[0, pl.ds(pl.multiple_of(blk_a * WINDOW, WINDOW), WINDOW), cols],
                                 k_ref[0, pl.ds(pl.multiple_of(blk_b * WINDOW, WINDOW), WINDOW), cols]], axis=0)
        q_all = q_ref[0, :, c * WINDOW:(c + 1) * WINDOW, :].reshape(SWA_GROUP * WINDOW, LANES)
        return lax.dot_general(k_win, q_all, (((1,), (1,)), ((), ())), preferred_element_type=F32)

    outs = {}
    st_next = scores(*chains[0])
    for n, (c, kv) in enumerate(chains):
        st = st_next
        if n + 1 < len(chains):
            st_next = scores(*chains[n + 1])
        blk_a, blk_b = blocks(c)
        st = st + bias_ref[kv]
        if c == 0:
            gone = jnp.where(blk_b == 0, -jnp.inf, 0.0)
            st = jnp.concatenate([st[:WINDOW] + gone, st[WINDOW:]], axis=0)
        sink = sink_ref[kv:kv + 1, :]
        m = jnp.maximum(jnp.max(st, axis=0, keepdims=True), sink)
        pt = jnp.exp(st - m).astype(BF16)
        rows = slice(kv * SWA_HEAD_DIM, (kv + 1) * SWA_HEAD_DIM)
        vt = jnp.concatenate([vt_ref[0, blk_a, rows, :], vt_ref[0, blk_b, rows, :]], axis=1)
        res = jnp.dot(jnp.concatenate([vt, ones], axis=0), pt, preferred_element_type=F32)
        denom = res[SWA_HEAD_DIM:SWA_HEAD_DIM + 1] + jnp.exp(sink - m)
        outs[kv] = res[:SWA_HEAD_DIM] / denom
        if kv == SWA_KV_HEADS - 1:
            for g in range(SWA_GROUP):
                gs = slice(g * WINDOW, (g + 1) * WINDOW)
                out_t = jnp.concatenate([outs[0][:, gs], outs[1][:, gs]], axis=0)
                o_ref[0, c * WINDOW:(c + 1) * WINDOW, g * LANES:(g + 1) * LANES] = out_t.T.astype(BF16)


def _swa_call(sinks, sq, sk, svt, tstep):
    b, _, s, _ = sq.shape
    bias = jnp.asarray(_swa_bias())
    sink_rows = jnp.repeat(sinks.reshape(SWA_KV_HEADS, SWA_GROUP), WINDOW, axis=1)
    return pl.pallas_call(
        _swa_kernel,
        grid=(b, s // tstep),
        in_specs=[
            _const_spec(sink_rows.shape),
            _const_spec(bias.shape),
            pl.BlockSpec((1, SWA_GROUP, tstep, LANES), lambda bi, i: (bi, 0, i, 0)),
            pl.BlockSpec((1, s, SWA_KV_HEADS * LANES), lambda bi, i: (bi, 0, 0)),
            pl.BlockSpec((1, s // WINDOW, SWA_KV_COLS, WINDOW), lambda bi, i: (bi, 0, 0, 0)),
        ],
        out_specs=pl.BlockSpec((1, tstep, SWA_Q_COLS), lambda bi, i: (bi, i, 0)),
        out_shape=jax.ShapeDtypeStruct((b, s, SWA_Q_COLS), BF16),
        compiler_params=pltpu.CompilerParams(
            dimension_semantics=("parallel", "arbitrary"), vmem_limit_bytes=VMEM_LIMIT_BYTES),
        name="swa_attn",
    )(sink_rows, bias, sq, sk, svt)


def _post_kernel(x_ref, mla_ref, swa_ref, mod_ref, woa_ref, wob_ref, gmlp_ref, wup_ref, wdown_ref, gfin_ref,
                 o_ref, *, ff_chunk):
    x = x_ref[0]
    mod = mod_ref[0]
    g1, sh2, sc2, g2 = mod[2:3], mod[3:4], mod[4:5], mod[5:6]
    attn = (jnp.dot(mla_ref[0], woa_ref[...], preferred_element_type=F32)
            + jnp.dot(swa_ref[0], wob_ref[...], preferred_element_type=F32))
    x1 = x + g1 * attn
    h2 = (_rms(x1) * gmlp_ref[...] * (1.0 + sc2) + sh2).astype(BF16)
    acc = jnp.zeros_like(x1)
    for c in range(D_FF // ff_chunk):
        cs = slice(c * ff_chunk, (c + 1) * ff_chunk)
        u = jnp.maximum(jnp.dot(h2, wup_ref[:, cs], preferred_element_type=F32), 0.0)
        acc = acc + jnp.dot((u * u).astype(BF16), wdown_ref[cs, :], preferred_element_type=F32)
    x2 = x1 + g2 * acc
    o_ref[0] = _rms(x2) * gfin_ref[...]


def _post_call(x, mla, swa, mod, woa, wob, gmlp, wup, wdown, gfin, tm, ff_chunk):
    b, s, d = x.shape
    row = lambda bi, i: (bi, i, 0)
    return pl.pallas_call(
        functools.partial(_post_kernel, ff_chunk=ff_chunk),
        grid=(b, s // tm),
        in_specs=[
            pl.BlockSpec((1, tm, d), row),
            pl.BlockSpec((1, tm, MLA_V_COLS), row),
            pl.BlockSpec((1, tm, SWA_Q_COLS), row),
            pl.BlockSpec((1, N_MOD, d), lambda bi, i: (bi, 0, 0)),
            _const_spec(woa.shape),
            _const_spec(wob.shape),
            _const_spec((1, d)),
            _const_spec(wup.shape),
            _const_spec(wdown.shape),
            _const_spec((1, d)),
        ],
        out_specs=pl.BlockSpec((1, tm, d), row),
        out_shape=jax.ShapeDtypeStruct((b, s, d), F32),
        compiler_params=pltpu.CompilerParams(
            dimension_semantics=("parallel", "parallel"), vmem_limit_bytes=VMEM_LIMIT_BYTES),
        name="post_attn",
    )(x, mla, swa, mod, woa, wob, gmlp, wup, wdown, gfin)


def _prep_layer_weights(w_in, w_qb, w_kvb, w_o):
    o1 = Q_LORA
    o2 = o1 + KV_LORA
    o3 = o2 + MLA_ROPE
    o4 = o3 + SWA_Q_COLS
    o5 = o4 + SWA_KV_COLS
    d = w_in.shape[0]
    kpe = jnp.zeros((d, LANES), w_in.dtype).at[:, MLA_NOPE:MLA_NOPE + MLA_ROPE].set(w_in[:, o2:o3])
    sq = w_in[:, o3:o4].reshape(d, SWA_KV_HEADS, SWA_GROUP, SWA_HEAD_DIM)
    sq = sq.transpose(0, 2, 1, 3).reshape(d, SWA_Q_COLS)
    win = jnp.concatenate([w_in[:, :o2], kpe, sq, w_in[:, o4:o5]], axis=1).astype(BF16)
    wsvt = w_in[:, o5:].T.astype(BF16)

    wq = jnp.pad(w_qb, ((0, 0), (0, 0), (0, LANES - MLA_NOPE - MLA_ROPE)))
    wq = wq.reshape(Q_LORA, MLA_QK_COLS).astype(BF16)
    wk = jnp.pad(w_kvb[:, :, :MLA_NOPE], ((0, 0), (0, 0), (0, LANES - MLA_NOPE)))
    wk = wk.reshape(KV_LORA, MLA_QK_COLS).astype(BF16)
    wv = w_kvb[:, :, MLA_NOPE:].reshape(KV_LORA, MLA_V_COLS).T.astype(BF16)

    woa = w_o[:MLA_V_COLS].astype(BF16)
    wob = w_o[MLA_V_COLS:].reshape(SWA_KV_HEADS, SWA_GROUP, SWA_HEAD_DIM, -1)
    wob = wob.transpose(1, 0, 2, 3).reshape(SWA_Q_COLS, -1).astype(BF16)
    return win, wsvt, wq, wk, wv, woa, wob


def kernel(x, c, w_ada, b_ada, norm_mix_g, w_in, g_qa, w_qb, g_kva, w_kvb, sinks,
           w_o, norm_mlp_g, w_up, w_down, final_g):
    depth = w_ada.shape[0]
    b = x.shape[0]
    assert depth == 1, "the final rmsnorm is fused into the single layer's post-attention call"
    for l in range(depth):
        mod = _ada_call(c, w_ada[l], b_ada[l]).reshape(b, N_MOD, D_MODEL)
        win, wsvt, wq, wk, wv, woa, wob = _prep_layer_weights(w_in[l], w_qb[l], w_kvb[l], w_o[l])
        qf, kf, vt, sq, sk, svt = _pre_call(
            x, mod, norm_mix_g[l][None], win, wsvt, g_qa[l][None], wq, g_kva[l][None], wk, wv, tm=512, tk=256)
        mla = _mla_call(qf, kf, vt, tq=256, hps=2)
        swa = _swa_call(sinks[l], sq, sk, svt, tstep=512)
        x = _post_call(x, mla, swa, mod, woa, wob, norm_mlp_g[l][None],
                       w_up[l].astype(BF16), w_down[l].astype(BF16), final_g[None], tm=512, ff_chunk=1024)
    return x
```

```python
import functools

import numpy as np
import jax
import jax.numpy as jnp
from jax import lax
from jax.experimental import pallas as pl
from jax.experimental.pallas import tpu as pltpu

D_MODEL = 1024
SEQ = 2048
MLA_HEADS = 8
MLA_NOPE = 64
MLA_ROPE = 32
MLA_V = 64
Q_LORA = 384
KV_LORA = 256
ROPE_THETA = 10000.0
SWA_HEADS = 8
SWA_KV_HEADS = 2
SWA_GROUP = SWA_HEADS // SWA_KV_HEADS
SWA_HEAD_DIM = 64
WINDOW = 128
D_FF = 4 * D_MODEL
EPS = 1e-6
N_MOD = 6

LANES = 128
HALF = LANES // 2
BF16_SUBLANES = 16
MLA_QK_COLS = MLA_HEADS * LANES
MLA_V_COLS = MLA_HEADS * MLA_V
SWA_Q_COLS = SWA_HEADS * SWA_HEAD_DIM
SWA_KV_COLS = SWA_KV_HEADS * SWA_HEAD_DIM
C_QLAT = 0
C_KVLAT = C_QLAT + Q_LORA
C_KPE = C_KVLAT + KV_LORA
C_SQ = C_KPE + LANES
C_SK = C_SQ + SWA_Q_COLS
IN_COLS_PAD = C_SK + SWA_KV_COLS

VMEM_LIMIT_BYTES = 56 * 1024 * 1024

BF16 = jnp.bfloat16
F32 = jnp.float32


def _const_spec(shape):
    nd = len(shape)
    return pl.BlockSpec(shape, lambda *_: (0,) * nd, pipeline_mode=pl.Buffered(1))


def _rms(x):
    return x * lax.rsqrt(jnp.mean(x * x, axis=-1, keepdims=True) + EPS)


def _ada_kernel(c_ref, w_ref, b_ref, o_ref):
    c = c_ref[...]
    s = c / (1.0 + jnp.exp(-c))
    o_ref[...] = jnp.dot(s.astype(BF16), w_ref[...].astype(BF16), preferred_element_type=F32) + b_ref[...]


def _ada_call(c, w_ada, b_ada):
    b = c.shape[0]
    n = w_ada.shape[1]
    tn = D_MODEL
    return pl.pallas_call(
        _ada_kernel,
        grid=(n // tn,),
        in_specs=[
            pl.BlockSpec((b, D_MODEL), lambda j: (0, 0)),
            pl.BlockSpec((D_MODEL, tn), lambda j: (0, j)),
            pl.BlockSpec((1, tn), lambda j: (0, j)),
        ],
        out_specs=pl.BlockSpec((b, tn), lambda j: (0, j)),
        out_shape=jax.ShapeDtypeStruct((b, n), F32),
        compiler_params=pltpu.CompilerParams(dimension_semantics=("parallel",)),
        name="ada_mod",
    )(c, w_ada, b_ada.reshape(1, n))


def _rope_group(xg, rope_c, rope_s1, rope_s2):
    up = pltpu.roll(xg, LANES - MLA_ROPE // 2, 1)
    down = pltpu.roll(xg, MLA_ROPE // 2, 1)
    return xg * rope_c + up * rope_s1 + down * rope_s2


def _pre_kernel(x_ref, mod_ref, gmix_ref, win_ref, wsvt_ref, gqa_ref, wq_ref, gkva_ref, wk_ref, wv_ref,
                ropeq_ref, ropek_ref,
                qf_ref, kf_ref, v_ref, sq_ref, sk_ref, sv_ref):
    x = x_ref[0]
    mod = mod_ref[0]
    sh1 = mod[0:1]
    sc1 = mod[1:2]
    h = (_rms(x) * gmix_ref[...] * (1.0 + sc1) + sh1).astype(BF16)
    proj = jnp.dot(h, win_ref[...], preferred_element_type=F32)

    for g in range(SWA_GROUP):
        sq_ref[0, g] = (proj[:, C_SQ + g * LANES:C_SQ + (g + 1) * LANES] * (SWA_HEAD_DIM ** -0.5)).astype(BF16)
    swa_k = proj[:, C_SK:IN_COLS_PAD]
    low = lax.broadcasted_iota(jnp.int32, swa_k.shape, 1) < HALF
    sk_ref[0, :, :LANES] = jnp.where(low, swa_k, 0.0).astype(BF16)
    sk_ref[0, :, LANES:] = jnp.where(low, 0.0, swa_k).astype(BF16)
    svt = lax.dot_general(wsvt_ref[...], h, (((1,), (1,)), ((), ())), preferred_element_type=F32).astype(BF16)
    for jb in range(sv_ref.shape[1]):
        sv_ref[0, jb] = svt[:, jb * WINDOW:(jb + 1) * WINDOW]

    qn = (_rms(proj[:, C_QLAT:C_KVLAT]) * gqa_ref[...]).astype(BF16)
    kvn = (_rms(proj[:, C_KVLAT:C_KPE]) * gkva_ref[...]).astype(BF16)
    q = jnp.dot(qn, wq_ref[...], preferred_element_type=F32)
    kn = jnp.dot(kvn, wk_ref[...], preferred_element_type=F32)
    vt = lax.dot_general(wv_ref[...], kvn, (((1,), (1,)), ((), ())), preferred_element_type=F32).astype(BF16)
    tk = v_ref.shape[-1]
    for hp in range(MLA_HEADS // 2):
        for jb in range(v_ref.shape[2]):
            v_ref[0, hp, jb] = vt[hp * LANES:(hp + 1) * LANES, jb * tk:(jb + 1) * tk]

    qc, qs1, qs2 = ropeq_ref[0], ropeq_ref[1], ropeq_ref[2]
    kc, ks1, ks2 = ropek_ref[0], ropek_ref[1], ropek_ref[2]
    kpe = _rope_group(proj[:, C_KPE:C_SQ], kc, ks1, ks2)
    for hd in range(MLA_HEADS):
        sl = slice(hd * LANES, (hd + 1) * LANES)
        qf_ref[0, :, sl] = _rope_group(q[:, sl], qc, qs1, qs2).astype(BF16)
        kf_ref[0, :, sl] = (kn[:, sl] + kpe).astype(BF16)


def _rope_tables(scale_q):
    half = MLA_ROPE // 2
    freqs = ROPE_THETA ** (-np.arange(0, MLA_ROPE, 2, dtype=np.float64) / MLA_ROPE)
    ang = np.arange(SEQ, dtype=np.float64)[:, None] * freqs[None, :]
    cos, sin = np.cos(ang), np.sin(ang)
    keep = np.zeros((SEQ, LANES))
    up = np.zeros((SEQ, LANES))
    down = np.zeros((SEQ, LANES))
    keep[:, :MLA_NOPE] = 1.0
    keep[:, MLA_NOPE:MLA_NOPE + half] = cos
    keep[:, MLA_NOPE + half:MLA_NOPE + MLA_ROPE] = cos
    up[:, MLA_NOPE:MLA_NOPE + half] = -sin
    down[:, MLA_NOPE + half:MLA_NOPE + MLA_ROPE] = sin
    return (np.stack([keep, up, down]) * scale_q).astype(np.float32)


def _pre_call(x, mod, gmix, win, wsvt, gqa, wq, gkva, wk, wv, tm, tk):
    b, s, d = x.shape
    pairs = MLA_HEADS // 2
    ropeq = jnp.asarray(_rope_tables((MLA_NOPE + MLA_ROPE) ** -0.5 * np.log2(np.e)))
    ropek = jnp.asarray(_rope_tables(1.0))
    row = lambda bi, i: (bi, i, 0)
    out_shapes = (
        jax.ShapeDtypeStruct((b, s, MLA_QK_COLS), BF16),
        jax.ShapeDtypeStruct((b, s, MLA_QK_COLS), BF16),
        jax.ShapeDtypeStruct((b, pairs, s // tk, LANES, tk), BF16),
        jax.ShapeDtypeStruct((b, SWA_GROUP, s, LANES), BF16),
        jax.ShapeDtypeStruct((b, s, SWA_KV_HEADS * LANES), BF16),
        jax.ShapeDtypeStruct((b, s // WINDOW, SWA_KV_COLS, WINDOW), BF16),
    )
    out_specs = [
        pl.BlockSpec((1, tm, MLA_QK_COLS), row),
        pl.BlockSpec((1, tm, MLA_QK_COLS), row),
        pl.BlockSpec((1, pairs, tm // tk, LANES, tk), lambda bi, i: (bi, 0, i, 0, 0)),
        pl.BlockSpec((1, SWA_GROUP, tm, LANES), lambda bi, i: (bi, 0, i, 0)),
        pl.BlockSpec((1, tm, SWA_KV_HEADS * LANES), row),
        pl.BlockSpec((1, tm // WINDOW, SWA_KV_COLS, WINDOW), lambda bi, i: (bi, i, 0, 0)),
    ]
    return pl.pallas_call(
        _pre_kernel,
        grid=(b, s // tm),
        in_specs=[
            pl.BlockSpec((1, tm, d), row),
            pl.BlockSpec((1, N_MOD, d), lambda bi, i: (bi, 0, 0)),
            _const_spec((1, d)),
            _const_spec(win.shape),
            _const_spec(wsvt.shape),
            _const_spec((1, Q_LORA)),
            _const_spec(wq.shape),
            _const_spec((1, KV_LORA)),
            _const_spec(wk.shape),
            _const_spec(wv.shape),
            pl.BlockSpec((3, tm, LANES), lambda bi, i: (0, i, 0)),
            pl.BlockSpec((3, tm, LANES), lambda bi, i: (0, i, 0)),
        ],
        out_specs=tuple(out_specs),
        out_shape=out_shapes,
        compiler_params=pltpu.CompilerParams(
            dimension_semantics=("parallel", "parallel"), vmem_limit_bytes=VMEM_LIMIT_BYTES),
        name="pre_attn",
    )(x, mod, gmix, win, wsvt, gqa, wq, gkva, wk, wv, ropeq, ropek)


def _mla_kernel(q_ref, k_ref, vt_ref, o_ref, *, tq):
    i = pl.program_id(2)
    for nblk in range(1, k_ref.shape[1] // tq + 1):
        pl.when(i == nblk - 1)(functools.partial(_mla_blocks, q_ref, k_ref, vt_ref, o_ref, nblk=nblk, tq=tq))


def _mla_blocks(q_ref, k_ref, vt_ref, o_ref, *, nblk, tq):
    ki = lax.broadcasted_iota(jnp.int32, (tq, tq), 0)
    qi = lax.broadcasted_iota(jnp.int32, (tq, tq), 1)
    causal = ki <= qi
    heads = range(q_ref.shape[2] // LANES)
    qs = [q_ref[0, :, h * LANES:(h + 1) * LANES] for h in heads]

    def scores(j):
        ks = [k_ref[0, j * tq:(j + 1) * tq, h * LANES:(h + 1) * LANES] for h in heads]
        return [lax.dot_general(ks[h], qs[h], (((1,), (1,)), ((), ())), preferred_element_type=F32)
                for h in heads]

    ones = jnp.ones((BF16_SUBLANES, tq), BF16)
    state = [(jnp.full((1, tq), -jnp.inf, F32), jnp.zeros((MLA_V + BF16_SUBLANES, tq), F32)) for _ in heads]
    st_next = scores(0)
    for j in range(nblk):
        st_cur = st_next
        if j + 1 < nblk:
            st_next = scores(j + 1)
        for h in heads:
            m, acc = state[h]
            st = st_cur[h]
            if j == nblk - 1:
                st = jnp.where(causal, st, -jnp.inf)
            vt = vt_ref[0, h // 2, j, (h % 2) * MLA_V:(h % 2 + 1) * MLA_V, :]
            vt = jnp.concatenate([vt, ones], axis=0)
            m_new = jnp.maximum(m, jnp.max(st, axis=0, keepdims=True))
            alpha = jnp.exp2(m - m_new)
            pt = jnp.exp2(st - m_new).astype(BF16)
            acc_new = alpha * acc + jnp.dot(vt, pt, preferred_element_type=F32)
            state[h] = (m_new, acc_new)
    for p in range(len(heads) // 2):
        pair = [state[2 * p][1], state[2 * p + 1][1]]
        out_t = jnp.concatenate([acc[:MLA_V] / acc[MLA_V:MLA_V + 1] for acc in pair], axis=0)
        o_ref[0, :, p * LANES:(p + 1) * LANES] = out_t.T.astype(BF16)


def _mla_call(qf, kf, vt, tq, hps):
    b, s, _ = qf.shape
    pairs = MLA_HEADS // 2
    assert vt.shape == (b, pairs, s // tq, LANES, tq) and pairs % hps == 0
    return pl.pallas_call(
        functools.partial(_mla_kernel, tq=tq),
        grid=(b, pairs // hps, s // tq),
        in_specs=[
            pl.BlockSpec((1, tq, 2 * hps * LANES), lambda bi, g, i: (bi, i, g)),
            pl.BlockSpec((1, s, 2 * hps * LANES), lambda bi, g, i: (bi, 0, g)),
            pl.BlockSpec((1, hps, s // tq, LANES, tq), lambda bi, g, i: (bi, g, 0, 0, 0)),
        ],
        out_specs=pl.BlockSpec((1, tq, hps * LANES), lambda bi, g, i: (bi, i, g)),
        out_shape=jax.ShapeDtypeStruct((b, s, MLA_V_COLS), BF16),
        compiler_params=pltpu.CompilerParams(
            dimension_semantics=("parallel", "parallel", "arbitrary"), vmem_limit_bytes=VMEM_LIMIT_BYTES),
        name="mla_attn",
    )(qf, kf, vt)


def _alibi_slope(h):
    return 2.0 ** (-8.0 * (h + 1) / SWA_HEADS)


SWA_STACK = 2
SWA_LOOKAHEAD = 4


def _swa_bias():
    k = np.arange(2 * WINDOW)[:, None]
    q = np.arange(WINDOW)[None, :]
    dist = q + WINDOW - k
    valid = (dist >= 0) & (dist < WINDOW)
    out = np.empty((SWA_KV_HEADS, 2 * WINDOW, SWA_GROUP * WINDOW), np.float32)
    for kv in range(SWA_KV_HEADS):
        for g in range(SWA_GROUP):
            slope = np.float32(_alibi_slope(kv * SWA_GROUP + g))
            out[kv, :, g * WINDOW:(g + 1) * WINDOW] = np.where(valid, -slope * dist.astype(np.float32), -np.inf)
    return out


def _swa_kernel(sink_ref, bias_ref, q_ref, kp_ref, k_ref, vtp_ref, vt_ref, o_ref):
    i = pl.program_id(1)
    nsub = q_ref.shape[2] // WINDOW
    ones = jnp.ones((BF16_SUBLANES, 2 * WINDOW), BF16)
    width = SWA_STACK * WINDOW
    gone = jnp.where(i == 0, -jnp.inf, 0.0)
    chains = [(c, g0, kv) for c in range(nsub) for g0 in range(0, SWA_GROUP, SWA_STACK)
              for kv in range(SWA_KV_HEADS)]

    def scores(c, g0, kv):
        cols = slice(kv * LANES, (kv + 1) * LANES)
        if c == 0:
            k_win = jnp.concatenate([kp_ref[0, :, cols], k_ref[0, :WINDOW, cols]], axis=0)
        else:
            k_win = k_ref[0, (c - 1) * WINDOW:(c + 1) * WINDOW, cols]
        q_st = q_ref[0, g0:g0 + SWA_STACK, c * WINDOW:(c + 1) * WINDOW, :].reshape(width, LANES)
        st = lax.dot_general(k_win, q_st, (((1,), (1,)), ((), ())), preferred_element_type=F32)
        return st + bias_ref[kv, :, g0 * WINDOW:g0 * WINDOW + width]

    outs = {}
    done = []
    pending = [scores(*ch) for ch in chains[:SWA_LOOKAHEAD]]
    for n, (c, g0, kv) in enumerate(chains):
        st = pending.pop(0)
        if n + SWA_LOOKAHEAD < len(chains):
            pending.append(scores(*chains[n + SWA_LOOKAHEAD]))
        rows = slice(kv * SWA_HEAD_DIM, (kv + 1) * SWA_HEAD_DIM)
        if c == 0:
            vt = jnp.concatenate([vtp_ref[0, 0, rows, :], vt_ref[0, 0, rows, :]], axis=1)
        else:
            vt = jnp.concatenate([vt_ref[0, c - 1, rows, :], vt_ref[0, c, rows, :]], axis=1)
        if c == 0:
            st = jnp.concatenate([st[:WINDOW] + gone, st[WINDOW:]], axis=0)
        sink = sink_ref[kv:kv + 1, g0 * WINDOW:g0 * WINDOW + width]
        m = jnp.maximum(jnp.max(st, axis=0, keepdims=True), sink)
        pt = jnp.exp(st - m).astype(BF16)
        res = jnp.dot(jnp.concatenate([vt, ones], axis=0), pt, preferred_element_type=F32)
        denom = res[SWA_HEAD_DIM:SWA_HEAD_DIM + 1] + jnp.exp(sink - m)
        outs[kv] = res[:SWA_HEAD_DIM] / denom
        if kv == SWA_KV_HEADS - 1:
            for gg in range(SWA_STACK):
                gs = slice(gg * WINDOW, (gg + 1) * WINDOW)
                out_t = jnp.concatenate([outs[0][:, gs], outs[1][:, gs]], axis=0)
                done.append((c, g0 + gg, out_t))
    for c, g, out_t in done:
        o_ref[0, c * WINDOW:(c + 1) * WINDOW, g * LANES:(g + 1) * LANES] = out_t.T.astype(BF16)


def _swa_call(sinks, sq, sk, svt, tstep):
    b, _, s, _ = sq.shape
    nsub = tstep // WINDOW
    bias = jnp.asarray(_swa_bias())
    sink_rows = jnp.repeat(sinks.reshape(SWA_KV_HEADS, SWA_GROUP), WINDOW, axis=1)
    prev_blk = lambda i: jnp.maximum(i * nsub - 1, 0)
    return pl.pallas_call(
        _swa_kernel,
        grid=(b, s // tstep),
        in_specs=[
            _const_spec(sink_rows.shape),
            _const_spec(bias.shape),
            pl.BlockSpec((1, SWA_GROUP, tstep, LANES), lambda bi, i: (bi, 0, i, 0)),
            pl.BlockSpec((1, WINDOW, SWA_KV_HEADS * LANES), lambda bi, i: (bi, prev_blk(i), 0)),
            pl.BlockSpec((1, tstep, SWA_KV_HEADS * LANES), lambda bi, i: (bi, i, 0)),
            pl.BlockSpec((1, 1, SWA_KV_COLS, WINDOW), lambda bi, i: (bi, prev_blk(i), 0, 0)),
            pl.BlockSpec((1, nsub, SWA_KV_COLS, WINDOW), lambda bi, i: (bi, i, 0, 0)),
        ],
        out_specs=pl.BlockSpec((1, tstep, SWA_Q_COLS), lambda bi, i: (bi, i, 0)),
        out_shape=jax.ShapeDtypeStruct((b, s, SWA_Q_COLS), BF16),
        compiler_params=pltpu.CompilerParams(
            dimension_semantics=("parallel", "arbitrary"), vmem_limit_bytes=VMEM_LIMIT_BYTES),
        name="swa_attn",
    )(sink_rows, bias, sq, sk, sk, svt, svt)


def _post_kernel(x_ref, mla_ref, swa_ref, mod_ref, woa_ref, wob_ref, gmlp_ref, wup_ref, wdown_ref, gfin_ref,
                 o_ref, *, ff_chunk):
    x = x_ref[0]
    mod = mod_ref[0]
    g1, sh2, sc2, g2 = mod[2:3], mod[3:4], mod[4:5], mod[5:6]
    attn = (jnp.dot(mla_ref[0], woa_ref[...], preferred_element_type=F32)
            + jnp.dot(swa_ref[0], wob_ref[...], preferred_element_type=F32))
    x1 = x + g1 * attn
    h2 = (_rms(x1) * gmlp_ref[...] * (1.0 + sc2) + sh2).astype(BF16)
    acc = jnp.zeros_like(x1)
    for c in range(D_FF // ff_chunk):
        cs = slice(c * ff_chunk, (c + 1) * ff_chunk)
        u = jnp.maximum(jnp.dot(h2, wup_ref[:, cs], preferred_element_type=F32), 0.0)
        acc = acc + jnp.dot((u * u).astype(BF16), wdown_ref[cs, :], preferred_element_type=F32)
    x2 = x1 + g2 * acc
    o_ref[0] = _rms(x2) * gfin_ref[...]


def _post_call(x, mla, swa, mod, woa, wob, gmlp, wup, wdown, gfin, tm, ff_chunk):
    b, s, d = x.shape
    row = lambda bi, i: (bi, i, 0)
    return pl.pallas_call(
        functools.partial(_post_kernel, ff_chunk=ff_chunk),
        grid=(b, s // tm),
        in_specs=[
            pl.BlockSpec((1, tm, d), row),
            pl.BlockSpec((1, tm, MLA_V_COLS), row),
            pl.BlockSpec((1, tm, SWA_Q_COLS), row),
            pl.BlockSpec((1, N_MOD, d), lambda bi, i: (bi, 0, 0)),
            _const_spec(woa.shape),
            _const_spec(wob.shape),
            _const_spec((1, d)),
            _const_spec(wup.shape),
            _const_spec(wdown.shape),
            _const_spec((1, d)),
        ],
        out_specs=pl.BlockSpec((1, tm, d), row),
        out_shape=jax.ShapeDtypeStruct((b, s, d), F32),
        compiler_params=pltpu.CompilerParams(
            dimension_semantics=("parallel", "parallel"), vmem_limit_bytes=VMEM_LIMIT_BYTES),
        name="post_attn",
    )(x, mla, swa, mod, woa, wob, gmlp, wup, wdown, gfin)


def _prep_layer_weights(w_in, w_qb, w_kvb, w_o):
    o1 = Q_LORA
    o2 = o1 + KV_LORA
    o3 = o2 + MLA_ROPE
    o4 = o3 + SWA_Q_COLS
    o5 = o4 + SWA_KV_COLS
    d = w_in.shape[0]
    kpe = jnp.zeros((d, LANES), w_in.dtype).at[:, MLA_NOPE:MLA_NOPE + MLA_ROPE].set(w_in[:, o2:o3])
    sq = w_in[:, o3:o4].reshape(d, SWA_KV_HEADS, SWA_GROUP, SWA_HEAD_DIM)
    sq = sq.transpose(0, 2, 1, 3).reshape(d, SWA_Q_COLS)
    win = jnp.concatenate([w_in[:, :o2], kpe, sq, w_in[:, o4:o5]], axis=1).astype(BF16)
    wsvt = w_in[:, o5:].T.astype(BF16)

    wq = jnp.pad(w_qb, ((0, 0), (0, 0), (0, LANES - MLA_NOPE - MLA_ROPE)))
    wq = wq.reshape(Q_LORA, MLA_QK_COLS).astype(BF16)
    wk = jnp.pad(w_kvb[:, :, :MLA_NOPE], ((0, 0), (0, 0), (0, LANES - MLA_NOPE)))
    wk = wk.reshape(KV_LORA, MLA_QK_COLS).astype(BF16)
    wv = w_kvb[:, :, MLA_NOPE:].reshape(KV_LORA, MLA_V_COLS).T.astype(BF16)

    woa = w_o[:MLA_V_COLS].astype(BF16)
    wob = w_o[MLA_V_COLS:].reshape(SWA_KV_HEADS, SWA_GROUP, SWA_HEAD_DIM, -1)
    wob = wob.transpose(1, 0, 2, 3).reshape(SWA_Q_COLS, -1).astype(BF16)
    return win, wsvt, wq, wk, wv, woa, wob


def kernel(x, c, w_ada, b_ada, norm_mix_g, w_in, g_qa, w_qb, g_kva, w_kvb, sinks,
           w_o, norm_mlp_g, w_up, w_down, final_g):
    depth = w_ada.shape[0]
    b = x.shape[0]
    assert depth == 1, "the final rmsnorm is fused into the single layer's post-attention call"
    for l in range(depth):
        mod = _ada_call(c, w_ada[l], b_ada[l]).reshape(b, N_MOD, D_MODEL)
        win, wsvt, wq, wk, wv, woa, wob = _prep_layer_weights(w_in[l], w_qb[l], w_kvb[l], w_o[l])
        qf, kf, vt, sq, sk, svt = _pre_call(
            x, mod, norm_mix_g[l][None], win, wsvt, g_qa[l][None], wq, g_kva[l][None], wk, wv, tm=512, tk=256)
        mla = _mla_call(qf, kf, vt, tq=256, hps=2)
        swa = _swa_call(sinks[l], sq, sk, svt, tstep=512)
        x = _post_call(x, mla, swa, mod, woa, wob, norm_mlp_g[l][None],
                       w_up[l].astype(BF16), w_down[l].astype(BF16), final_g[None], tm=512, ff_chunk=1024)
    return x
```

```python
import functools

import numpy as np
import jax
import jax.numpy as jnp
from jax import lax
from jax.experimental import pallas as pl
from jax.experimental.pallas import tpu as pltpu

D_MODEL = 1024
SEQ = 2048
MLA_HEADS = 8
MLA_NOPE = 64
MLA_ROPE = 32
MLA_V = 64
Q_LORA = 384
KV_LORA = 256
ROPE_THETA = 10000.0
SWA_HEADS = 8
SWA_KV_HEADS = 2
SWA_GROUP = SWA_HEADS // SWA_KV_HEADS
SWA_HEAD_DIM = 64
WINDOW = 128
D_FF = 4 * D_MODEL
EPS = 1e-6
N_MOD = 6

LANES = 128
HALF = LANES // 2
BF16_SUBLANES = 16
MLA_QK_COLS = MLA_HEADS * LANES
MLA_V_COLS = MLA_HEADS * MLA_V
SWA_Q_COLS = SWA_HEADS * SWA_HEAD_DIM
SWA_KV_COLS = SWA_KV_HEADS * SWA_HEAD_DIM
C_QLAT = 0
C_KVLAT = C_QLAT + Q_LORA
C_KPE = C_KVLAT + KV_LORA
C_SQ = C_KPE + LANES
C_SK = C_SQ + SWA_Q_COLS
IN_COLS_PAD = C_SK + SWA_KV_COLS

VMEM_LIMIT_BYTES = 56 * 1024 * 1024

BF16 = jnp.bfloat16
F32 = jnp.float32


def _const_spec(shape):
    nd = len(shape)
    return pl.BlockSpec(shape, lambda *_: (0,) * nd, pipeline_mode=pl.Buffered(1))


def _rms(x):
    return x * lax.rsqrt(jnp.mean(x * x, axis=-1, keepdims=True) + EPS)


def _ada_kernel(c_ref, w_ref, b_ref, o_ref):
    c = c_ref[...]
    s = c / (1.0 + jnp.exp(-c))
    o_ref[...] = jnp.dot(s.astype(BF16), w_ref[...].astype(BF16), preferred_element_type=F32) + b_ref[...]


def _ada_call(c, w_ada, b_ada):
    b = c.shape[0]
    n = w_ada.shape[1]
    tn = D_MODEL
    return pl.pallas_call(
        _ada_kernel,
        grid=(n // tn,),
        in_specs=[
            pl.BlockSpec((b, D_MODEL), lambda j: (0, 0)),
            pl.BlockSpec((D_MODEL, tn), lambda j: (0, j)),
            pl.BlockSpec((1, tn), lambda j: (0, j)),
        ],
        out_specs=pl.BlockSpec((b, tn), lambda j: (0, j)),
        out_shape=jax.ShapeDtypeStruct((b, n), F32),
        compiler_params=pltpu.CompilerParams(dimension_semantics=("parallel",)),
        name="ada_mod",
    )(c, w_ada, b_ada.reshape(1, n))


def _rope_group(xg, rope_c, rope_s1, rope_s2):
    up = pltpu.roll(xg, LANES - MLA_ROPE // 2, 1)
    down = pltpu.roll(xg, MLA_ROPE // 2, 1)
    return xg * rope_c + up * rope_s1 + down * rope_s2


def _pre_kernel(x_ref, mod_ref, gmix_ref, win_ref, wsvt_ref, gqa_ref, wq_ref, gkva_ref, wk_ref, wv_ref,
                ropeq_ref, ropek_ref,
                qf_ref, kf_ref, v_ref, sq_ref, sk_ref, sv_ref):
    x = x_ref[0]
    mod = mod_ref[0]
    sh1 = mod[0:1]
    sc1 = mod[1:2]
    h = (_rms(x) * gmix_ref[...] * (1.0 + sc1) + sh1).astype(BF16)
    proj = jnp.dot(h, win_ref[...], preferred_element_type=F32)

    for g in range(SWA_GROUP):
        sq_ref[0, g] = (proj[:, C_SQ + g * LANES:C_SQ + (g + 1) * LANES] * (SWA_HEAD_DIM ** -0.5)).astype(BF16)
    swa_k = proj[:, C_SK:IN_COLS_PAD]
    low = lax.broadcasted_iota(jnp.int32, swa_k.shape, 1) < HALF
    sk_ref[0, :, :LANES] = jnp.where(low, swa_k, 0.0).astype(BF16)
    sk_ref[0, :, LANES:] = jnp.where(low, 0.0, swa_k).astype(BF16)
    svt = lax.dot_general(wsvt_ref[...], h, (((1,), (1,)), ((), ())), preferred_element_type=F32).astype(BF16)
    for jb in range(sv_ref.shape[1]):
        sv_ref[0, jb] = svt[:, jb * WINDOW:(jb + 1) * WINDOW]

    qn = (_rms(proj[:, C_QLAT:C_KVLAT]) * gqa_ref[...]).astype(BF16)
    kvn = (_rms(proj[:, C_KVLAT:C_KPE]) * gkva_ref[...]).astype(BF16)
    q = jnp.dot(qn, wq_ref[...], preferred_element_type=F32)
    kn = jnp.dot(kvn, wk_ref[...], preferred_element_type=F32)
    vt = lax.dot_general(wv_ref[...], kvn, (((1,), (1,)), ((), ())), preferred_element_type=F32).astype(BF16)
    tk = v_ref.shape[-1]
    for hp in range(MLA_HEADS // 2):
        for jb in range(v_ref.shape[2]):
            v_ref[0, hp, jb] = vt[hp * LANES:(hp + 1) * LANES, jb * tk:(jb + 1) * tk]

    qc, qs1, qs2 = ropeq_ref[0], ropeq_ref[1], ropeq_ref[2]
    kc, ks1, ks2 = ropek_ref[0], ropek_ref[1], ropek_ref[2]
    kpe = _rope_group(proj[:, C_KPE:C_SQ], kc, ks1, ks2)
    for hd in range(MLA_HEADS):
        sl = slice(hd * LANES, (hd + 1) * LANES)
        qf_ref[0, :, sl] = _rope_group(q[:, sl], qc, qs1, qs2).astype(BF16)
        kf_ref[0, :, sl] = (kn[:, sl] + kpe).astype(BF16)


def _rope_tables(scale_q):
    half = MLA_ROPE // 2
    freqs = ROPE_THETA ** (-np.arange(0, MLA_ROPE, 2, dtype=np.float64) / MLA_ROPE)
    ang = np.arange(SEQ, dtype=np.float64)[:, None] * freqs[None, :]
    cos, sin = np.cos(ang), np.sin(ang)
    keep = np.zeros((SEQ, LANES))
    up = np.zeros((SEQ, LANES))
    down = np.zeros((SEQ, LANES))
    keep[:, :MLA_NOPE] = 1.0
    keep[:, MLA_NOPE:MLA_NOPE + half] = cos
    keep[:, MLA_NOPE + half:MLA_NOPE + MLA_ROPE] = cos
    up[:, MLA_NOPE:MLA_NOPE + half] = -sin
    down[:, MLA_NOPE + half:MLA_NOPE + MLA_ROPE] = sin
    return (np.stack([keep, up, down]) * scale_q).astype(np.float32)


def _pre_call(x, mod, gmix, win, wsvt, gqa, wq, gkva, wk, wv, tm, tk):
    b, s, d = x.shape
    pairs = MLA_HEADS // 2
    ropeq = jnp.asarray(_rope_tables((MLA_NOPE + MLA_ROPE) ** -0.5 * np.log2(np.e)))
    ropek = jnp.asarray(_rope_tables(1.0))
    row = lambda bi, i: (bi, i, 0)
    out_shapes = (
        jax.ShapeDtypeStruct((b, s, MLA_QK_COLS), BF16),
        jax.ShapeDtypeStruct((b, s, MLA_QK_COLS), BF16),
        jax.ShapeDtypeStruct((b, pairs, s // tk, LANES, tk), BF16),
        jax.ShapeDtypeStruct((b, SWA_GROUP, s, LANES), BF16),
        jax.ShapeDtypeStruct((b, s, SWA_KV_HEADS * LANES), BF16),
        jax.ShapeDtypeStruct((b, s // WINDOW, SWA_KV_COLS, WINDOW), BF16),
    )
    out_specs = [
        pl.BlockSpec((1, tm, MLA_QK_COLS), row),
        pl.BlockSpec((1, tm, MLA_QK_COLS), row),
        pl.BlockSpec((1, pairs, tm // tk, LANES, tk), lambda bi, i: (bi, 0, i, 0, 0)),
        pl.BlockSpec((1, SWA_GROUP, tm, LANES), lambda bi, i: (bi, 0, i, 0)),
        pl.BlockSpec((1, tm, SWA_KV_HEADS * LANES), row),
        pl.BlockSpec((1, tm // WINDOW, SWA_KV_COLS, WINDOW), lambda bi, i: (bi, i, 0, 0)),
    ]
    return pl.pallas_call(
        _pre_kernel,
        grid=(b, s // tm),
        in_specs=[
            pl.BlockSpec((1, tm, d), row),
            pl.BlockSpec((1, N_MOD, d), lambda bi, i: (bi, 0, 0)),
            _const_spec((1, d)),
            _const_spec(win.shape),
            _const_spec(wsvt.shape),
            _const_spec((1, Q_LORA)),
            _const_spec(wq.shape),
            _const_spec((1, KV_LORA)),
            _const_spec(wk.shape),
            _const_spec(wv.shape),
            pl.BlockSpec((3, tm, LANES), lambda bi, i: (0, i, 0)),
            pl.BlockSpec((3, tm, LANES), lambda bi, i: (0, i, 0)),
        ],
        out_specs=tuple(out_specs),
        out_shape=out_shapes,
        compiler_params=pltpu.CompilerParams(
            dimension_semantics=("parallel", "parallel"), vmem_limit_bytes=VMEM_LIMIT_BYTES),
        name="pre_attn",
    )(x, mod, gmix, win, wsvt, gqa, wq, gkva, wk, wv, ropeq, ropek)


def _mla_kernel(q_ref, k_ref, vt_ref, o_ref, *, tq, qsets):
    if len(qsets) == 1:
        _mla_blocks(q_ref, k_ref, vt_ref, o_ref, qblocks=qsets[0], tq=tq)
        return
    for v, qblocks in enumerate(qsets):
        pl.when(pl.program_id(2) == v)(
            functools.partial(_mla_blocks, q_ref, k_ref, vt_ref, o_ref, qblocks=qblocks, tq=tq))


def _mla_blocks(q_ref, k_ref, vt_ref, o_ref, *, qblocks, tq):
    ki = lax.broadcasted_iota(jnp.int32, (tq, tq), 0)
    qi = lax.broadcasted_iota(jnp.int32, (tq, tq), 1)
    causal = ki <= qi
    heads = range(q_ref.shape[2] // LANES)
    steps = [(i, j) for i in qblocks for j in range(i + 1)]

    def scores(i, j):
        out = []
        for h in heads:
            cols = slice(h * LANES, (h + 1) * LANES)
            q = q_ref[0, i * tq:(i + 1) * tq, cols]
            k = k_ref[0, j * tq:(j + 1) * tq, cols]
            out.append(lax.dot_general(k, q, (((1,), (1,)), ((), ())), preferred_element_type=F32))
        return out

    ones = jnp.ones((BF16_SUBLANES, tq), BF16)
    st_next = scores(*steps[0])
    for n, (i, j) in enumerate(steps):
        st_cur = st_next
        if n + 1 < len(steps):
            st_next = scores(*steps[n + 1])
        if j == 0:
            state = [(jnp.full((1, tq), -jnp.inf, F32), jnp.zeros((MLA_V + BF16_SUBLANES, tq), F32))
                     for _ in heads]
        for h in heads:
            m, acc = state[h]
            st = st_cur[h]
            if j == i:
                st = jnp.where(causal, st, -jnp.inf)
            vt = vt_ref[0, h // 2, j, (h % 2) * MLA_V:(h % 2 + 1) * MLA_V, :]
            vt = jnp.concatenate([vt, ones], axis=0)
            m_new = jnp.maximum(m, jnp.max(st, axis=0, keepdims=True))
            alpha = jnp.exp2(m - m_new)
            pt = jnp.exp2(st - m_new).astype(BF16)
            acc_new = alpha * acc + jnp.dot(vt, pt, preferred_element_type=F32)
            state[h] = (m_new, acc_new)
        if j == i:
            for p in range(len(heads) // 2):
                pair = [state[2 * p][1], state[2 * p + 1][1]]
                out_t = jnp.concatenate([acc[:MLA_V] / acc[MLA_V:MLA_V + 1] for acc in pair], axis=0)
                o_ref[0, i * tq:(i + 1) * tq, p * LANES:(p + 1) * LANES] = out_t.T.astype(BF16)


def _mla_call(qf, kf, vt, tq, hps, qsets):
    b, s, _ = qf.shape
    pairs = MLA_HEADS // 2
    assert vt.shape == (b, pairs, s // tq, LANES, tq) and pairs % hps == 0
    assert sorted(i for qs in qsets for i in qs) == list(range(s // tq))
    seq_cols = lambda bi, g, v: (bi, 0, g)
    return pl.pallas_call(
        functools.partial(_mla_kernel, tq=tq, qsets=qsets),
        grid=(b, pairs // hps, len(qsets)),
        in_specs=[
            pl.BlockSpec((1, s, 2 * hps * LANES), seq_cols),
            pl.BlockSpec((1, s, 2 * hps * LANES), seq_cols),
            pl.BlockSpec((1, hps, s // tq, LANES, tq), lambda bi, g, v: (bi, g, 0, 0, 0)),
        ],
        out_specs=pl.BlockSpec((1, s, hps * LANES), seq_cols),
        out_shape=jax.ShapeDtypeStruct((b, s, MLA_V_COLS), BF16),
        compiler_params=pltpu.CompilerParams(
            dimension_semantics=("parallel", "parallel", "arbitrary"), vmem_limit_bytes=VMEM_LIMIT_BYTES),
        name="mla_attn",
    )(qf, kf, vt)


def _alibi_slope(h):
    return 2.0 ** (-8.0 * (h + 1) / SWA_HEADS)


SWA_STACK = 2
SWA_LOOKAHEAD = 4


def _swa_bias():
    k = np.arange(2 * WINDOW)[:, None]
    q = np.arange(WINDOW)[None, :]
    dist = q + WINDOW - k
    valid = (dist >= 0) & (dist < WINDOW)
    out = np.empty((SWA_KV_HEADS, 2 * WINDOW, SWA_GROUP * WINDOW), np.float32)
    for kv in range(SWA_KV_HEADS):
        for g in range(SWA_GROUP):
            slope = np.float32(_alibi_slope(kv * SWA_GROUP + g))
            out[kv, :, g * WINDOW:(g + 1) * WINDOW] = np.where(valid, -slope * dist.astype(np.float32), -np.inf)
    return out


def _swa_kernel(sink_ref, bias_ref, q_ref, kp_ref, k_ref, vtp_ref, vt_ref, o_ref):
    i = pl.program_id(1)
    nsub = q_ref.shape[2] // WINDOW
    ones = jnp.ones((BF16_SUBLANES, 2 * WINDOW), BF16)
    width = SWA_STACK * WINDOW
    gone = jnp.where(i == 0, -jnp.inf, 0.0)
    chains = [(c, g0, kv) for c in range(nsub) for g0 in range(0, SWA_GROUP, SWA_STACK)
              for kv in range(SWA_KV_HEADS)]

    def scores(c, g0, kv):
        cols = slice(kv * LANES, (kv + 1) * LANES)
        if c == 0:
            k_win = jnp.concatenate([kp_ref[0, :, cols], k_ref[0, :WINDOW, cols]], axis=0)
        else:
            k_win = k_ref[0, (c - 1) * WINDOW:(c + 1) * WINDOW, cols]
        q_st = q_ref[0, g0:g0 + SWA_STACK, c * WINDOW:(c + 1) * WINDOW, :].reshape(width, LANES)
        st = lax.dot_general(k_win, q_st, (((1,), (1,)), ((), ())), preferred_element_type=F32)
        return st + bias_ref[kv, :, g0 * WINDOW:g0 * WINDOW + width]

    outs = {}
    done = []
    pending = [scores(*ch) for ch in chains[:SWA_LOOKAHEAD]]
    for n, (c, g0, kv) in enumerate(chains):
        st = pending.pop(0)
        if n + SWA_LOOKAHEAD < len(chains):
            pending.append(scores(*chains[n + SWA_LOOKAHEAD]))
        rows = slice(kv * SWA_HEAD_DIM, (kv + 1) * SWA_HEAD_DIM)
        if c == 0:
            vt = jnp.concatenate([vtp_ref[0, 0, rows, :], vt_ref[0, 0, rows, :]], axis=1)
        else:
            vt = jnp.concatenate([vt_ref[0, c - 1, rows, :], vt_ref[0, c, rows, :]], axis=1)
        if c == 0:
            st = jnp.concatenate([st[:WINDOW] + gone, st[WINDOW:]], axis=0)
        sink = sink_ref[kv:kv + 1, g0 * WINDOW:g0 * WINDOW + width]
        m = jnp.maximum(jnp.max(st, axis=0, keepdims=True), sink)
        pt = jnp.exp(st - m).astype(BF16)
        res = jnp.dot(jnp.concatenate([vt, ones], axis=0), pt, preferred_element_type=F32)
        denom = res[SWA_HEAD_DIM:SWA_HEAD_DIM + 1] + jnp.exp(sink - m)
        outs[kv] = res[:SWA_HEAD_DIM] / denom
        if kv == SWA_KV_HEADS - 1:
            for gg in range(SWA_STACK):
                gs = slice(gg * WINDOW, (gg + 1) * WINDOW)
                out_t = jnp.concatenate([outs[0][:, gs], outs[1][:, gs]], axis=0)
                done.append((c, g0 + gg, out_t))
    for c, g, out_t in done:
        o_ref[0, c * WINDOW:(c + 1) * WINDOW, g * LANES:(g + 1) * LANES] = out_t.T.astype(BF16)


def _swa_call(sinks, sq, sk, svt, tstep):
    b, _, s, _ = sq.shape
    nsub = tstep // WINDOW
    bias = jnp.asarray(_swa_bias())
    sink_rows = jnp.repeat(sinks.reshape(SWA_KV_HEADS, SWA_GROUP), WINDOW, axis=1)
    prev_blk = lambda i: jnp.maximum(i * nsub - 1, 0)
    return pl.pallas_call(
        _swa_kernel,
        grid=(b, s // tstep),
        in_specs=[
            _const_spec(sink_rows.shape),
            _const_spec(bias.shape),
            pl.BlockSpec((1, SWA_GROUP, tstep, LANES), lambda bi, i: (bi, 0, i, 0)),
            pl.BlockSpec((1, WINDOW, SWA_KV_HEADS * LANES), lambda bi, i: (bi, prev_blk(i), 0)),
            pl.BlockSpec((1, tstep, SWA_KV_HEADS * LANES), lambda bi, i: (bi, i, 0)),
            pl.BlockSpec((1, 1, SWA_KV_COLS, WINDOW), lambda bi, i: (bi, prev_blk(i), 0, 0)),
            pl.BlockSpec((1, nsub, SWA_KV_COLS, WINDOW), lambda bi, i: (bi, i, 0, 0)),
        ],
        out_specs=pl.BlockSpec((1, tstep, SWA_Q_COLS), lambda bi, i: (bi, i, 0)),
        out_shape=jax.ShapeDtypeStruct((b, s, SWA_Q_COLS), BF16),
        compiler_params=pltpu.CompilerParams(
            dimension_semantics=("parallel", "arbitrary"), vmem_limit_bytes=VMEM_LIMIT_BYTES),
        name="swa_attn",
    )(sink_rows, bias, sq, sk, sk, svt, svt)


def _post_kernel(x_ref, mla_ref, swa_ref, mod_ref, woa_ref, wob_ref, gmlp_ref, wup_ref, wdown_ref, gfin_ref,
                 o_ref, *, ff_chunk):
    x = x_ref[0]
    mod = mod_ref[0]
    g1, sh2, sc2, g2 = mod[2:3], mod[3:4], mod[4:5], mod[5:6]
    attn = (jnp.dot(mla_ref[0], woa_ref[...], preferred_element_type=F32)
            + jnp.dot(swa_ref[0], wob_ref[...], preferred_element_type=F32))
    x1 = x + g1 * attn
    h2 = (_rms(x1) * gmlp_ref[...] * (1.0 + sc2) + sh2).astype(BF16)
    acc = jnp.zeros_like(x1)
    for c in range(D_FF // ff_chunk):
        cs = slice(c * ff_chunk, (c + 1) * ff_chunk)
        u = jnp.maximum(jnp.dot(h2, wup_ref[:, cs], preferred_element_type=F32), 0.0)
        acc = acc + jnp.dot((u * u).astype(BF16), wdown_ref[cs, :], preferred_element_type=F32)
    x2 = x1 + g2 * acc
    o_ref[0] = _rms(x2) * gfin_ref[...]


def _post_call(x, mla, swa, mod, woa, wob, gmlp, wup, wdown, gfin, tm, ff_chunk):
    b, s, d = x.shape
    row = lambda bi, i: (bi, i, 0)
    return pl.pallas_call(
        functools.partial(_post_kernel, ff_chunk=ff_chunk),
        grid=(b, s // tm),
        in_specs=[
            pl.BlockSpec((1, tm, d), row),
            pl.BlockSpec((1, tm, MLA_V_COLS), row),
            pl.BlockSpec((1, tm, SWA_Q_COLS), row),
            pl.BlockSpec((1, N_MOD, d), lambda bi, i: (bi, 0, 0)),
            _const_spec(woa.shape),
            _const_spec(wob.shape),
            _const_spec((1, d)),
            _const_spec(wup.shape),
            _const_spec(wdown.shape),
            _const_spec((1, d)),
        ],
        out_specs=pl.BlockSpec((1, tm, d), row),
        out_shape=jax.ShapeDtypeStruct((b, s, d), F32),
        compiler_params=pltpu.CompilerParams(
            dimension_semantics=("parallel", "parallel"), vmem_limit_bytes=VMEM_LIMIT_BYTES),
        name="post_attn",
    )(x, mla, swa, mod, woa, wob, gmlp, wup, wdown, gfin)


def _prep_layer_weights(w_in, w_qb, w_kvb, w_o):
    o1 = Q_LORA
    o2 = o1 + KV_LORA
    o3 = o2 + MLA_ROPE
    o4 = o3 + SWA_Q_COLS
    o5 = o4 + SWA_KV_COLS
    d = w_in.shape[0]
    kpe = jnp.zeros((d, LANES), w_in.dtype).at[:, MLA_NOPE:MLA_NOPE + MLA_ROPE].set(w_in[:, o2:o3])
    sq = w_in[:, o3:o4].reshape(d, SWA_KV_HEADS, SWA_GROUP, SWA_HEAD_DIM)
    sq = sq.transpose(0, 2, 1, 3).reshape(d, SWA_Q_COLS)
    win = jnp.concatenate([w_in[:, :o2], kpe, sq, w_in[:, o4:o5]], axis=1).astype(BF16)
    wsvt = w_in[:, o5:].T.astype(BF16)

    wq = jnp.pad(w_qb, ((0, 0), (0, 0), (0, LANES - MLA_NOPE - MLA_ROPE)))
    wq = wq.reshape(Q_LORA, MLA_QK_COLS).astype(BF16)
    wk = jnp.pad(w_kvb[:, :, :MLA_NOPE], ((0, 0), (0, 0), (0, LANES - MLA_NOPE)))
    wk = wk.reshape(KV_LORA, MLA_QK_COLS).astype(BF16)
    wv = w_kvb[:, :, MLA_NOPE:].reshape(KV_LORA, MLA_V_COLS).T.astype(BF16)

    woa = w_o[:MLA_V_COLS].astype(BF16)
    wob = w_o[MLA_V_COLS:].reshape(SWA_KV_HEADS, SWA_GROUP, SWA_HEAD_DIM, -1)
    wob = wob.transpose(1, 0, 2, 3).reshape(SWA_Q_COLS, -1).astype(BF16)
    return win, wsvt, wq, wk, wv, woa, wob


def kernel(x, c, w_ada, b_ada, norm_mix_g, w_in, g_qa, w_qb, g_kva, w_kvb, sinks,
           w_o, norm_mlp_g, w_up, w_down, final_g):
    depth = w_ada.shape[0]
    b = x.shape[0]
    assert depth == 1, "the final rmsnorm is fused into the single layer's post-attention call"
    for l in range(depth):
        mod = _ada_call(c, w_ada[l], b_ada[l]).reshape(b, N_MOD, D_MODEL)
        win, wsvt, wq, wk, wv, woa, wob = _prep_layer_weights(w_in[l], w_qb[l], w_kvb[l], w_o[l])
        qf, kf, vt, sq, sk, svt = _pre_call(
            x, mod, norm_mix_g[l][None], win, wsvt, g_qa[l][None], wq, g_kva[l][None], wk, wv, tm=512, tk=256)
        mla = _mla_call(qf, kf, vt, tq=256, hps=2, qsets=(tuple(range(SEQ // 256)),))
        swa = _swa_call(sinks[l], sq, sk, svt, tstep=512)
        x = _post_call(x, mla, swa, mod, woa, wob, norm_mlp_g[l][None],
                       w_up[l].astype(BF16), w_down[l].astype(BF16), final_g[None], tm=512, ff_chunk=1024)
    return x
```

```python
import functools

import numpy as np
import jax
import jax.numpy as jnp
from jax import lax
from jax.experimental import pallas as pl
from jax.experimental.pallas import tpu as pltpu

D_MODEL = 1024
SEQ = 2048
MLA_HEADS = 8
MLA_NOPE = 64
MLA_ROPE = 32
MLA_V = 64
Q_LORA = 384
KV_LORA = 256
ROPE_THETA = 10000.0
SWA_HEADS = 8
SWA_KV_HEADS = 2
SWA_GROUP = SWA_HEADS // SWA_KV_HEADS
SWA_HEAD_DIM = 64
WINDOW = 128
D_FF = 4 * D_MODEL
EPS = 1e-6
N_MOD = 6

LANES = 128
HALF = LANES // 2
BF16_SUBLANES = 16
MLA_QK_COLS = MLA_HEADS * LANES
MLA_V_COLS = MLA_HEADS * MLA_V
SWA_Q_COLS = SWA_HEADS * SWA_HEAD_DIM
SWA_KV_COLS = SWA_KV_HEADS * SWA_HEAD_DIM
C_QLAT = 0
C_KVLAT = C_QLAT + Q_LORA
C_KPE = C_KVLAT + KV_LORA
C_SQ = C_KPE + LANES
C_SK = C_SQ + SWA_Q_COLS
C_SV = C_SK + SWA_KV_COLS
IN_COLS_PAD = C_SV + SWA_KV_COLS

VMEM_LIMIT_BYTES = 56 * 1024 * 1024

BF16 = jnp.bfloat16
F32 = jnp.float32


def _const_spec(shape):
    nd = len(shape)
    return pl.BlockSpec(shape, lambda *_: (0,) * nd, pipeline_mode=pl.Buffered(1))


def _rms(x):
    return x * lax.rsqrt(jnp.mean(x * x, axis=-1, keepdims=True) + EPS)


def _ada_kernel(c_ref, w_ref, b_ref, o_ref):
    c = c_ref[...]
    s = c / (1.0 + jnp.exp(-c))
    o_ref[...] = jnp.dot(s.astype(BF16), w_ref[...].astype(BF16), preferred_element_type=F32) + b_ref[...]


def _ada_call(c, w_ada, b_ada):
    b = c.shape[0]
    n = w_ada.shape[1]
    tn = D_MODEL
    return pl.pallas_call(
        _ada_kernel,
        grid=(n // tn,),
        in_specs=[
            pl.BlockSpec((b, D_MODEL), lambda j: (0, 0)),
            pl.BlockSpec((D_MODEL, tn), lambda j: (0, j)),
            pl.BlockSpec((1, tn), lambda j: (0, j)),
        ],
        out_specs=pl.BlockSpec((b, tn), lambda j: (0, j)),
        out_shape=jax.ShapeDtypeStruct((b, n), F32),
        compiler_params=pltpu.CompilerParams(dimension_semantics=("parallel",)),
        name="ada_mod",
    )(c, w_ada, b_ada.reshape(1, n))


def _rope_group(xg, keep, swap):
    up = pltpu.roll(xg, LANES - MLA_ROPE // 2, 1)
    return xg * keep + up * swap


def _pre_kernel(x_ref, mod_ref, gmix_ref, win_ref, gqa_ref, wq_ref, gkva_ref, wk_ref, wv_ref,
                ropeq_ref, ropek_ref,
                qf_ref, kf_ref, v_ref, sq_ref, sk_ref, sv_ref, *, ts):
    for r0 in range(0, x_ref.shape[1], ts):
        _pre_rows(r0, ts, x_ref, mod_ref, gmix_ref, win_ref, gqa_ref, wq_ref, gkva_ref, wk_ref, wv_ref,
                  ropeq_ref, ropek_ref, qf_ref, kf_ref, v_ref, sq_ref, sk_ref, sv_ref)


def _pre_rows(r0, ts, x_ref, mod_ref, gmix_ref, win_ref, gqa_ref, wq_ref, gkva_ref, wk_ref, wv_ref,
              ropeq_ref, ropek_ref, qf_ref, kf_ref, v_ref, sq_ref, sk_ref, sv_ref):
    rows = slice(r0, r0 + ts)
    x = x_ref[0, rows]
    mod = mod_ref[0]
    sh1 = mod[0:1]
    sc1 = mod[1:2]
    h = (_rms(x) * gmix_ref[...] * (1.0 + sc1) + sh1).astype(BF16)
    proj = jnp.dot(h, win_ref[...], preferred_element_type=F32)

    for g in range(SWA_GROUP):
        sq_ref[0, g, rows] = (proj[:, C_SQ + g * LANES:C_SQ + (g + 1) * LANES] * (SWA_HEAD_DIM ** -0.5)).astype(BF16)
    swa_k = proj[:, C_SK:C_SV]
    low = lax.broadcasted_iota(jnp.int32, swa_k.shape, 1) < HALF
    sk_ref[0, rows, :LANES] = jnp.where(low, swa_k, 0.0).astype(BF16)
    sk_ref[0, rows, LANES:] = jnp.where(low, 0.0, swa_k).astype(BF16)
    svt = proj[:, C_SV:IN_COLS_PAD].T.astype(BF16)
    for jb in range(ts // WINDOW):
        sv_ref[0, r0 // WINDOW + jb] = svt[:, jb * WINDOW:(jb + 1) * WINDOW]

    qn = (_rms(proj[:, C_QLAT:C_KVLAT]) * gqa_ref[...]).astype(BF16)
    kvn = (_rms(proj[:, C_KVLAT:C_KPE]) * gkva_ref[...]).astype(BF16)
    q = jnp.dot(qn, wq_ref[...], preferred_element_type=F32)
    kn = jnp.dot(kvn, wk_ref[...], preferred_element_type=F32)
    vt = lax.dot_general(wv_ref[...], kvn, (((1,), (1,)), ((), ())), preferred_element_type=F32).astype(BF16)
    tk = v_ref.shape[-1]
    for hp in range(MLA_HEADS // 2):
        for jb in range(ts // tk):
            v_ref[0, hp, r0 // tk + jb] = vt[hp * LANES:(hp + 1) * LANES, jb * tk:(jb + 1) * tk]

    q_keep, q_swap = ropeq_ref[0, rows], ropeq_ref[1, rows]
    kpe = _rope_group(proj[:, C_KPE:C_SQ], ropek_ref[0, rows], ropek_ref[1, rows])
    nope_lanes = lax.broadcasted_iota(jnp.int32, kpe.shape, 1) < MLA_NOPE
    for hd in range(MLA_HEADS):
        sl = slice(hd * LANES, (hd + 1) * LANES)
        qf_ref[0, rows, sl] = _rope_group(q[:, sl], q_keep, q_swap).astype(BF16)
        kn_pair = kn[:, (hd // 2) * LANES:(hd // 2 + 1) * LANES]
        if hd % 2:
            kn_pair = pltpu.roll(kn_pair, HALF, 1)
        kf_ref[0, rows, sl] = jnp.where(nope_lanes, kn_pair, kpe).astype(BF16)


def _rope_tables(scale_q):
    half = MLA_ROPE // 2
    freqs = ROPE_THETA ** (-np.arange(0, MLA_ROPE, 2, dtype=np.float64) / MLA_ROPE)
    ang = np.arange(SEQ, dtype=np.float64)[:, None] * freqs[None, :]
    cos, sin = np.cos(ang), np.sin(ang)
    keep = np.zeros((SEQ, LANES))
    swap = np.zeros((SEQ, LANES))
    keep[:, :MLA_NOPE] = 1.0
    keep[:, MLA_NOPE:MLA_NOPE + half] = cos
    keep[:, MLA_NOPE + half:MLA_NOPE + MLA_ROPE] = cos
    swap[:, MLA_NOPE:MLA_NOPE + half] = -sin
    swap[:, MLA_NOPE + half:MLA_NOPE + MLA_ROPE] = sin
    return (np.stack([keep, swap]) * scale_q).astype(np.float32)


def _pre_call(x, mod, gmix, win, gqa, wq, gkva, wk, wv, tm, ts, tk):
    b, s, d = x.shape
    pairs = MLA_HEADS // 2
    ropeq = jnp.asarray(_rope_tables((MLA_NOPE + MLA_ROPE) ** -0.5 * np.log2(np.e)))
    ropek = jnp.asarray(_rope_tables(1.0))
    row = lambda bi, i: (bi, i, 0)
    out_shapes = (
        jax.ShapeDtypeStruct((b, s, MLA_QK_COLS), BF16),
        jax.ShapeDtypeStruct((b, s, MLA_QK_COLS), BF16),
        jax.ShapeDtypeStruct((b, pairs, s // tk, LANES, tk), BF16),
        jax.ShapeDtypeStruct((b, SWA_GROUP, s, LANES), BF16),
        jax.ShapeDtypeStruct((b, s, SWA_KV_HEADS * LANES), BF16),
        jax.ShapeDtypeStruct((b, s // WINDOW, SWA_KV_COLS, WINDOW), BF16),
    )
    out_specs = [
        pl.BlockSpec((1, tm, MLA_QK_COLS), row),
        pl.BlockSpec((1, tm, MLA_QK_COLS), row),
        pl.BlockSpec((1, pairs, tm // tk, LANES, tk), lambda bi, i: (bi, 0, i, 0, 0)),
        pl.BlockSpec((1, SWA_GROUP, tm, LANES), lambda bi, i: (bi, 0, i, 0)),
        pl.BlockSpec((1, tm, SWA_KV_HEADS * LANES), row),
        pl.BlockSpec((1, tm // WINDOW, SWA_KV_COLS, WINDOW), lambda bi, i: (bi, i, 0, 0)),
    ]
    return pl.pallas_call(
        functools.partial(_pre_kernel, ts=ts),
        grid=(b, s // tm),
        in_specs=[
            pl.BlockSpec((1, tm, d), row),
            pl.BlockSpec((1, N_MOD, d), lambda bi, i: (bi, 0, 0)),
            _const_spec((1, d)),
            _const_spec(win.shape),
            _const_spec((1, Q_LORA)),
            _const_spec(wq.shape),
            _const_spec((1, KV_LORA)),
            _const_spec(wk.shape),
            _const_spec(wv.shape),
            pl.BlockSpec((2, tm, LANES), lambda bi, i: (0, i, 0)),
            pl.BlockSpec((2, tm, LANES), lambda bi, i: (0, i, 0)),
        ],
        out_specs=tuple(out_specs),
        out_shape=out_shapes,
        compiler_params=pltpu.CompilerParams(
            dimension_semantics=("parallel", "parallel"), vmem_limit_bytes=VMEM_LIMIT_BYTES),
        name="pre_attn",
    )(x, mod, gmix, win, gqa, wq, gkva, wk, wv, ropeq, ropek)


def _mla_kernel(q_ref, k_ref, vt_ref, o_ref, *, tq, qsets):
    if len(qsets) == 1:
        _mla_blocks(q_ref, k_ref, vt_ref, o_ref, qblocks=qsets[0], tq=tq)
        return
    for v, qblocks in enumerate(qsets):
        pl.when(pl.program_id(2) == v)(
            functools.partial(_mla_blocks, q_ref, k_ref, vt_ref, o_ref, qblocks=qblocks, tq=tq))


def _mla_blocks(q_ref, k_ref, vt_ref, o_ref, *, qblocks, tq):
    ki = lax.broadcasted_iota(jnp.int32, (tq, tq), 0)
    qi = lax.broadcasted_iota(jnp.int32, (tq, tq), 1)
    causal = ki <= qi
    heads = range(q_ref.shape[2] // LANES)
    steps = [(i, j) for i in qblocks for j in range(i + 1)]

    def scores(i, j):
        out = []
        for h in heads:
            cols = slice(h * LANES, (h + 1) * LANES)
            q = q_ref[0, i * tq:(i + 1) * tq, cols]
            k = k_ref[0, j * tq:(j + 1) * tq, cols]
            out.append(lax.dot_general(k, q, (((1,), (1,)), ((), ())), preferred_element_type=F32))
        return out

    ones = jnp.ones((BF16_SUBLANES, tq), BF16)
    st_next = scores(*steps[0])
    for n, (i, j) in enumerate(steps):
        st_cur = st_next
        if n + 1 < len(steps):
            st_next = scores(*steps[n + 1])
        if j == 0:
            state = [(jnp.full((1, tq), -jnp.inf, F32), jnp.zeros((MLA_V + BF16_SUBLANES, tq), F32))
                     for _ in heads]
        for h in heads:
            m, acc = state[h]
            st = st_cur[h]
            if j == i:
                st = jnp.where(causal, st, -jnp.inf)
            vt = vt_ref[0, h // 2, j, (h % 2) * MLA_V:(h % 2 + 1) * MLA_V, :]
            vt = jnp.concatenate([vt, ones], axis=0)
            m_new = jnp.maximum(m, jnp.max(st, axis=0, keepdims=True))
            alpha = jnp.exp2(m - m_new)
            pt = jnp.exp2(st - m_new).astype(BF16)
            acc_new = alpha * acc + jnp.dot(vt, pt, preferred_element_type=F32)
            state[h] = (m_new, acc_new)
        if j == i:
            for p in range(len(heads) // 2):
                pair = [state[2 * p][1], state[2 * p + 1][1]]
                out_t = jnp.concatenate([acc[:MLA_V] / acc[MLA_V:MLA_V + 1] for acc in pair], axis=0)
                o_ref[0, i * tq:(i + 1) * tq, p * LANES:(p + 1) * LANES] = out_t.T.astype(BF16)


def _mla_call(qf, kf, vt, tq, hps, qsets):
    b, s, _ = qf.shape
    pairs = MLA_HEADS // 2
    assert vt.shape == (b, pairs, s // tq, LANES, tq) and pairs % hps == 0
    assert sorted(i for qs in qsets for i in qs) == list(range(s // tq))
    seq_cols = lambda bi, g, v: (bi, 0, g)
    return pl.pallas_call(
        functools.partial(_mla_kernel, tq=tq, qsets=qsets),
        grid=(b, pairs // hps, len(qsets)),
        in_specs=[
            pl.BlockSpec((1, s, 2 * hps * LANES), seq_cols),
            pl.BlockSpec((1, s, 2 * hps * LANES), seq_cols),
            pl.BlockSpec((1, hps, s // tq, LANES, tq), lambda bi, g, v: (bi, g, 0, 0, 0)),
        ],
        out_specs=pl.BlockSpec((1, s, hps * LANES), seq_cols),
        out_shape=jax.ShapeDtypeStruct((b, s, MLA_V_COLS), BF16),
        compiler_params=pltpu.CompilerParams(
            dimension_semantics=("parallel", "parallel", "arbitrary"), vmem_limit_bytes=VMEM_LIMIT_BYTES),
        name="mla_attn",
    )(qf, kf, vt)


def _alibi_slope(h):
    return 2.0 ** (-8.0 * (h + 1) / SWA_HEADS)


SWA_STACK = 2
SWA_LOOKAHEAD = 4


def _swa_bias():
    k = np.arange(2 * WINDOW)[:, None]
    q = np.arange(WINDOW)[None, :]
    dist = q + WINDOW - k
    valid = (dist >= 0) & (dist < WINDOW)
    out = np.empty((SWA_KV_HEADS, 2 * WINDOW, SWA_GROUP * WINDOW), np.float32)
    for kv in range(SWA_KV_HEADS):
        for g in range(SWA_GROUP):
            slope = np.float32(_alibi_slope(kv * SWA_GROUP + g))
            out[kv, :, g * WINDOW:(g + 1) * WINDOW] = np.where(valid, -slope * dist.astype(np.float32), -np.inf)
    return out


def _swa_kernel(sink_ref, bias_ref, q_ref, kp_ref, k_ref, vtp_ref, vt_ref, o_ref):
    i = pl.program_id(1)
    nsub = q_ref.shape[2] // WINDOW
    ones = jnp.ones((BF16_SUBLANES, 2 * WINDOW), BF16)
    width = SWA_STACK * WINDOW
    gone = jnp.where(i == 0, -jnp.inf, 0.0)
    chains = [(c, g0, kv) for c in range(nsub) for g0 in range(0, SWA_GROUP, SWA_STACK)
              for kv in range(SWA_KV_HEADS)]

    def scores(c, g0, kv):
        cols = slice(kv * LANES, (kv + 1) * LANES)
        if c == 0:
            k_win = jnp.concatenate([kp_ref[0, :, cols], k_ref[0, :WINDOW, cols]], axis=0)
        else:
            k_win = k_ref[0, (c - 1) * WINDOW:(c + 1) * WINDOW, cols]
        q_st = q_ref[0, g0:g0 + SWA_STACK, c * WINDOW:(c + 1) * WINDOW, :].reshape(width, LANES)
        st = lax.dot_general(k_win, q_st, (((1,), (1,)), ((), ())), preferred_element_type=F32)
        return st + bias_ref[kv, :, g0 * WINDOW:g0 * WINDOW + width]

    outs = {}
    done = []
    pending = [scores(*ch) for ch in chains[:SWA_LOOKAHEAD]]
    for n, (c, g0, kv) in enumerate(chains):
        st = pending.pop(0)
        if n + SWA_LOOKAHEAD < len(chains):
            pending.append(scores(*chains[n + SWA_LOOKAHEAD]))
        rows = slice(kv * SWA_HEAD_DIM, (kv + 1) * SWA_HEAD_DIM)
        if c == 0:
            vt = jnp.concatenate([vtp_ref[0, 0, rows, :], vt_ref[0, 0, rows, :]], axis=1)
        else:
            vt = jnp.concatenate([vt_ref[0, c - 1, rows, :], vt_ref[0, c, rows, :]], axis=1)
        if c == 0:
            st = jnp.concatenate([st[:WINDOW] + gone, st[WINDOW:]], axis=0)
        sink = sink_ref[kv:kv + 1, g0 * WINDOW:g0 * WINDOW + width]
        m = jnp.maximum(jnp.max(st, axis=0, keepdims=True), sink)
        pt = jnp.exp(st - m).astype(BF16)
        res = jnp.dot(jnp.concatenate([vt, ones], axis=0), pt, preferred_element_type=F32)
        denom = res[SWA_HEAD_DIM:SWA_HEAD_DIM + 1] + jnp.exp(sink - m)
        outs[kv] = res[:SWA_HEAD_DIM] / denom
        if kv == SWA_KV_HEADS - 1:
            for gg in range(SWA_STACK):
                gs = slice(gg * WINDOW, (gg + 1) * WINDOW)
                out_t = jnp.concatenate([outs[0][:, gs], outs[1][:, gs]], axis=0)
                done.append((c, g0 + gg, out_t))
    for c, g, out_t in done:
        o_ref[0, c * WINDOW:(c + 1) * WINDOW, g * LANES:(g + 1) * LANES] = out_t.T.astype(BF16)


def _swa_call(sinks, sq, sk, svt, tstep):
    b, _, s, _ = sq.shape
    nsub = tstep // WINDOW
    bias = jnp.asarray(_swa_bias())
    sink_rows = jnp.repeat(sinks.reshape(SWA_KV_HEADS, SWA_GROUP), WINDOW, axis=1)
    prev_blk = lambda i: jnp.maximum(i * nsub - 1, 0)
    return pl.pallas_call(
        _swa_kernel,
        grid=(b, s // tstep),
        in_specs=[
            _const_spec(sink_rows.shape),
            _const_spec(bias.shape),
            pl.BlockSpec((1, SWA_GROUP, tstep, LANES), lambda bi, i: (bi, 0, i, 0)),
            pl.BlockSpec((1, WINDOW, SWA_KV_HEADS * LANES), lambda bi, i: (bi, prev_blk(i), 0)),
            pl.BlockSpec((1, tstep, SWA_KV_HEADS * LANES), lambda bi, i: (bi, i, 0)),
            pl.BlockSpec((1, 1, SWA_KV_COLS, WINDOW), lambda bi, i: (bi, prev_blk(i), 0, 0)),
            pl.BlockSpec((1, nsub, SWA_KV_COLS, WINDOW), lambda bi, i: (bi, i, 0, 0)),
        ],
        out_specs=pl.BlockSpec((1, tstep, SWA_Q_COLS), lambda bi, i: (bi, i, 0)),
        out_shape=jax.ShapeDtypeStruct((b, s, SWA_Q_COLS), BF16),
        compiler_params=pltpu.CompilerParams(
            dimension_semantics=("parallel", "arbitrary"), vmem_limit_bytes=VMEM_LIMIT_BYTES),
        name="swa_attn",
    )(sink_rows, bias, sq, sk, sk, svt, svt)


def _post_kernel(x_ref, mla_ref, swa_ref, mod_ref, woa_ref, wob_ref, gmlp_ref, wup_ref, wdown_ref, gfin_ref,
                 o_ref, *, ff_chunk):
    x = x_ref[0]
    mod = mod_ref[0]
    g1, sh2, sc2, g2 = mod[2:3], mod[3:4], mod[4:5], mod[5:6]
    attn = (jnp.dot(mla_ref[0], woa_ref[...], preferred_element_type=F32)
            + jnp.dot(swa_ref[0], wob_ref[...], preferred_element_type=F32))
    x1 = x + g1 * attn
    h2 = (_rms(x1) * gmlp_ref[...] * (1.0 + sc2) + sh2).astype(BF16)
    acc = jnp.zeros_like(x1)
    for c in range(D_FF // ff_chunk):
        cs = slice(c * ff_chunk, (c + 1) * ff_chunk)
        u = jnp.maximum(jnp.dot(h2, wup_ref[:, cs], preferred_element_type=F32), 0.0)
        acc = acc + jnp.dot((u * u).astype(BF16), wdown_ref[cs, :], preferred_element_type=F32)
    x2 = x1 + g2 * acc
    o_ref[0] = _rms(x2) * gfin_ref[...]


def _post_call(x, mla, swa, mod, woa, wob, gmlp, wup, wdown, gfin, tm, ff_chunk):
    b, s, d = x.shape
    row = lambda bi, i: (bi, i, 0)
    return pl.pallas_call(
        functools.partial(_post_kernel, ff_chunk=ff_chunk),
        grid=(b, s // tm),
        in_specs=[
            pl.BlockSpec((1, tm, d), row),
            pl.BlockSpec((1, tm, MLA_V_COLS), row),
            pl.BlockSpec((1, tm, SWA_Q_COLS), row),
            pl.BlockSpec((1, N_MOD, d), lambda bi, i: (bi, 0, 0)),
            _const_spec(woa.shape),
            _const_spec(wob.shape),
            _const_spec((1, d)),
            _const_spec(wup.shape),
            _const_spec(wdown.shape),
            _const_spec((1, d)),
        ],
        out_specs=pl.BlockSpec((1, tm, d), row),
        out_shape=jax.ShapeDtypeStruct((b, s, d), F32),
        compiler_params=pltpu.CompilerParams(
            dimension_semantics=("parallel", "parallel"), vmem_limit_bytes=VMEM_LIMIT_BYTES),
        name="post_attn",
    )(x, mla, swa, mod, woa, wob, gmlp, wup, wdown, gfin)


def _prep_layer_weights(w_in, w_qb, w_kvb, w_o):
    o1 = Q_LORA
    o2 = o1 + KV_LORA
    o3 = o2 + MLA_ROPE
    o4 = o3 + SWA_Q_COLS
    d = w_in.shape[0]
    half = MLA_ROPE // 2
    pad = LANES - MLA_NOPE - MLA_ROPE - half
    kpe = jnp.concatenate([jnp.zeros((d, MLA_NOPE), w_in.dtype), w_in[:, o2:o3], w_in[:, o2:o2 + half],
                           jnp.zeros((d, pad), w_in.dtype)], axis=1)
    sq = w_in[:, o3:o4].reshape(d, SWA_KV_HEADS, SWA_GROUP, SWA_HEAD_DIM)
    sq = sq.transpose(0, 2, 1, 3).reshape(d, SWA_Q_COLS)
    win = jnp.concatenate([w_in[:, :o2], kpe, sq, w_in[:, o4:]], axis=1).astype(BF16)

    wq = jnp.concatenate([w_qb, w_qb[:, :, MLA_NOPE:MLA_NOPE + half],
                          jnp.zeros(w_qb.shape[:2] + (pad,), w_qb.dtype)], axis=2)
    wq = wq.reshape(Q_LORA, MLA_QK_COLS).astype(BF16)
    wk = w_kvb[:, :, :MLA_NOPE].reshape(KV_LORA, MLA_HEADS * MLA_NOPE).astype(BF16)
    wv = w_kvb[:, :, MLA_NOPE:].reshape(KV_LORA, MLA_V_COLS).T.astype(BF16)

    woa = w_o[:MLA_V_COLS].astype(BF16)
    wob = w_o[MLA_V_COLS:].reshape(SWA_KV_HEADS, SWA_GROUP, SWA_HEAD_DIM, -1)
    wob = wob.transpose(1, 0, 2, 3).reshape(SWA_Q_COLS, -1).astype(BF16)
    return win, wq, wk, wv, woa, wob


def kernel(x, c, w_ada, b_ada, norm_mix_g, w_in, g_qa, w_qb, g_kva, w_kvb, sinks,
           w_o, norm_mlp_g, w_up, w_down, final_g):
    depth = w_ada.shape[0]
    b = x.shape[0]
    assert depth == 1, "the final rmsnorm is fused into the single layer's post-attention call"
    for l in range(depth):
        mod = _ada_call(c, w_ada[l], b_ada[l]).reshape(b, N_MOD, D_MODEL)
        win, wq, wk, wv, woa, wob = _prep_layer_weights(w_in[l], w_qb[l], w_kvb[l], w_o[l])
        qf, kf, vt, sq, sk, svt = _pre_call(
            x, mod, norm_mix_g[l][None], win, g_qa[l][None], wq, g_kva[l][None], wk, wv,
            tm=1024, ts=512, tk=256)
        mla = _mla_call(qf, kf, vt, tq=256, hps=2, qsets=(tuple(range(SEQ // 256)),))
        swa = _swa_call(sinks[l], sq, sk, svt, tstep=512)
        x = _post_call(x, mla, swa, mod, woa, wob, norm_mlp_g[l][None],
                       w_up[l].astype(BF16), w_down[l].astype(BF16), final_g[None], tm=512, ff_chunk=1024)
    return x
```

```python
import functools

import numpy as np
import jax
import jax.numpy as jnp
from jax import lax
from jax.experimental import pallas as pl
from jax.experimental.pallas import tpu as pltpu

D_MODEL = 1024
SEQ = 2048
MLA_HEADS = 8
MLA_NOPE = 64
MLA_ROPE = 32
MLA_V = 64
Q_LORA = 384
KV_LORA = 256
ROPE_THETA = 10000.0
SWA_HEADS = 8
SWA_KV_HEADS = 2
SWA_GROUP = SWA_HEADS // SWA_KV_HEADS
SWA_HEAD_DIM = 64
WINDOW = 128
D_FF = 4 * D_MODEL
EPS = 1e-6
N_MOD = 6

LANES = 128
HALF = LANES // 2
BF16_SUBLANES = 16
MLA_QK_COLS = MLA_HEADS * LANES
MLA_V_COLS = MLA_HEADS * MLA_V
SWA_Q_COLS = SWA_HEADS * SWA_HEAD_DIM
SWA_KV_COLS = SWA_KV_HEADS * SWA_HEAD_DIM
C_QLAT = 0
C_KVLAT = C_QLAT + Q_LORA
C_KPE = C_KVLAT + KV_LORA
C_SQ = C_KPE + LANES
C_SK = C_SQ + SWA_Q_COLS
C_SV = C_SK + SWA_KV_COLS
IN_COLS_PAD = C_SV + SWA_KV_COLS

VMEM_LIMIT_BYTES = 56 * 1024 * 1024

LOG2E = float(np.log2(np.e))
MLA_Q_SCALE = (MLA_NOPE + MLA_ROPE) ** -0.5 * LOG2E
SWA_Q_SCALE = SWA_HEAD_DIM ** -0.5 * LOG2E

BF16 = jnp.bfloat16
F32 = jnp.float32


def _const_spec(shape):
    nd = len(shape)
    return pl.BlockSpec(shape, lambda *_: (0,) * nd, pipeline_mode=pl.Buffered(1))


def _rms(x):
    return x * lax.rsqrt(jnp.mean(x * x, axis=-1, keepdims=True) + EPS)


def _ada_kernel(c_ref, w_ref, b_ref, o_ref):
    c = c_ref[...]
    s = c / (1.0 + jnp.exp(-c))
    o_ref[...] = jnp.dot(s.astype(BF16), w_ref[...].astype(BF16), preferred_element_type=F32) + b_ref[...]


def _ada_call(c, w_ada, b_ada):
    b = c.shape[0]
    n = w_ada.shape[1]
    tn = D_MODEL
    return pl.pallas_call(
        _ada_kernel,
        grid=(n // tn,),
        in_specs=[
            pl.BlockSpec((b, D_MODEL), lambda j: (0, 0)),
            pl.BlockSpec((D_MODEL, tn), lambda j: (0, j)),
            pl.BlockSpec((1, tn), lambda j: (0, j)),
        ],
        out_specs=pl.BlockSpec((b, tn), lambda j: (0, j)),
        out_shape=jax.ShapeDtypeStruct((b, n), F32),
        compiler_params=pltpu.CompilerParams(dimension_semantics=("parallel",)),
        name="ada_mod",
    )(c, w_ada, b_ada.reshape(1, n))


def _rope_group(xg, keep, swap):
    up = pltpu.roll(xg, LANES - MLA_ROPE // 2, 1)
    return xg * keep + up * swap


def _pre_kernel(x_ref, mod_ref, gmix_ref, win_ref, gqa_ref, wq_ref, gkva_ref, wk_ref, wv_ref,
                ropeq_ref, ropek_ref,
                qf_ref, kf_ref, v_ref, sq_ref, sk_ref, sv_ref, *, ts):
    for r0 in range(0, x_ref.shape[1], ts):
        _pre_rows(r0, ts, x_ref, mod_ref, gmix_ref, win_ref, gqa_ref, wq_ref, gkva_ref, wk_ref, wv_ref,
                  ropeq_ref, ropek_ref, qf_ref, kf_ref, v_ref, sq_ref, sk_ref, sv_ref)


def _pre_rows(r0, ts, x_ref, mod_ref, gmix_ref, win_ref, gqa_ref, wq_ref, gkva_ref, wk_ref, wv_ref,
              ropeq_ref, ropek_ref, qf_ref, kf_ref, v_ref, sq_ref, sk_ref, sv_ref):
    rows = slice(r0, r0 + ts)
    x = x_ref[0, rows]
    mod = mod_ref[0]
    sh1 = mod[0:1]
    sc1 = mod[1:2]
    h = (_rms(x) * gmix_ref[...] * (1.0 + sc1) + sh1).astype(BF16)
    proj = jnp.dot(h, win_ref[...], preferred_element_type=F32)

    for g in range(SWA_GROUP):
        sq_ref[0, g, rows] = (proj[:, C_SQ + g * LANES:C_SQ + (g + 1) * LANES] * SWA_Q_SCALE).astype(BF16)
    swa_k = proj[:, C_SK:C_SV]
    low = lax.broadcasted_iota(jnp.int32, swa_k.shape, 1) < HALF
    sk_ref[0, rows, :LANES] = jnp.where(low, swa_k, 0.0).astype(BF16)
    sk_ref[0, rows, LANES:] = jnp.where(low, 0.0, swa_k).astype(BF16)
    svt = proj[:, C_SV:IN_COLS_PAD].T.astype(BF16)
    for jb in range(ts // WINDOW):
        sv_ref[0, r0 // WINDOW + jb] = svt[:, jb * WINDOW:(jb + 1) * WINDOW]

    qn = (_rms(proj[:, C_QLAT:C_KVLAT]) * gqa_ref[...]).astype(BF16)
    kvn = (_rms(proj[:, C_KVLAT:C_KPE]) * gkva_ref[...]).astype(BF16)
    q = jnp.dot(qn, wq_ref[...], preferred_element_type=F32)
    kn = jnp.dot(kvn, wk_ref[...], preferred_element_type=F32)
    vt = lax.dot_general(wv_ref[...], kvn, (((1,), (1,)), ((), ())), preferred_element_type=F32).astype(BF16)
    tk = v_ref.shape[-1]
    for hp in range(MLA_HEADS // 2):
        for jb in range(ts // tk):
            v_ref[0, hp, r0 // tk + jb] = vt[hp * LANES:(hp + 1) * LANES, jb * tk:(jb + 1) * tk]

    q_keep, q_swap = ropeq_ref[0, rows], ropeq_ref[1, rows]
    kpe = _rope_group(proj[:, C_KPE:C_SQ], ropek_ref[0, rows], ropek_ref[1, rows])
    nope_lanes = lax.broadcasted_iota(jnp.int32, kpe.shape, 1) < MLA_NOPE
    for hd in range(MLA_HEADS):
        sl = slice(hd * LANES, (hd + 1) * LANES)
        qf_ref[0, rows, sl] = _rope_group(q[:, sl], q_keep, q_swap).astype(BF16)
        kn_pair = kn[:, (hd // 2) * LANES:(hd // 2 + 1) * LANES]
        if hd % 2:
            kn_pair = pltpu.roll(kn_pair, HALF, 1)
        kf_ref[0, rows, sl] = jnp.where(nope_lanes, kn_pair, kpe).astype(BF16)


def _rope_tables(scale_q):
    half = MLA_ROPE // 2
    freqs = ROPE_THETA ** (-np.arange(0, MLA_ROPE, 2, dtype=np.float64) / MLA_ROPE)
    ang = np.arange(SEQ, dtype=np.float64)[:, None] * freqs[None, :]
    cos, sin = np.cos(ang), np.sin(ang)
    keep = np.zeros((SEQ, LANES))
    swap = np.zeros((SEQ, LANES))
    keep[:, :MLA_NOPE] = 1.0
    keep[:, MLA_NOPE:MLA_NOPE + half] = cos
    keep[:, MLA_NOPE + half:MLA_NOPE + MLA_ROPE] = cos
    swap[:, MLA_NOPE:MLA_NOPE + half] = -sin
    swap[:, MLA_NOPE + half:MLA_NOPE + MLA_ROPE] = sin
    return (np.stack([keep, swap]) * scale_q).astype(np.float32)


def _pre_call(x, mod, gmix, win, gqa, wq, gkva, wk, wv, tm, ts, tk):
    b, s, d = x.shape
    pairs = MLA_HEADS // 2
    ropeq = jnp.asarray(_rope_tables(MLA_Q_SCALE))
    ropek = jnp.asarray(_rope_tables(1.0))
    row = lambda bi, i: (bi, i, 0)
    out_shapes = (
        jax.ShapeDtypeStruct((b, s, MLA_QK_COLS), BF16),
        jax.ShapeDtypeStruct((b, s, MLA_QK_COLS), BF16),
        jax.ShapeDtypeStruct((b, pairs, s // tk, LANES, tk), BF16),
        jax.ShapeDtypeStruct((b, SWA_GROUP, s, LANES), BF16),
        jax.ShapeDtypeStruct((b, s, SWA_KV_HEADS * LANES), BF16),
        jax.ShapeDtypeStruct((b, s // WINDOW, SWA_KV_COLS, WINDOW), BF16),
    )
    out_specs = [
        pl.BlockSpec((1, tm, MLA_QK_COLS), row),
        pl.BlockSpec((1, tm, MLA_QK_COLS), row),
        pl.BlockSpec((1, pairs, tm // tk, LANES, tk), lambda bi, i: (bi, 0, i, 0, 0)),
        pl.BlockSpec((1, SWA_GROUP, tm, LANES), lambda bi, i: (bi, 0, i, 0)),
        pl.BlockSpec((1, tm, SWA_KV_HEADS * LANES), row),
        pl.BlockSpec((1, tm // WINDOW, SWA_KV_COLS, WINDOW), lambda bi, i: (bi, i, 0, 0)),
    ]
    return pl.pallas_call(
        functools.partial(_pre_kernel, ts=ts),
        grid=(b, s // tm),
        in_specs=[
            pl.BlockSpec((1, tm, d), row),
            pl.BlockSpec((1, N_MOD, d), lambda bi, i: (bi, 0, 0)),
            _const_spec((1, d)),
            _const_spec(win.shape),
            _const_spec((1, Q_LORA)),
            _const_spec(wq.shape),
            _const_spec((1, KV_LORA)),
            _const_spec(wk.shape),
            _const_spec(wv.shape),
            pl.BlockSpec((2, tm, LANES), lambda bi, i: (0, i, 0)),
            pl.BlockSpec((2, tm, LANES), lambda bi, i: (0, i, 0)),
        ],
        out_specs=tuple(out_specs),
        out_shape=out_shapes,
        compiler_params=pltpu.CompilerParams(
            dimension_semantics=("parallel", "parallel"), vmem_limit_bytes=VMEM_LIMIT_BYTES),
        name="pre_attn",
    )(x, mod, gmix, win, gqa, wq, gkva, wk, wv, ropeq, ropek)


def _mla_kernel(q_ref, k_ref, vt_ref, o_ref, *, tq, qsets):
    if len(qsets) == 1:
        _mla_blocks(q_ref, k_ref, vt_ref, o_ref, qblocks=qsets[0], tq=tq)
        return
    for v, qblocks in enumerate(qsets):
        pl.when(pl.program_id(2) == v)(
            functools.partial(_mla_blocks, q_ref, k_ref, vt_ref, o_ref, qblocks=qblocks, tq=tq))


def _mla_blocks(q_ref, k_ref, vt_ref, o_ref, *, qblocks, tq):
    ki = lax.broadcasted_iota(jnp.int32, (tq, tq), 0)
    qi = lax.broadcasted_iota(jnp.int32, (tq, tq), 1)
    causal = ki <= qi
    heads = range(q_ref.shape[2] // LANES)
    steps = [(i, j) for i in qblocks for j in range(i + 1)]

    def scores(i, j):
        out = []
        for h in heads:
            cols = slice(h * LANES, (h + 1) * LANES)
            q = q_ref[0, i * tq:(i + 1) * tq, cols]
            k = k_ref[0, j * tq:(j + 1) * tq, cols]
            out.append(lax.dot_general(k, q, (((1,), (1,)), ((), ())), preferred_element_type=F32))
        return out

    ones = jnp.ones((BF16_SUBLANES, tq), BF16)
    st_next = scores(*steps[0])
    for n, (i, j) in enumerate(steps):
        st_cur = st_next
        if n + 1 < len(steps):
            st_next = scores(*steps[n + 1])
        if j == 0:
            state = [(jnp.full((1, tq), -jnp.inf, F32), jnp.zeros((MLA_V + BF16_SUBLANES, tq), F32))
                     for _ in heads]
        for h in heads:
            m, acc = state[h]
            st = st_cur[h]
            if j == i:
                st = jnp.where(causal, st, -jnp.inf)
            vt = vt_ref[0, h // 2, j, (h % 2) * MLA_V:(h % 2 + 1) * MLA_V, :]
            vt = jnp.concatenate([vt, ones], axis=0)
            m_new = jnp.maximum(m, jnp.max(st, axis=0, keepdims=True))
            alpha = jnp.exp2(m - m_new)
            pt = jnp.exp2(st - m_new).astype(BF16)
            acc_new = alpha * acc + jnp.dot(vt, pt, preferred_element_type=F32)
            state[h] = (m_new, acc_new)
        if j == i:
            for p in range(len(heads) // 2):
                pair = [state[2 * p][1], state[2 * p + 1][1]]
                out_t = jnp.concatenate([acc[:MLA_V] / acc[MLA_V:MLA_V + 1] for acc in pair], axis=0)
                o_ref[0, i * tq:(i + 1) * tq, p * LANES:(p + 1) * LANES] = out_t.T.astype(BF16)


def _mla_call(qf, kf, vt, tq, hps, qsets):
    b, s, _ = qf.shape
    pairs = MLA_HEADS // 2
    assert vt.shape == (b, pairs, s // tq, LANES, tq) and pairs % hps == 0
    assert sorted(i for qs in qsets for i in qs) == list(range(s // tq))
    seq_cols = lambda bi, g, v: (bi, 0, g)
    return pl.pallas_call(
        functools.partial(_mla_kernel, tq=tq, qsets=qsets),
        grid=(b, pairs // hps, len(qsets)),
        in_specs=[
            pl.BlockSpec((1, s, 2 * hps * LANES), seq_cols),
            pl.BlockSpec((1, s, 2 * hps * LANES), seq_cols),
            pl.BlockSpec((1, hps, s // tq, LANES, tq), lambda bi, g, v: (bi, g, 0, 0, 0)),
        ],
        out_specs=pl.BlockSpec((1, s, hps * LANES), seq_cols),
        out_shape=jax.ShapeDtypeStruct((b, s, MLA_V_COLS), BF16),
        compiler_params=pltpu.CompilerParams(
            dimension_semantics=("parallel", "parallel", "arbitrary"), vmem_limit_bytes=VMEM_LIMIT_BYTES),
        name="mla_attn",
    )(qf, kf, vt)


def _alibi_slope(h):
    return 2.0 ** (-8.0 * (h + 1) / SWA_HEADS)


SWA_STACK = 2
SWA_LOOKAHEAD = 4


def _swa_bias():
    k = np.arange(2 * WINDOW)[:, None]
    q = np.arange(WINDOW)[None, :]
    dist = q + WINDOW - k
    valid = (dist >= 0) & (dist < WINDOW)
    out = np.empty((SWA_KV_HEADS, 2 * WINDOW, SWA_GROUP * WINDOW), np.float32)
    for kv in range(SWA_KV_HEADS):
        for g in range(SWA_GROUP):
            slope = np.float32(_alibi_slope(kv * SWA_GROUP + g))
            out[kv, :, g * WINDOW:(g + 1) * WINDOW] = np.where(valid, -slope * dist.astype(np.float32) * LOG2E, -np.inf)
    return out


def _swa_kernel(sink_ref, bias_ref, q_ref, kp_ref, k_ref, vtp_ref, vt_ref, o_ref):
    i = pl.program_id(1)
    nsub = q_ref.shape[2] // WINDOW
    ones = jnp.ones((BF16_SUBLANES, 2 * WINDOW), BF16)
    width = SWA_STACK * WINDOW
    gone = jnp.where(i == 0, -jnp.inf, 0.0)
    chains = [(c, g0, kv) for c in range(nsub) for g0 in range(0, SWA_GROUP, SWA_STACK)
              for kv in range(SWA_KV_HEADS)]

    def scores(c, g0, kv):
        cols = slice(kv * LANES, (kv + 1) * LANES)
        if c == 0:
            k_win = jnp.concatenate([kp_ref[0, :, cols], k_ref[0, :WINDOW, cols]], axis=0)
        else:
            k_win = k_ref[0, (c - 1) * WINDOW:(c + 1) * WINDOW, cols]
        q_st = q_ref[0, g0:g0 + SWA_STACK, c * WINDOW:(c + 1) * WINDOW, :].reshape(width, LANES)
        st = lax.dot_general(k_win, q_st, (((1,), (1,)), ((), ())), preferred_element_type=F32)
        return st + bias_ref[kv, :, g0 * WINDOW:g0 * WINDOW + width]

    outs = {}
    done = []
    pending = [scores(*ch) for ch in chains[:SWA_LOOKAHEAD]]
    for n, (c, g0, kv) in enumerate(chains):
        st = pending.pop(0)
        if n + SWA_LOOKAHEAD < len(chains):
            pending.append(scores(*chains[n + SWA_LOOKAHEAD]))
        rows = slice(kv * SWA_HEAD_DIM, (kv + 1) * SWA_HEAD_DIM)
        if c == 0:
            vt = jnp.concatenate([vtp_ref[0, 0, rows, :], vt_ref[0, 0, rows, :]], axis=1)
        else:
            vt = jnp.concatenate([vt_ref[0, c - 1, rows, :], vt_ref[0, c, rows, :]], axis=1)
        if c == 0:
            st = jnp.concatenate([st[:WINDOW] + gone, st[WINDOW:]], axis=0)
        sink = sink_ref[kv:kv + 1, g0 * WINDOW:g0 * WINDOW + width]
        m = jnp.maximum(jnp.max(st, axis=0, keepdims=True), sink)
        pt = jnp.exp2(st - m).astype(BF16)
        res = jnp.dot(jnp.concatenate([vt, ones], axis=0), pt, preferred_element_type=F32)
        denom = res[SWA_HEAD_DIM:SWA_HEAD_DIM + 1] + jnp.exp2(sink - m)
        outs[kv] = res[:SWA_HEAD_DIM] / denom
        if kv == SWA_KV_HEADS - 1:
            for gg in range(SWA_STACK):
                gs = slice(gg * WINDOW, (gg + 1) * WINDOW)
                out_t = jnp.concatenate([outs[0][:, gs], outs[1][:, gs]], axis=0)
                done.append((c, g0 + gg, out_t))
    for c, g, out_t in done:
        o_ref[0, c * WINDOW:(c + 1) * WINDOW, g * LANES:(g + 1) * LANES] = out_t.T.astype(BF16)


def _swa_call(sinks, sq, sk, svt, tstep):
    b, _, s, _ = sq.shape
    nsub = tstep // WINDOW
    bias = jnp.asarray(_swa_bias())
    sink_rows = jnp.repeat(sinks.reshape(SWA_KV_HEADS, SWA_GROUP), WINDOW, axis=1) * LOG2E
    prev_blk = lambda i: jnp.maximum(i * nsub - 1, 0)
    return pl.pallas_call(
        _swa_kernel,
        grid=(b, s // tstep),
        in_specs=[
            _const_spec(sink_rows.shape),
            _const_spec(bias.shape),
            pl.BlockSpec((1, SWA_GROUP, tstep, LANES), lambda bi, i: (bi, 0, i, 0)),
            pl.BlockSpec((1, WINDOW, SWA_KV_HEADS * LANES), lambda bi, i: (bi, prev_blk(i), 0)),
            pl.BlockSpec((1, tstep, SWA_KV_HEADS * LANES), lambda bi, i: (bi, i, 0)),
            pl.BlockSpec((1, 1, SWA_KV_COLS, WINDOW), lambda bi, i: (bi, prev_blk(i), 0, 0)),
            pl.BlockSpec((1, nsub, SWA_KV_COLS, WINDOW), lambda bi, i: (bi, i, 0, 0)),
        ],
        out_specs=pl.BlockSpec((1, tstep, SWA_Q_COLS), lambda bi, i: (bi, i, 0)),
        out_shape=jax.ShapeDtypeStruct((b, s, SWA_Q_COLS), BF16),
        compiler_params=pltpu.CompilerParams(
            dimension_semantics=("parallel", "arbitrary"), vmem_limit_bytes=VMEM_LIMIT_BYTES),
        name="swa_attn",
    )(sink_rows, bias, sq, sk, sk, svt, svt)


def _post_kernel(x_ref, mla_ref, swa_ref, mod_ref, woa_ref, wob_ref, gmlp_ref, wup_ref, wdown_ref, gfin_ref,
                 o_ref, *, ff_chunk):
    x = x_ref[0]
    mod = mod_ref[0]
    g1, sh2, sc2, g2 = mod[2:3], mod[3:4], mod[4:5], mod[5:6]
    attn = (jnp.dot(mla_ref[0], woa_ref[...], preferred_element_type=F32)
            + jnp.dot(swa_ref[0], wob_ref[...], preferred_element_type=F32))
    x1 = x + g1 * attn
    h2 = (_rms(x1) * gmlp_ref[...] * (1.0 + sc2) + sh2).astype(BF16)
    acc = jnp.zeros_like(x1)
    for c in range(D_FF // ff_chunk):
        cs = slice(c * ff_chunk, (c + 1) * ff_chunk)
        u = jnp.maximum(jnp.dot(h2, wup_ref[:, cs], preferred_element_type=F32), 0.0)
        acc = acc + jnp.dot((u * u).astype(BF16), wdown_ref[cs, :], preferred_element_type=F32)
    x2 = x1 + g2 * acc
    o_ref[0] = _rms(x2) * gfin_ref[...]


def _post_call(x, mla, swa, mod, woa, wob, gmlp, wup, wdown, gfin, tm, ff_chunk):
    b, s, d = x.shape
    row = lambda bi, i: (bi, i, 0)
    return pl.pallas_call(
        functools.partial(_post_kernel, ff_chunk=ff_chunk),
        grid=(b, s // tm),
        in_specs=[
            pl.BlockSpec((1, tm, d), row),
            pl.BlockSpec((1, tm, MLA_V_COLS), row),
            pl.BlockSpec((1, tm, SWA_Q_COLS), row),
            pl.BlockSpec((1, N_MOD, d), lambda bi, i: (bi, 0, 0)),
            _const_spec(woa.shape),
            _const_spec(wob.shape),
            _const_spec((1, d)),
            _const_spec(wup.shape),
            _const_spec(wdown.shape),
            _const_spec((1, d)),
        ],
        out_specs=pl.BlockSpec((1, tm, d), row),
        out_shape=jax.ShapeDtypeStruct((b, s, d), F32),
        compiler_params=pltpu.CompilerParams(
            dimension_semantics=("parallel", "parallel"), vmem_limit_bytes=VMEM_LIMIT_BYTES),
        name="post_attn",
    )(x, mla, swa, mod, woa, wob, gmlp, wup, wdown, gfin)


def _prep_layer_weights(w_in, w_qb, w_kvb, w_o):
    o1 = Q_LORA
    o2 = o1 + KV_LORA
    o3 = o2 + MLA_ROPE
    o4 = o3 + SWA_Q_COLS
    d = w_in.shape[0]
    half = MLA_ROPE // 2
    pad = LANES - MLA_NOPE - MLA_ROPE - half
    kpe = jnp.concatenate([jnp.zeros((d, MLA_NOPE), w_in.dtype), w_in[:, o2:o3], w_in[:, o2:o2 + half],
                           jnp.zeros((d, pad), w_in.dtype)], axis=1)
    sq = w_in[:, o3:o4].reshape(d, SWA_KV_HEADS, SWA_GROUP, SWA_HEAD_DIM)
    sq = sq.transpose(0, 2, 1, 3).reshape(d, SWA_Q_COLS)
    win = jnp.concatenate([w_in[:, :o2], kpe, sq, w_in[:, o4:]], axis=1).astype(BF16)

    wq = jnp.concatenate([w_qb, w_qb[:, :, MLA_NOPE:MLA_NOPE + half],
                          jnp.zeros(w_qb.shape[:2] + (pad,), w_qb.dtype)], axis=2)
    wq = wq.reshape(Q_LORA, MLA_QK_COLS).astype(BF16)
    wk = w_kvb[:, :, :MLA_NOPE].reshape(KV_LORA, MLA_HEADS * MLA_NOPE).astype(BF16)
    wv = w_kvb[:, :, MLA_NOPE:].reshape(KV_LORA, MLA_V_COLS).T.astype(BF16)

    woa = w_o[:MLA_V_COLS].astype(BF16)
    wob = w_o[MLA_V_COLS:].reshape(SWA_KV_HEADS, SWA_GROUP, SWA_HEAD_DIM, -1)
    wob = wob.transpose(1, 0, 2, 3).reshape(SWA_Q_COLS, -1).astype(BF16)
    return win, wq, wk, wv, woa, wob


def kernel(x, c, w_ada, b_ada, norm_mix_g, w_in, g_qa, w_qb, g_kva, w_kvb, sinks,
           w_o, norm_mlp_g, w_up, w_down, final_g):
    depth = w_ada.shape[0]
    b = x.shape[0]
    assert depth == 1, "the final rmsnorm is fused into the single layer's post-attention call"
    for l in range(depth):
        mod = _ada_call(c, w_ada[l], b_ada[l]).reshape(b, N_MOD, D_MODEL)
        win, wq, wk, wv, woa, wob = _prep_layer_weights(w_in[l], w_qb[l], w_kvb[l], w_o[l])
        qf, kf, vt, sq, sk, svt = _pre_call(
            x, mod, norm_mix_g[l][None], win, g_qa[l][None], wq, g_kva[l][None], wk, wv,
            tm=1024, ts=512, tk=256)
        mla = _mla_call(qf, kf, vt, tq=256, hps=2, qsets=(tuple(range(SEQ // 256)),))
        swa = _swa_call(sinks[l], sq, sk, svt, tstep=1024)
        x = _post_call(x, mla, swa, mod, woa, wob, norm_mlp_g[l][None],
                       w_up[l].astype(BF16), w_down[l].astype(BF16), final_g[None], tm=512, ff_chunk=1024)
    return x
```

```python
import functools

import numpy as np
import jax
import jax.numpy as jnp
from jax import lax
from jax.experimental import pallas as pl
from jax.experimental.pallas import tpu as pltpu

D_MODEL = 1024
SEQ = 2048
MLA_HEADS = 8
MLA_NOPE = 64
MLA_ROPE = 32
MLA_V = 64
Q_LORA = 384
KV_LORA = 256
ROPE_THETA = 10000.0
SWA_HEADS = 8
SWA_KV_HEADS = 2
SWA_GROUP = SWA_HEADS // SWA_KV_HEADS
SWA_HEAD_DIM = 64
WINDOW = 128
D_FF = 4 * D_MODEL
EPS = 1e-6
N_MOD = 6

LANES = 128
HALF = LANES // 2
BF16_SUBLANES = 16
MLA_QK_COLS = MLA_HEADS * LANES
MLA_V_COLS = MLA_HEADS * MLA_V
SWA_Q_COLS = SWA_HEADS * SWA_HEAD_DIM
SWA_KV_COLS = SWA_KV_HEADS * SWA_HEAD_DIM
C_QLAT = 0
C_KVLAT = C_QLAT + Q_LORA
C_KPE = C_KVLAT + KV_LORA
C_SQ = C_KPE + LANES
C_SK = C_SQ + SWA_Q_COLS
C_SV = C_SK + SWA_KV_COLS
IN_COLS_PAD = C_SV + SWA_KV_COLS

VMEM_LIMIT_BYTES = 56 * 1024 * 1024

LOG2E = float(np.log2(np.e))
MLA_Q_SCALE = (MLA_NOPE + MLA_ROPE) ** -0.5 * LOG2E
SWA_Q_SCALE = SWA_HEAD_DIM ** -0.5 * LOG2E

BF16 = jnp.bfloat16
F32 = jnp.float32


def _const_spec(shape):
    nd = len(shape)
    return pl.BlockSpec(shape, lambda *_: (0,) * nd, pipeline_mode=pl.Buffered(1))


def _rms(x):
    return x * lax.rsqrt(jnp.mean(x * x, axis=-1, keepdims=True) + EPS)


def _ada_kernel(c_ref, w_ref, b_ref, o_ref):
    c = c_ref[...]
    s = c / (1.0 + jnp.exp(-c))
    o_ref[...] = jnp.dot(s.astype(BF16), w_ref[...].astype(BF16), preferred_element_type=F32) + b_ref[...]


def _ada_call(c, w_ada, b_ada):
    b = c.shape[0]
    n = w_ada.shape[1]
    tn = D_MODEL
    return pl.pallas_call(
        _ada_kernel,
        grid=(n // tn,),
        in_specs=[
            pl.BlockSpec((b, D_MODEL), lambda j: (0, 0)),
            pl.BlockSpec((D_MODEL, tn), lambda j: (0, j)),
            pl.BlockSpec((1, tn), lambda j: (0, j)),
        ],
        out_specs=pl.BlockSpec((b, tn), lambda j: (0, j)),
        out_shape=jax.ShapeDtypeStruct((b, n), F32),
        compiler_params=pltpu.CompilerParams(dimension_semantics=("parallel",)),
        name="ada_mod",
    )(c, w_ada, b_ada.reshape(1, n))


def _rope_group(xg, keep, swap):
    up = pltpu.roll(xg, LANES - MLA_ROPE // 2, 1)
    return xg * keep + up * swap


def _pre_kernel(x_ref, mod_ref, gmix_ref, win_ref, gqa_ref, wq_ref, gkva_ref, wk_ref, wv_ref,
                ropeq_ref, ropek_ref,
                qf_ref, kf_ref, v_ref, sq_ref, sk_ref, sv_ref, *, ts):
    for r0 in range(0, x_ref.shape[1], ts):
        _pre_rows(r0, ts, x_ref, mod_ref, gmix_ref, win_ref, gqa_ref, wq_ref, gkva_ref, wk_ref, wv_ref,
                  ropeq_ref, ropek_ref, qf_ref, kf_ref, v_ref, sq_ref, sk_ref, sv_ref)


def _pre_rows(r0, ts, x_ref, mod_ref, gmix_ref, win_ref, gqa_ref, wq_ref, gkva_ref, wk_ref, wv_ref,
              ropeq_ref, ropek_ref, qf_ref, kf_ref, v_ref, sq_ref, sk_ref, sv_ref):
    rows = slice(r0, r0 + ts)
    x = x_ref[0, rows]
    mod = mod_ref[0]
    sh1 = mod[0:1]
    sc1 = mod[1:2]
    h = (_rms(x) * gmix_ref[...] * (1.0 + sc1) + sh1).astype(BF16)
    proj = jnp.dot(h, win_ref[...], preferred_element_type=F32)

    for g in range(SWA_GROUP):
        sq_ref[0, g, rows] = (proj[:, C_SQ + g * LANES:C_SQ + (g + 1) * LANES] * SWA_Q_SCALE).astype(BF16)
    swa_k = proj[:, C_SK:C_SV]
    low = lax.broadcasted_iota(jnp.int32, swa_k.shape, 1) < HALF
    sk_ref[0, rows, :LANES] = jnp.where(low, swa_k, 0.0).astype(BF16)
    sk_ref[0, rows, LANES:] = jnp.where(low, 0.0, swa_k).astype(BF16)
    svt = proj[:, C_SV:IN_COLS_PAD].T.astype(BF16)
    for jb in range(ts // WINDOW):
        sv_ref[0, r0 // WINDOW + jb] = svt[:, jb * WINDOW:(jb + 1) * WINDOW]

    qn = (_rms(proj[:, C_QLAT:C_KVLAT]) * gqa_ref[...]).astype(BF16)
    kvn = (_rms(proj[:, C_KVLAT:C_KPE]) * gkva_ref[...]).astype(BF16)
    q = jnp.dot(qn, wq_ref[...], preferred_element_type=F32)
    kn = jnp.dot(kvn, wk_ref[...], preferred_element_type=F32)
    vt = lax.dot_general(wv_ref[...], kvn, (((1,), (1,)), ((), ())), preferred_element_type=F32).astype(BF16)
    tk = v_ref.shape[-1]
    for hp in range(MLA_HEADS // 2):
        for jb in range(ts // tk):
            v_ref[0, hp, r0 // tk + jb] = vt[hp * LANES:(hp + 1) * LANES, jb * tk:(jb + 1) * tk]

    q_keep, q_swap = ropeq_ref[0, rows], ropeq_ref[1, rows]
    kpe = _rope_group(proj[:, C_KPE:C_SQ], ropek_ref[0, rows], ropek_ref[1, rows])
    nope_lanes = lax.broadcasted_iota(jnp.int32, kpe.shape, 1) < MLA_NOPE
    for hd in range(MLA_HEADS):
        sl = slice(hd * LANES, (hd + 1) * LANES)
        qf_ref[0, rows, sl] = _rope_group(q[:, sl], q_keep, q_swap).astype(BF16)
        kn_pair = kn[:, (hd // 2) * LANES:(hd // 2 + 1) * LANES]
        if hd % 2:
            kn_pair = pltpu.roll(kn_pair, HALF, 1)
        kf_ref[0, rows, sl] = jnp.where(nope_lanes, kn_pair, kpe).astype(BF16)


def _rope_tables(scale_q):
    half = MLA_ROPE // 2
    freqs = ROPE_THETA ** (-np.arange(0, MLA_ROPE, 2, dtype=np.float64) / MLA_ROPE)
    ang = np.arange(SEQ, dtype=np.float64)[:, None] * freqs[None, :]
    cos, sin = np.cos(ang), np.sin(ang)
    keep = np.zeros((SEQ, LANES))
    swap = np.zeros((SEQ, LANES))
    keep[:, :MLA_NOPE] = 1.0
    keep[:, MLA_NOPE:MLA_NOPE + half] = cos
    keep[:, MLA_NOPE + half:MLA_NOPE + MLA_ROPE] = cos
    swap[:, MLA_NOPE:MLA_NOPE + half] = -sin
    swap[:, MLA_NOPE + half:MLA_NOPE + MLA_ROPE] = sin
    return (np.stack([keep, swap]) * scale_q).astype(np.float32)


def _pre_call(x, mod, gmix, win, gqa, wq, gkva, wk, wv, tm, ts, tk):
    b, s, d = x.shape
    pairs = MLA_HEADS // 2
    ropeq = jnp.asarray(_rope_tables(MLA_Q_SCALE))
    ropek = jnp.asarray(_rope_tables(1.0))
    row = lambda bi, i: (bi, i, 0)
    out_shapes = (
        jax.ShapeDtypeStruct((b, s, MLA_QK_COLS), BF16),
        jax.ShapeDtypeStruct((b, s, MLA_QK_COLS), BF16),
        jax.ShapeDtypeStruct((b, pairs, s // tk, LANES, tk), BF16),
        jax.ShapeDtypeStruct((b, SWA_GROUP, s, LANES), BF16),
        jax.ShapeDtypeStruct((b, s, SWA_KV_HEADS * LANES), BF16),
        jax.ShapeDtypeStruct((b, s // WINDOW, SWA_KV_COLS, WINDOW), BF16),
    )
    out_specs = [
        pl.BlockSpec((1, tm, MLA_QK_COLS), row),
        pl.BlockSpec((1, tm, MLA_QK_COLS), row),
        pl.BlockSpec((1, pairs, tm // tk, LANES, tk), lambda bi, i: (bi, 0, i, 0, 0)),
        pl.BlockSpec((1, SWA_GROUP, tm, LANES), lambda bi, i: (bi, 0, i, 0)),
        pl.BlockSpec((1, tm, SWA_KV_HEADS * LANES), row),
        pl.BlockSpec((1, tm // WINDOW, SWA_KV_COLS, WINDOW), lambda bi, i: (bi, i, 0, 0)),
    ]
    return pl.pallas_call(
        functools.partial(_pre_kernel, ts=ts),
        grid=(b, s // tm),
        in_specs=[
            pl.BlockSpec((1, tm, d), row),
            pl.BlockSpec((1, N_MOD, d), lambda bi, i: (bi, 0, 0)),
            _const_spec((1, d)),
            _const_spec(win.shape),
            _const_spec((1, Q_LORA)),
            _const_spec(wq.shape),
            _const_spec((1, KV_LORA)),
            _const_spec(wk.shape),
            _const_spec(wv.shape),
            pl.BlockSpec((2, tm, LANES), lambda bi, i: (0, i, 0)),
            pl.BlockSpec((2, tm, LANES), lambda bi, i: (0, i, 0)),
        ],
        out_specs=tuple(out_specs),
        out_shape=out_shapes,
        compiler_params=pltpu.CompilerParams(
            dimension_semantics=("parallel", "parallel"), vmem_limit_bytes=VMEM_LIMIT_BYTES),
        name="pre_attn",
    )(x, mod, gmix, win, gqa, wq, gkva, wk, wv, ropeq, ropek)


def _mla_kernel(q_ref, k_ref, vt_ref, o_ref, *, tq, qsets):
    if len(qsets) == 1:
        _mla_blocks(q_ref, k_ref, vt_ref, o_ref, qblocks=qsets[0], tq=tq)
        return
    for v, qblocks in enumerate(qsets):
        pl.when(pl.program_id(2) == v)(
            functools.partial(_mla_blocks, q_ref, k_ref, vt_ref, o_ref, qblocks=qblocks, tq=tq))


def _mla_blocks(q_ref, k_ref, vt_ref, o_ref, *, qblocks, tq):
    ki = lax.broadcasted_iota(jnp.int32, (tq, tq), 0)
    qi = lax.broadcasted_iota(jnp.int32, (tq, tq), 1)
    causal = ki <= qi
    heads = range(q_ref.shape[2] // LANES)
    steps = [(i, j) for i in qblocks for j in range(i + 1)]

    def scores(i, j):
        out = []
        for h in heads:
            cols = slice(h * LANES, (h + 1) * LANES)
            q = q_ref[0, i * tq:(i + 1) * tq, cols]
            k = k_ref[0, j * tq:(j + 1) * tq, cols]
            out.append(lax.dot_general(k, q, (((1,), (1,)), ((), ())), preferred_element_type=F32))
        return out

    ones = jnp.ones((BF16_SUBLANES, tq), BF16)
    st_next = scores(*steps[0])
    for n, (i, j) in enumerate(steps):
        st_cur = st_next
        if n + 1 < len(steps):
            st_next = scores(*steps[n + 1])
        if j == 0:
            state = [(jnp.full((1, tq), -jnp.inf, F32), jnp.zeros((MLA_V + BF16_SUBLANES, tq), F32))
                     for _ in heads]
        for h in heads:
            m, acc = state[h]
            st = st_cur[h]
            if j == i:
                st = jnp.where(causal, st, -jnp.inf)
            vt = vt_ref[0, h // 2, j, (h % 2) * MLA_V:(h % 2 + 1) * MLA_V, :]
            vt = jnp.concatenate([vt, ones], axis=0)
            m_new = jnp.maximum(m, jnp.max(st, axis=0, keepdims=True))
            alpha = jnp.exp2(m - m_new)
            pt = jnp.exp2(st - m_new).astype(BF16)
            acc_new = alpha * acc + jnp.dot(vt, pt, preferred_element_type=F32)
            state[h] = (m_new, acc_new)
        if j == i:
            for p in range(len(heads) // 2):
                pair = [state[2 * p][1], state[2 * p + 1][1]]
                out_t = jnp.concatenate([acc[:MLA_V] / acc[MLA_V:MLA_V + 1] for acc in pair], axis=0)
                o_ref[0, i * tq:(i + 1) * tq, p * LANES:(p + 1) * LANES] = out_t.T.astype(BF16)


def _mla_call(qf, kf, vt, tq, hps, qsets):
    b, s, _ = qf.shape
    pairs = MLA_HEADS // 2
    assert vt.shape == (b, pairs, s // tq, LANES, tq) and pairs % hps == 0
    assert sorted(i for qs in qsets for i in qs) == list(range(s // tq))
    seq_cols = lambda bi, g, v: (bi, 0, g)
    return pl.pallas_call(
        functools.partial(_mla_kernel, tq=tq, qsets=qsets),
        grid=(b, pairs // hps, len(qsets)),
        in_specs=[
            pl.BlockSpec((1, s, 2 * hps * LANES), seq_cols),
            pl.BlockSpec((1, s, 2 * hps * LANES), seq_cols),
            pl.BlockSpec((1, hps, s // tq, LANES, tq), lambda bi, g, v: (bi, g, 0, 0, 0)),
        ],
        out_specs=pl.BlockSpec((1, s, hps * LANES), seq_cols),
        out_shape=jax.ShapeDtypeStruct((b, s, MLA_V_COLS), BF16),
        compiler_params=pltpu.CompilerParams(
            dimension_semantics=("parallel", "parallel", "arbitrary"), vmem_limit_bytes=VMEM_LIMIT_BYTES),
        name="mla_attn",
    )(qf, kf, vt)


def _alibi_slope(h):
    return 2.0 ** (-8.0 * (h + 1) / SWA_HEADS)


SWA_STACK = 2
SWA_LOOKAHEAD = 4


def _swa_bias():
    k = np.arange(2 * WINDOW)[:, None]
    q = np.arange(WINDOW)[None, :]
    dist = q + WINDOW - k
    valid = (dist >= 0) & (dist < WINDOW)
    out = np.empty((SWA_KV_HEADS, 2 * WINDOW, SWA_GROUP * WINDOW), np.float32)
    for kv in range(SWA_KV_HEADS):
        for g in range(SWA_GROUP):
            slope = np.float32(_alibi_slope(kv * SWA_GROUP + g))
            out[kv, :, g * WINDOW:(g + 1) * WINDOW] = np.where(valid, -slope * dist.astype(np.float32) * LOG2E, -np.inf)
    return out


def _swa_kernel(sink_ref, bias_ref, q_ref, kp_ref, k_ref, vtp_ref, vt_ref, o_ref):
    i = pl.program_id(1)
    nsub = q_ref.shape[2] // WINDOW
    ones = jnp.ones((BF16_SUBLANES, 2 * WINDOW), BF16)
    width = SWA_STACK * WINDOW
    gone = jnp.where(i == 0, -jnp.inf, 0.0)
    chains = [(c, g0, kv) for c in range(nsub) for g0 in range(0, SWA_GROUP, SWA_STACK)
              for kv in range(SWA_KV_HEADS)]

    def scores(c, g0, kv):
        cols = slice(kv * LANES, (kv + 1) * LANES)
        if c == 0:
            k_win = jnp.concatenate([kp_ref[0, :, cols], k_ref[0, :WINDOW, cols]], axis=0)
        else:
            k_win = k_ref[0, (c - 1) * WINDOW:(c + 1) * WINDOW, cols]
        q_st = q_ref[0, g0:g0 + SWA_STACK, c * WINDOW:(c + 1) * WINDOW, :].reshape(width, LANES)
        st = lax.dot_general(k_win, q_st, (((1,), (1,)), ((), ())), preferred_element_type=F32)
        return st + bias_ref[kv, :, g0 * WINDOW:g0 * WINDOW + width]

    outs = {}
    done = []
    pending = [scores(*ch) for ch in chains[:SWA_LOOKAHEAD]]
    for n, (c, g0, kv) in enumerate(chains):
        st = pending.pop(0)
        if n + SWA_LOOKAHEAD < len(chains):
            pending.append(scores(*chains[n + SWA_LOOKAHEAD]))
        rows = slice(kv * SWA_HEAD_DIM, (kv + 1) * SWA_HEAD_DIM)
        if c == 0:
            vt = jnp.concatenate([vtp_ref[0, 0, rows, :], vt_ref[0, 0, rows, :]], axis=1)
        else:
            vt = jnp.concatenate([vt_ref[0, c - 1, rows, :], vt_ref[0, c, rows, :]], axis=1)
        if c == 0:
            st = jnp.concatenate([st[:WINDOW] + gone, st[WINDOW:]], axis=0)
        sink = sink_ref[kv:kv + 1, g0 * WINDOW:g0 * WINDOW + width]
        m = jnp.maximum(jnp.max(st, axis=0, keepdims=True), sink)
        pt = jnp.exp2(st - m).astype(BF16)
        res = jnp.dot(jnp.concatenate([vt, ones], axis=0), pt, preferred_element_type=F32)
        denom = res[SWA_HEAD_DIM:SWA_HEAD_DIM + 1] + jnp.exp2(sink - m)
        outs[kv] = res[:SWA_HEAD_DIM] / denom
        if kv == SWA_KV_HEADS - 1:
            for gg in range(SWA_STACK):
                gs = slice(gg * WINDOW, (gg + 1) * WINDOW)
                out_t = jnp.concatenate([outs[0][:, gs], outs[1][:, gs]], axis=0)
                done.append((c, g0 + gg, out_t))
    for c, g, out_t in done:
        o_ref[0, c * WINDOW:(c + 1) * WINDOW, g * LANES:(g + 1) * LANES] = out_t.T.astype(BF16)


def _swa_call(sinks, sq, sk, svt, tstep):
    b, _, s, _ = sq.shape
    nsub = tstep // WINDOW
    bias = jnp.asarray(_swa_bias())
    sink_rows = jnp.repeat(sinks.reshape(SWA_KV_HEADS, SWA_GROUP), WINDOW, axis=1) * LOG2E
    prev_blk = lambda i: jnp.maximum(i * nsub - 1, 0)
    return pl.pallas_call(
        _swa_kernel,
        grid=(b, s // tstep),
        in_specs=[
            _const_spec(sink_rows.shape),
            _const_spec(bias.shape),
            pl.BlockSpec((1, SWA_GROUP, tstep, LANES), lambda bi, i: (bi, 0, i, 0)),
            pl.BlockSpec((1, WINDOW, SWA_KV_HEADS * LANES), lambda bi, i: (bi, prev_blk(i), 0)),
            pl.BlockSpec((1, tstep, SWA_KV_HEADS * LANES), lambda bi, i: (bi, i, 0)),
            pl.BlockSpec((1, 1, SWA_KV_COLS, WINDOW), lambda bi, i: (bi, prev_blk(i), 0, 0)),
            pl.BlockSpec((1, nsub, SWA_KV_COLS, WINDOW), lambda bi, i: (bi, i, 0, 0)),
        ],
        out_specs=pl.BlockSpec((1, tstep, SWA_Q_COLS), lambda bi, i: (bi, i, 0)),
        out_shape=jax.ShapeDtypeStruct((b, s, SWA_Q_COLS), BF16),
        compiler_params=pltpu.CompilerParams(
            dimension_semantics=("parallel", "arbitrary"), vmem_limit_bytes=VMEM_LIMIT_BYTES),
        name="swa_attn",
    )(sink_rows, bias, sq, sk, sk, svt, svt)


def _post_kernel(x_ref, mla_ref, swa_ref, mod_ref, woa_ref, wob_ref, gmlp_ref, wup_ref, wdown_ref, gfin_ref,
                 o_ref, *, ts, ff_chunk):
    mod = mod_ref[0]
    g1, sh2, sc2, g2 = mod[2:3], mod[3:4], mod[4:5], mod[5:6]
    tiles = [slice(r0, r0 + ts) for r0 in range(0, x_ref.shape[1], ts)]
    mid = []
    for rows in tiles:
        attn = (jnp.dot(mla_ref[0, rows], woa_ref[...], preferred_element_type=F32)
                + jnp.dot(swa_ref[0, rows], wob_ref[...], preferred_element_type=F32))
        x1 = x_ref[0, rows] + g1 * attn
        h2 = (_rms(x1) * gmlp_ref[...] * (1.0 + sc2) + sh2).astype(BF16)
        mid.append((x1, h2))
    for rows, (x1, h2) in zip(tiles, mid):
        acc = jnp.zeros_like(x1)
        for c in range(D_FF // ff_chunk):
            cs = slice(c * ff_chunk, (c + 1) * ff_chunk)
            u = jnp.maximum(jnp.dot(h2, wup_ref[:, cs], preferred_element_type=F32), 0.0)
            acc = acc + jnp.dot((u * u).astype(BF16), wdown_ref[cs, :], preferred_element_type=F32)
        x2 = x1 + g2 * acc
        o_ref[0, rows] = _rms(x2) * gfin_ref[...]


def _post_call(x, mla, swa, mod, woa, wob, gmlp, wup, wdown, gfin, tm, ts, ff_chunk):
    b, s, d = x.shape
    row = lambda bi, i: (bi, i, 0)
    return pl.pallas_call(
        functools.partial(_post_kernel, ts=ts, ff_chunk=ff_chunk),
        grid=(b, s // tm),
        in_specs=[
            pl.BlockSpec((1, tm, d), row),
            pl.BlockSpec((1, tm, MLA_V_COLS), row),
            pl.BlockSpec((1, tm, SWA_Q_COLS), row),
            pl.BlockSpec((1, N_MOD, d), lambda bi, i: (bi, 0, 0)),
            _const_spec(woa.shape),
            _const_spec(wob.shape),
            _const_spec((1, d)),
            _const_spec(wup.shape),
            _const_spec(wdown.shape),
            _const_spec((1, d)),
        ],
        out_specs=pl.BlockSpec((1, tm, d), row),
        out_shape=jax.ShapeDtypeStruct((b, s, d), F32),
        compiler_params=pltpu.CompilerParams(
            dimension_semantics=("parallel", "parallel"), vmem_limit_bytes=VMEM_LIMIT_BYTES),
        name="post_attn",
    )(x, mla, swa, mod, woa, wob, gmlp, wup, wdown, gfin)


def _prep_layer_weights(w_in, w_qb, w_kvb, w_o):
    o1 = Q_LORA
    o2 = o1 + KV_LORA
    o3 = o2 + MLA_ROPE
    o4 = o3 + SWA_Q_COLS
    d = w_in.shape[0]
    half = MLA_ROPE // 2
    pad = LANES - MLA_NOPE - MLA_ROPE - half
    kpe = jnp.concatenate([jnp.zeros((d, MLA_NOPE), w_in.dtype), w_in[:, o2:o3], w_in[:, o2:o2 + half],
                           jnp.zeros((d, pad), w_in.dtype)], axis=1)
    sq = w_in[:, o3:o4].reshape(d, SWA_KV_HEADS, SWA_GROUP, SWA_HEAD_DIM)
    sq = sq.transpose(0, 2, 1, 3).reshape(d, SWA_Q_COLS)
    win = jnp.concatenate([w_in[:, :o2], kpe, sq, w_in[:, o4:]], axis=1).astype(BF16)

    wq = jnp.concatenate([w_qb, w_qb[:, :, MLA_NOPE:MLA_NOPE + half],
                          jnp.zeros(w_qb.shape[:2] + (pad,), w_qb.dtype)], axis=2)
    wq = wq.reshape(Q_LORA, MLA_QK_COLS).astype(BF16)
    wk = w_kvb[:, :, :MLA_NOPE].reshape(KV_LORA, MLA_HEADS * MLA_NOPE).astype(BF16)
    wv = w_kvb[:, :, MLA_NOPE:].reshape(KV_LORA, MLA_V_COLS).T.astype(BF16)

    woa = w_o[:MLA_V_COLS].astype(BF16)
    wob = w_o[MLA_V_COLS:].reshape(SWA_KV_HEADS, SWA_GROUP, SWA_HEAD_DIM, -1)
    wob = wob.transpose(1, 0, 2, 3).reshape(SWA_Q_COLS, -1).astype(BF16)
    return win, wq, wk, wv, woa, wob


def kernel(x, c, w_ada, b_ada, norm_mix_g, w_in, g_qa, w_qb, g_kva, w_kvb, sinks,
           w_o, norm_mlp_g, w_up, w_down, final_g):
    depth = w_ada.shape[0]
    b = x.shape[0]
    assert depth == 1, "the final rmsnorm is fused into the single layer's post-attention call"
    for l in range(depth):
        mod = _ada_call(c, w_ada[l], b_ada[l]).reshape(b, N_MOD, D_MODEL)
        win, wq, wk, wv, woa, wob = _prep_layer_weights(w_in[l], w_qb[l], w_kvb[l], w_o[l])
        qf, kf, vt, sq, sk, svt = _pre_call(
            x, mod, norm_mix_g[l][None], win, g_qa[l][None], wq, g_kva[l][None], wk, wv,
            tm=1024, ts=256, tk=256)
        mla = _mla_call(qf, kf, vt, tq=256, hps=2, qsets=(tuple(range(SEQ // 256)),))
        swa = _swa_call(sinks[l], sq, sk, svt, tstep=1024)
        x = _post_call(x, mla, swa, mod, woa, wob, norm_mlp_g[l][None],
                       w_up[l].astype(BF16), w_down[l].astype(BF16), final_g[None],
                       tm=1024, ts=256, ff_chunk=1024)
    return x
```

```python
import functools

import numpy as np
import jax
import jax.numpy as jnp
from jax import lax
from jax.experimental import pallas as pl
from jax.experimental.pallas import tpu as pltpu

D_MODEL = 1024
SEQ = 2048
MLA_HEADS = 8
MLA_NOPE = 64
MLA_ROPE = 32
MLA_V = 64
Q_LORA = 384
KV_LORA = 256
ROPE_THETA = 10000.0
SWA_HEADS = 8
SWA_KV_HEADS = 2
SWA_GROUP = SWA_HEADS // SWA_KV_HEADS
SWA_HEAD_DIM = 64
WINDOW = 128
D_FF = 4 * D_MODEL
EPS = 1e-6
N_MOD = 6

LANES = 128
HALF = LANES // 2
BF16_SUBLANES = 16
MLA_QK_COLS = MLA_HEADS * LANES
MLA_V_COLS = MLA_HEADS * MLA_V
SWA_Q_COLS = SWA_HEADS * SWA_HEAD_DIM
SWA_KV_COLS = SWA_KV_HEADS * SWA_HEAD_DIM
C_QLAT = 0
C_KVLAT = C_QLAT + Q_LORA
C_KPE = C_KVLAT + KV_LORA
C_SQ = C_KPE + LANES
C_SK = C_SQ + SWA_Q_COLS
C_SV = C_SK + SWA_KV_COLS
IN_COLS_PAD = C_SV + SWA_KV_COLS

VMEM_LIMIT_BYTES = 56 * 1024 * 1024

LOG2E = float(np.log2(np.e))
MLA_Q_SCALE = (MLA_NOPE + MLA_ROPE) ** -0.5 * LOG2E
SWA_Q_SCALE = SWA_HEAD_DIM ** -0.5 * LOG2E

BF16 = jnp.bfloat16
F32 = jnp.float32


def _const_spec(shape):
    nd = len(shape)
    return pl.BlockSpec(shape, lambda *_: (0,) * nd, pipeline_mode=pl.Buffered(1))


def _rms(x):
    return x * lax.rsqrt(jnp.mean(x * x, axis=-1, keepdims=True) + EPS)


def _ada_kernel(c_ref, w_ref, b_ref, o_ref):
    c = c_ref[...]
    s = c / (1.0 + jnp.exp(-c))
    o_ref[...] = jnp.dot(s.astype(BF16), w_ref[...].astype(BF16), preferred_element_type=F32) + b_ref[...]


def _ada_call(c, w_ada, b_ada):
    b = c.shape[0]
    n = w_ada.shape[1]
    tn = D_MODEL
    return pl.pallas_call(
        _ada_kernel,
        grid=(n // tn,),
        in_specs=[
            pl.BlockSpec((b, D_MODEL), lambda j: (0, 0)),
            pl.BlockSpec((D_MODEL, tn), lambda j: (0, j)),
            pl.BlockSpec((1, tn), lambda j: (0, j)),
        ],
        out_specs=pl.BlockSpec((b, tn), lambda j: (0, j)),
        out_shape=jax.ShapeDtypeStruct((b, n), F32),
        compiler_params=pltpu.CompilerParams(dimension_semantics=("parallel",)),
        name="ada_mod",
    )(c, w_ada, b_ada.reshape(1, n))


def _rope_group(xg, keep, swap):
    up = pltpu.roll(xg, LANES - MLA_ROPE // 2, 1)
    return xg * keep + up * swap


def _pre_kernel(x_ref, mod_ref, gmix_ref, win_ref, gqa_ref, wq_ref, gkva_ref, wk_ref, wv_ref,
                ropeq_ref, ropek_ref,
                qf_ref, kf_ref, v_ref, sq_ref, sk_ref, sv_ref, *, ts):
    for r0 in range(0, x_ref.shape[1], ts):
        _pre_rows(r0, ts, x_ref, mod_ref, gmix_ref, win_ref, gqa_ref, wq_ref, gkva_ref, wk_ref, wv_ref,
                  ropeq_ref, ropek_ref, qf_ref, kf_ref, v_ref, sq_ref, sk_ref, sv_ref)


def _pre_rows(r0, ts, x_ref, mod_ref, gmix_ref, win_ref, gqa_ref, wq_ref, gkva_ref, wk_ref, wv_ref,
              ropeq_ref, ropek_ref, qf_ref, kf_ref, v_ref, sq_ref, sk_ref, sv_ref):
    rows = slice(r0, r0 + ts)
    x = x_ref[0, rows]
    mod = mod_ref[0]
    sh1 = mod[0:1]
    sc1 = mod[1:2]
    h = (_rms(x) * gmix_ref[...] * (1.0 + sc1) + sh1).astype(BF16)
    proj = jnp.dot(h, win_ref[...], preferred_element_type=F32)

    for g in range(SWA_GROUP):
        sq_ref[0, g, rows] = (proj[:, C_SQ + g * LANES:C_SQ + (g + 1) * LANES] * SWA_Q_SCALE).astype(BF16)
    swa_k = proj[:, C_SK:C_SV]
    low = lax.broadcasted_iota(jnp.int32, swa_k.shape, 1) < HALF
    sk_ref[0, rows, :LANES] = jnp.where(low, swa_k, 0.0).astype(BF16)
    sk_ref[0, rows, LANES:] = jnp.where(low, 0.0, swa_k).astype(BF16)
    svt = proj[:, C_SV:IN_COLS_PAD].T.astype(BF16)
    for jb in range(ts // WINDOW):
        sv_ref[0, r0 // WINDOW + jb] = svt[:, jb * WINDOW:(jb + 1) * WINDOW]

    qn = (_rms(proj[:, C_QLAT:C_KVLAT]) * gqa_ref[...]).astype(BF16)
    kvn = (_rms(proj[:, C_KVLAT:C_KPE]) * gkva_ref[...]).astype(BF16)
    q = jnp.dot(qn, wq_ref[...], preferred_element_type=F32)
    kn = jnp.dot(kvn, wk_ref[...], preferred_element_type=F32)
    vt = lax.dot_general(wv_ref[...], kvn, (((1,), (1,)), ((), ())), preferred_element_type=F32).astype(BF16)
    tk = v_ref.shape[-1]
    for hp in range(MLA_HEADS // 2):
        for jb in range(ts // tk):
            v_ref[0, hp, r0 // tk + jb] = vt[hp * LANES:(hp + 1) * LANES, jb * tk:(jb + 1) * tk]

    q_keep, q_swap = ropeq_ref[0, rows], ropeq_ref[1, rows]
    kpe = _rope_group(proj[:, C_KPE:C_SQ], ropek_ref[0, rows], ropek_ref[1, rows])
    nope_lanes = lax.broadcasted_iota(jnp.int32, kpe.shape, 1) < MLA_NOPE
    for hd in range(MLA_HEADS):
        sl = slice(hd * LANES, (hd + 1) * LANES)
        qf_ref[0, rows, sl] = _rope_group(q[:, sl], q_keep, q_swap).astype(BF16)
        kn_pair = kn[:, (hd // 2) * LANES:(hd // 2 + 1) * LANES]
        if hd % 2:
            kn_pair = pltpu.roll(kn_pair, HALF, 1)
        kf_ref[0, rows, sl] = jnp.where(nope_lanes, kn_pair, kpe).astype(BF16)


def _rope_tables(scale_q):
    half = MLA_ROPE // 2
    freqs = ROPE_THETA ** (-np.arange(0, MLA_ROPE, 2, dtype=np.float64) / MLA_ROPE)
    ang = np.arange(SEQ, dtype=np.float64)[:, None] * freqs[None, :]
    cos, sin = np.cos(ang), np.sin(ang)
    keep = np.zeros((SEQ, LANES))
    swap = np.zeros((SEQ, LANES))
    keep[:, :MLA_NOPE] = 1.0
    keep[:, MLA_NOPE:MLA_NOPE + half] = cos
    keep[:, MLA_NOPE + half:MLA_NOPE + MLA_ROPE] = cos
    swap[:, MLA_NOPE:MLA_NOPE + half] = -sin
    swap[:, MLA_NOPE + half:MLA_NOPE + MLA_ROPE] = sin
    return (np.stack([keep, swap]) * scale_q).astype(np.float32)


def _pre_call(x, mod, gmix, win, gqa, wq, gkva, wk, wv, tm, ts, tk):
    b, s, d = x.shape
    pairs = MLA_HEADS // 2
    ropeq = jnp.asarray(_rope_tables(MLA_Q_SCALE))
    ropek = jnp.asarray(_rope_tables(1.0))
    row = lambda bi, i: (bi, i, 0)
    out_shapes = (
        jax.ShapeDtypeStruct((b, s, MLA_QK_COLS), BF16),
        jax.ShapeDtypeStruct((b, s, MLA_QK_COLS), BF16),
        jax.ShapeDtypeStruct((b, pairs, s // tk, LANES, tk), BF16),
        jax.ShapeDtypeStruct((b, SWA_GROUP, s, LANES), BF16),
        jax.ShapeDtypeStruct((b, s, SWA_KV_HEADS * LANES), BF16),
        jax.ShapeDtypeStruct((b, s // WINDOW, SWA_KV_COLS, WINDOW), BF16),
    )
    out_specs = [
        pl.BlockSpec((1, tm, MLA_QK_COLS), row),
        pl.BlockSpec((1, tm, MLA_QK_COLS), row),
        pl.BlockSpec((1, pairs, tm // tk, LANES, tk), lambda bi, i: (bi, 0, i, 0, 0)),
        pl.BlockSpec((1, SWA_GROUP, tm, LANES), lambda bi, i: (bi, 0, i, 0)),
        pl.BlockSpec((1, tm, SWA_KV_HEADS * LANES), row),
        pl.BlockSpec((1, tm // WINDOW, SWA_KV_COLS, WINDOW), lambda bi, i: (bi, i, 0, 0)),
    ]
    return pl.pallas_call(
        functools.partial(_pre_kernel, ts=ts),
        grid=(b, s // tm),
        in_specs=[
            pl.BlockSpec((1, tm, d), row),
            pl.BlockSpec((1, N_MOD, d), lambda bi, i: (bi, 0, 0)),
            _const_spec((1, d)),
            _const_spec(win.shape),
            _const_spec((1, Q_LORA)),
            _const_spec(wq.shape),
            _const_spec((1, KV_LORA)),
            _const_spec(wk.shape),
            _const_spec(wv.shape),
            pl.BlockSpec((2, tm, LANES), lambda bi, i: (0, i, 0)),
            pl.BlockSpec((2, tm, LANES), lambda bi, i: (0, i, 0)),
        ],
        out_specs=tuple(out_specs),
        out_shape=out_shapes,
        compiler_params=pltpu.CompilerParams(
            dimension_semantics=("parallel", "parallel"), vmem_limit_bytes=VMEM_LIMIT_BYTES),
        name="pre_attn",
    )(x, mod, gmix, win, gqa, wq, gkva, wk, wv, ropeq, ropek)


def _mla_kernel(q_ref, k_ref, vt_ref, o_ref, *, tq, qsets):
    if len(qsets) == 1:
        _mla_blocks(q_ref, k_ref, vt_ref, o_ref, qblocks=qsets[0], tq=tq)
        return
    for v, qblocks in enumerate(qsets):
        pl.when(pl.program_id(2) == v)(
            functools.partial(_mla_blocks, q_ref, k_ref, vt_ref, o_ref, qblocks=qblocks, tq=tq))


def _mla_blocks(q_ref, k_ref, vt_ref, o_ref, *, qblocks, tq):
    ki = lax.broadcasted_iota(jnp.int32, (tq, tq), 0)
    qi = lax.broadcasted_iota(jnp.int32, (tq, tq), 1)
    causal = ki <= qi
    heads = range(q_ref.shape[2] // LANES)
    steps = [(i, j) for i in qblocks for j in range(i + 1)]

    def scores(i, j):
        out = []
        for h in heads:
            cols = slice(h * LANES, (h + 1) * LANES)
            q = q_ref[0, i * tq:(i + 1) * tq, cols]
            k = k_ref[0, j * tq:(j + 1) * tq, cols]
            out.append(lax.dot_general(k, q, (((1,), (1,)), ((), ())), preferred_element_type=F32))
        return out

    ones = jnp.ones((BF16_SUBLANES, tq), BF16)
    st_next = scores(*steps[0])
    for n, (i, j) in enumerate(steps):
        st_cur = st_next
        if n + 1 < len(steps):
            st_next = scores(*steps[n + 1])
        if j == 0:
            state = [(jnp.full((1, tq), -jnp.inf, F32), jnp.zeros((MLA_V + BF16_SUBLANES, tq), F32))
                     for _ in heads]
        for h in heads:
            m, acc = state[h]
            st = st_cur[h]
            if j == i:
                st = jnp.where(causal, st, -jnp.inf)
            vt = vt_ref[0, h // 2, j, (h % 2) * MLA_V:(h % 2 + 1) * MLA_V, :]
            vt = jnp.concatenate([vt, ones], axis=0)
            m_new = jnp.maximum(m, jnp.max(st, axis=0, keepdims=True))
            alpha = jnp.exp2(m - m_new)
            pt = jnp.exp2(st - m_new).astype(BF16)
            acc_new = alpha * acc + jnp.dot(vt, pt, preferred_element_type=F32)
            state[h] = (m_new, acc_new)
        if j == i:
            for p in range(len(heads) // 2):
                pair = [state[2 * p][1], state[2 * p + 1][1]]
                out_t = jnp.concatenate([acc[:MLA_V] / acc[MLA_V:MLA_V + 1] for acc in pair], axis=0)
                o_ref[0, i * tq:(i + 1) * tq, p * LANES:(p + 1) * LANES] = out_t.T.astype(BF16)


def _mla_call(qf, kf, vt, tq, hps, qsets):
    b, s, _ = qf.shape
    pairs = MLA_HEADS // 2
    assert vt.shape == (b, pairs, s // tq, LANES, tq) and pairs % hps == 0
    assert sorted(i for qs in qsets for i in qs) == list(range(s // tq))
    seq_cols = lambda bi, g, v: (bi, 0, g)
    return pl.pallas_call(
        functools.partial(_mla_kernel, tq=tq, qsets=qsets),
        grid=(b, pairs // hps, len(qsets)),
        in_specs=[
            pl.BlockSpec((1, s, 2 * hps * LANES), seq_cols),
            pl.BlockSpec((1, s, 2 * hps * LANES), seq_cols),
            pl.BlockSpec((1, hps, s // tq, LANES, tq), lambda bi, g, v: (bi, g, 0, 0, 0)),
        ],
        out_specs=pl.BlockSpec((1, s, hps * LANES), seq_cols),
        out_shape=jax.ShapeDtypeStruct((b, s, MLA_V_COLS), BF16),
        compiler_params=pltpu.CompilerParams(
            dimension_semantics=("parallel", "parallel", "arbitrary"), vmem_limit_bytes=VMEM_LIMIT_BYTES),
        name="mla_attn",
    )(qf, kf, vt)


def _alibi_slope(h):
    return 2.0 ** (-8.0 * (h + 1) / SWA_HEADS)


SWA_STACK = 2
SWA_LOOKAHEAD = 4


def _swa_bias():
    k = np.arange(2 * WINDOW)[:, None]
    q = np.arange(WINDOW)[None, :]
    dist = q + WINDOW - k
    valid = (dist >= 0) & (dist < WINDOW)
    out = np.empty((SWA_KV_HEADS, 2 * WINDOW, SWA_GROUP * WINDOW), np.float32)
    for kv in range(SWA_KV_HEADS):
        for g in range(SWA_GROUP):
            slope = np.float32(_alibi_slope(kv * SWA_GROUP + g))
            out[kv, :, g * WINDOW:(g + 1) * WINDOW] = np.where(valid, -slope * dist.astype(np.float32) * LOG2E, -np.inf)
    return out


def _swa_kernel(sink_ref, bias_ref, q_ref, kp_ref, k_ref, vtp_ref, vt_ref, o_ref):
    i = pl.program_id(1)
    nsub = q_ref.shape[2] // WINDOW
    ones = jnp.ones((BF16_SUBLANES, 2 * WINDOW), BF16)
    width = SWA_STACK * WINDOW
    gone = jnp.where(i == 0, -jnp.inf, 0.0)
    chains = [(c, g0, kv) for c in range(nsub) for g0 in range(0, SWA_GROUP, SWA_STACK)
              for kv in range(SWA_KV_HEADS)]

    def scores(c, g0, kv):
        cols = slice(kv * LANES, (kv + 1) * LANES)
        if c == 0:
            k_win = jnp.concatenate([kp_ref[0, :, cols], k_ref[0, :WINDOW, cols]], axis=0)
        else:
            k_win = k_ref[0, (c - 1) * WINDOW:(c + 1) * WINDOW, cols]
        q_st = q_ref[0, g0:g0 + SWA_STACK, c * WINDOW:(c + 1) * WINDOW, :].reshape(width, LANES)
        st = lax.dot_general(k_win, q_st, (((1,), (1,)), ((), ())), preferred_element_type=F32)
        return st + bias_ref[kv, :, g0 * WINDOW:g0 * WINDOW + width]

    outs = {}
    done = []
    pending = [scores(*ch) for ch in chains[:SWA_LOOKAHEAD]]
    for n, (c, g0, kv) in enumerate(chains):
        st = pending.pop(0)
        if n + SWA_LOOKAHEAD < len(chains):
            pending.append(scores(*chains[n + SWA_LOOKAHEAD]))
        rows = slice(kv * SWA_HEAD_DIM, (kv + 1) * SWA_HEAD_DIM)
        if c == 0:
            vt = jnp.concatenate([vtp_ref[0, 0, rows, :], vt_ref[0, 0, rows, :]], axis=1)
        else:
            vt = jnp.concatenate([vt_ref[0, c - 1, rows, :], vt_ref[0, c, rows, :]], axis=1)
        if c == 0:
            st = jnp.concatenate([st[:WINDOW] + gone, st[WINDOW:]], axis=0)
        sink = sink_ref[kv:kv + 1, g0 * WINDOW:g0 * WINDOW + width]
        m = jnp.maximum(jnp.max(st, axis=0, keepdims=True), sink)
        pt = jnp.exp2(st - m).astype(BF16)
        res = jnp.dot(jnp.concatenate([vt, ones], axis=0), pt, preferred_element_type=F32)
        denom = res[SWA_HEAD_DIM:SWA_HEAD_DIM + 1] + jnp.exp2(sink - m)
        outs[kv] = res[:SWA_HEAD_DIM] / denom
        if kv == SWA_KV_HEADS - 1:
            for gg in range(SWA_STACK):
                gs = slice(gg * WINDOW, (gg + 1) * WINDOW)
                out_t = jnp.concatenate([outs[0][:, gs], outs[1][:, gs]], axis=0)
                done.append((c, g0 + gg, out_t))
    for c, g, out_t in done:
        o_ref[0, c * WINDOW:(c + 1) * WINDOW, g * LANES:(g + 1) * LANES] = out_t.T.astype(BF16)


def _swa_call(sinks, sq, sk, svt, tstep):
    b, _, s, _ = sq.shape
    nsub = tstep // WINDOW
    bias = jnp.asarray(_swa_bias())
    sink_rows = jnp.repeat(sinks.reshape(SWA_KV_HEADS, SWA_GROUP), WINDOW, axis=1) * LOG2E
    prev_blk = lambda i: jnp.maximum(i * nsub - 1, 0)
    return pl.pallas_call(
        _swa_kernel,
        grid=(b, s // tstep),
        in_specs=[
            _const_spec(sink_rows.shape),
            _const_spec(bias.shape),
            pl.BlockSpec((1, SWA_GROUP, tstep, LANES), lambda bi, i: (bi, 0, i, 0)),
            pl.BlockSpec((1, WINDOW, SWA_KV_HEADS * LANES), lambda bi, i: (bi, prev_blk(i), 0)),
            pl.BlockSpec((1, tstep, SWA_KV_HEADS * LANES), lambda bi, i: (bi, i, 0)),
            pl.BlockSpec((1, 1, SWA_KV_COLS, WINDOW), lambda bi, i: (bi, prev_blk(i), 0, 0)),
            pl.BlockSpec((1, nsub, SWA_KV_COLS, WINDOW), lambda bi, i: (bi, i, 0, 0)),
        ],
        out_specs=pl.BlockSpec((1, tstep, SWA_Q_COLS), lambda bi, i: (bi, i, 0)),
        out_shape=jax.ShapeDtypeStruct((b, s, SWA_Q_COLS), BF16),
        compiler_params=pltpu.CompilerParams(
            dimension_semantics=("parallel", "arbitrary"), vmem_limit_bytes=VMEM_LIMIT_BYTES),
        name="swa_attn",
    )(sink_rows, bias, sq, sk, sk, svt, svt)


def _post_kernel(x_ref, mla_ref, swa_ref, mod_ref, woa_ref, wob_ref, gmlp_ref, wup_ref, wdown_ref, gfin_ref,
                 o_ref, *, ts, ff_chunk):
    mod = mod_ref[0]
    g1, sh2, sc2, g2 = mod[2:3], mod[3:4], mod[4:5], mod[5:6]
    tiles = [slice(r0, r0 + ts) for r0 in range(0, x_ref.shape[1], ts)]
    mid = []
    for rows in tiles:
        attn = (jnp.dot(mla_ref[0, rows], woa_ref[...], preferred_element_type=F32)
                + jnp.dot(swa_ref[0, rows], wob_ref[...], preferred_element_type=F32))
        x1 = x_ref[0, rows] + g1 * attn
        h2 = (_rms(x1) * gmlp_ref[...] * (1.0 + sc2) + sh2).astype(BF16)
        mid.append((x1, h2))
    accs = [jnp.zeros_like(x1) for x1, _ in mid]
    for c in range(D_FF // ff_chunk):
        cs = slice(c * ff_chunk, (c + 1) * ff_chunk)
        w_up_c = wup_ref[:, cs].astype(BF16)
        w_down_c = wdown_ref[cs, :].astype(BF16)
        for t, (_, h2) in enumerate(mid):
            u = jnp.maximum(jnp.dot(h2, w_up_c, preferred_element_type=F32), 0.0)
            accs[t] = accs[t] + jnp.dot((u * u).astype(BF16), w_down_c, preferred_element_type=F32)
    for rows, (x1, _), acc in zip(tiles, mid, accs):
        x2 = x1 + g2 * acc
        o_ref[0, rows] = _rms(x2) * gfin_ref[...]


def _post_call(x, mla, swa, mod, woa, wob, gmlp, wup, wdown, gfin, tm, ts, ff_chunk):
    b, s, d = x.shape
    row = lambda bi, i: (bi, i, 0)
    return pl.pallas_call(
        functools.partial(_post_kernel, ts=ts, ff_chunk=ff_chunk),
        grid=(b, s // tm),
        in_specs=[
            pl.BlockSpec((1, tm, d), row),
            pl.BlockSpec((1, tm, MLA_V_COLS), row),
            pl.BlockSpec((1, tm, SWA_Q_COLS), row),
            pl.BlockSpec((1, N_MOD, d), lambda bi, i: (bi, 0, 0)),
            _const_spec(woa.shape),
            _const_spec(wob.shape),
            _const_spec((1, d)),
            _const_spec(wup.shape),
            _const_spec(wdown.shape),
            _const_spec((1, d)),
        ],
        out_specs=pl.BlockSpec((1, tm, d), row),
        out_shape=jax.ShapeDtypeStruct((b, s, d), F32),
        compiler_params=pltpu.CompilerParams(
            dimension_semantics=("parallel", "parallel"), vmem_limit_bytes=VMEM_LIMIT_BYTES),
        name="post_attn",
    )(x, mla, swa, mod, woa, wob, gmlp, wup, wdown, gfin)


def _prep_layer_weights(w_in, w_qb, w_kvb, w_o):
    o1 = Q_LORA
    o2 = o1 + KV_LORA
    o3 = o2 + MLA_ROPE
    o4 = o3 + SWA_Q_COLS
    d = w_in.shape[0]
    half = MLA_ROPE // 2
    pad = LANES - MLA_NOPE - MLA_ROPE - half
    kpe = jnp.concatenate([jnp.zeros((d, MLA_NOPE), w_in.dtype), w_in[:, o2:o3], w_in[:, o2:o2 + half],
                           jnp.zeros((d, pad), w_in.dtype)], axis=1)
    sq = w_in[:, o3:o4].reshape(d, SWA_KV_HEADS, SWA_GROUP, SWA_HEAD_DIM)
    sq = sq.transpose(0, 2, 1, 3).reshape(d, SWA_Q_COLS)
    win = jnp.concatenate([w_in[:, :o2], kpe, sq, w_in[:, o4:]], axis=1).astype(BF16)

    wq = jnp.concatenate([w_qb, w_qb[:, :, MLA_NOPE:MLA_NOPE + half],
                          jnp.zeros(w_qb.shape[:2] + (pad,), w_qb.dtype)], axis=2)
    wq = wq.reshape(Q_LORA, MLA_QK_COLS).astype(BF16)
    wk = w_kvb[:, :, :MLA_NOPE].reshape(KV_LORA, MLA_HEADS * MLA_NOPE).astype(BF16)
    wv = w_kvb[:, :, MLA_NOPE:].reshape(KV_LORA, MLA_V_COLS).T.astype(BF16)

    woa = w_o[:MLA_V_COLS].astype(BF16)
    wob = w_o[MLA_V_COLS:].reshape(SWA_KV_HEADS, SWA_GROUP, SWA_HEAD_DIM, -1)
    wob = wob.transpose(1, 0, 2, 3).reshape(SWA_Q_COLS, -1).astype(BF16)
    return win, wq, wk, wv, woa, wob


def kernel(x, c, w_ada, b_ada, norm_mix_g, w_in, g_qa, w_qb, g_kva, w_kvb, sinks,
           w_o, norm_mlp_g, w_up, w_down, final_g):
    depth = w_ada.shape[0]
    b = x.shape[0]
    assert depth == 1, "the final rmsnorm is fused into the single layer's post-attention call"
    for l in range(depth):
        mod = _ada_call(c, w_ada[l], b_ada[l]).reshape(b, N_MOD, D_MODEL)
        win, wq, wk, wv, woa, wob = _prep_layer_weights(w_in[l], w_qb[l], w_kvb[l], w_o[l])
        qf, kf, vt, sq, sk, svt = _pre_call(
            x, mod, norm_mix_g[l][None], win, g_qa[l][None], wq, g_kva[l][None], wk, wv,
            tm=1024, ts=256, tk=256)
        mla = _mla_call(qf, kf, vt, tq=256, hps=2, qsets=(tuple(range(SEQ // 256)),))
        swa = _swa_call(sinks[l], sq, sk, svt, tstep=1024)
        x = _post_call(x, mla, swa, mod, woa, wob, norm_mlp_g[l][None],
                       w_up[l], w_down[l], final_g[None], tm=512, ts=256, ff_chunk=1024)
    return x
```

```python
import functools

import numpy as np
import jax
import jax.numpy as jnp
from jax import lax
from jax.experimental import pallas as pl
from jax.experimental.pallas import tpu as pltpu

D_MODEL = 1024
SEQ = 2048
MLA_HEADS = 8
MLA_NOPE = 64
MLA_ROPE = 32
MLA_V = 64
Q_LORA = 384
KV_LORA = 256
ROPE_THETA = 10000.0
SWA_HEADS = 8
SWA_KV_HEADS = 2
SWA_GROUP = SWA_HEADS // SWA_KV_HEADS
SWA_HEAD_DIM = 64
WINDOW = 128
D_FF = 4 * D_MODEL
EPS = 1e-6
N_MOD = 6

LANES = 128
HALF = LANES // 2
BF16_SUBLANES = 16
MLA_QK_COLS = MLA_HEADS * LANES
MLA_V_COLS = MLA_HEADS * MLA_V
SWA_Q_COLS = SWA_HEADS * SWA_HEAD_DIM
SWA_KV_COLS = SWA_KV_HEADS * SWA_HEAD_DIM
C_QLAT = 0
C_KVLAT = C_QLAT + Q_LORA
C_KPE = C_KVLAT + KV_LORA
C_SQ = C_KPE + LANES
C_SK = C_SQ + SWA_Q_COLS
C_SV = C_SK + SWA_KV_COLS
IN_COLS_PAD = C_SV + SWA_KV_COLS

VMEM_LIMIT_BYTES = 56 * 1024 * 1024

LOG2E = float(np.log2(np.e))
MLA_Q_SCALE = (MLA_NOPE + MLA_ROPE) ** -0.5 * LOG2E
SWA_Q_SCALE = SWA_HEAD_DIM ** -0.5 * LOG2E

BF16 = jnp.bfloat16
F32 = jnp.float32


def _const_spec(shape):
    nd = len(shape)
    return pl.BlockSpec(shape, lambda *_: (0,) * nd, pipeline_mode=pl.Buffered(1))


def _rms(x):
    return x * lax.rsqrt(jnp.mean(x * x, axis=-1, keepdims=True) + EPS)


def _ada_kernel(c_ref, w_ref, b_ref, o_ref):
    c = c_ref[...]
    s = c / (1.0 + jnp.exp(-c))
    o_ref[...] = jnp.dot(s.astype(BF16), w_ref[...].astype(BF16), preferred_element_type=F32) + b_ref[...]


def _ada_call(c, w_ada, b_ada):
    b = c.shape[0]
    n = w_ada.shape[1]
    tn = D_MODEL
    return pl.pallas_call(
        _ada_kernel,
        grid=(n // tn,),
        in_specs=[
            pl.BlockSpec((b, D_MODEL), lambda j: (0, 0)),
            pl.BlockSpec((D_MODEL, tn), lambda j: (0, j)),
            pl.BlockSpec((1, tn), lambda j: (0, j)),
        ],
        out_specs=pl.BlockSpec((b, tn), lambda j: (0, j)),
        out_shape=jax.ShapeDtypeStruct((b, n), F32),
        compiler_params=pltpu.CompilerParams(dimension_semantics=("parallel",)),
        name="ada_mod",
    )(c, w_ada, b_ada.reshape(1, n))


def _rope_group(xg, keep, swap):
    up = pltpu.roll(xg, LANES - MLA_ROPE // 2, 1)
    return xg * keep + up * swap


def _relayout_weights(w_in_ref, w_qb_ref, w_kvb_ref, win_ref, wq_ref, wk_ref, wv_ref):
    o2 = Q_LORA + KV_LORA
    o3 = o2 + MLA_ROPE
    o4 = o3 + SWA_Q_COLS
    half = MLA_ROPE // 2
    pad = LANES - MLA_NOPE - MLA_ROPE - half
    chunk = 256
    for r0 in range(0, w_in_ref.shape[0], chunk):
        w = w_in_ref[r0:r0 + chunk, :]
        zeros = lambda n: jnp.zeros((chunk, n), F32)
        kpe = [zeros(MLA_NOPE), w[:, o2:o3], w[:, o2:o2 + half], zeros(pad)]
        sq = [w[:, o3 + (kv * SWA_GROUP + g) * SWA_HEAD_DIM:o3 + (kv * SWA_GROUP + g + 1) * SWA_HEAD_DIM]
              for g in range(SWA_GROUP) for kv in range(SWA_KV_HEADS)]
        win_ref[r0:r0 + chunk, :] = jnp.concatenate([w[:, :o2]] + kpe + sq + [w[:, o4:]], axis=1).astype(BF16)
    wqb = w_qb_ref[...]
    hd_cols = MLA_NOPE + MLA_ROPE
    parts = []
    for hd in range(MLA_HEADS):
        c0 = hd * hd_cols
        parts += [wqb[:, c0:c0 + hd_cols], wqb[:, c0 + MLA_NOPE:c0 + MLA_NOPE + half],
                  jnp.zeros((wqb.shape[0], pad), F32)]
    wq_ref[...] = jnp.concatenate(parts, axis=1).astype(BF16)
    wkvb = w_kvb_ref[...]
    wk_ref[...] = jnp.concatenate([wkvb[:, hd * LANES:hd * LANES + MLA_NOPE] for hd in range(MLA_HEADS)],
                                  axis=1).astype(BF16)
    wv = jnp.concatenate([wkvb[:, hd * LANES + MLA_NOPE:(hd + 1) * LANES] for hd in range(MLA_HEADS)], axis=1)
    wv_ref[...] = wv.T.astype(BF16)


def _pre_kernel(x_ref, mod_ref, gmix_ref, w_in_ref, gqa_ref, w_qb_ref, gkva_ref, w_kvb_ref,
                ropeq_ref, ropek_ref,
                qf_ref, kf_ref, v_ref, sq_ref, sk_ref, sv_ref,
                win_ref, wq_ref, wk_ref, wv_ref, *, ts):
    @pl.when((pl.program_id(0) == 0) & (pl.program_id(1) == 0))
    def _():
        _relayout_weights(w_in_ref, w_qb_ref, w_kvb_ref, win_ref, wq_ref, wk_ref, wv_ref)

    for r0 in range(0, x_ref.shape[1], ts):
        _pre_rows(r0, ts, x_ref, mod_ref, gmix_ref, win_ref, gqa_ref, wq_ref, gkva_ref, wk_ref, wv_ref,
                  ropeq_ref, ropek_ref, qf_ref, kf_ref, v_ref, sq_ref, sk_ref, sv_ref)


def _pre_rows(r0, ts, x_ref, mod_ref, gmix_ref, win_ref, gqa_ref, wq_ref, gkva_ref, wk_ref, wv_ref,
              ropeq_ref, ropek_ref, qf_ref, kf_ref, v_ref, sq_ref, sk_ref, sv_ref):
    rows = slice(r0, r0 + ts)
    x = x_ref[0, rows]
    mod = mod_ref[0]
    sh1 = mod[0:1]
    sc1 = mod[1:2]
    h = (_rms(x) * gmix_ref[...] * (1.0 + sc1) + sh1).astype(BF16)
    proj = jnp.dot(h, win_ref[...], preferred_element_type=F32)

    for g in range(SWA_GROUP):
        sq_ref[0, g, rows] = (proj[:, C_SQ + g * LANES:C_SQ + (g + 1) * LANES] * SWA_Q_SCALE).astype(BF16)
    swa_k = proj[:, C_SK:C_SV]
    low = lax.broadcasted_iota(jnp.int32, swa_k.shape, 1) < HALF
    sk_ref[0, rows, :LANES] = jnp.where(low, swa_k, 0.0).astype(BF16)
    sk_ref[0, rows, LANES:] = jnp.where(low, 0.0, swa_k).astype(BF16)
    svt = proj[:, C_SV:IN_COLS_PAD].T.astype(BF16)
    for jb in range(ts // WINDOW):
        sv_ref[0, r0 // WINDOW + jb] = svt[:, jb * WINDOW:(jb + 1) * WINDOW]

    qn = (_rms(proj[:, C_QLAT:C_KVLAT]) * gqa_ref[...]).astype(BF16)
    kvn = (_rms(proj[:, C_KVLAT:C_KPE]) * gkva_ref[...]).astype(BF16)
    q = jnp.dot(qn, wq_ref[...], preferred_element_type=F32)
    kn = jnp.dot(kvn, wk_ref[...], preferred_element_type=F32)
    vt = lax.dot_general(wv_ref[...], kvn, (((1,), (1,)), ((), ())), preferred_element_type=F32).astype(BF16)
    tk = v_ref.shape[-1]
    for hp in range(MLA_HEADS // 2):
        for jb in range(ts // tk):
            v_ref[0, hp, r0 // tk + jb] = vt[hp * LANES:(hp + 1) * LANES, jb * tk:(jb + 1) * tk]

    q_keep, q_swap = ropeq_ref[0, rows], ropeq_ref[1, rows]
    kpe = _rope_group(proj[:, C_KPE:C_SQ], ropek_ref[0, rows], ropek_ref[1, rows])
    nope_lanes = lax.broadcasted_iota(jnp.int32, kpe.shape, 1) < MLA_NOPE
    for hd in range(MLA_HEADS):
        sl = slice(hd * LANES, (hd + 1) * LANES)
        qf_ref[0, rows, sl] = _rope_group(q[:, sl], q_keep, q_swap).astype(BF16)
        kn_pair = kn[:, (hd // 2) * LANES:(hd // 2 + 1) * LANES]
        if hd % 2:
            kn_pair = pltpu.roll(kn_pair, HALF, 1)
        kf_ref[0, rows, sl] = jnp.where(nope_lanes, kn_pair, kpe).astype(BF16)


def _rope_tables(scale_q):
    half = MLA_ROPE // 2
    freqs = ROPE_THETA ** (-np.arange(0, MLA_ROPE, 2, dtype=np.float64) / MLA_ROPE)
    ang = np.arange(SEQ, dtype=np.float64)[:, None] * freqs[None, :]
    cos, sin = np.cos(ang), np.sin(ang)
    keep = np.zeros((SEQ, LANES))
    swap = np.zeros((SEQ, LANES))
    keep[:, :MLA_NOPE] = 1.0
    keep[:, MLA_NOPE:MLA_NOPE + half] = cos
    keep[:, MLA_NOPE + half:MLA_NOPE + MLA_ROPE] = cos
    swap[:, MLA_NOPE:MLA_NOPE + half] = -sin
    swap[:, MLA_NOPE + half:MLA_NOPE + MLA_ROPE] = sin
    return (np.stack([keep, swap]) * scale_q).astype(np.float32)


def _pre_call(x, mod, gmix, w_in, gqa, w_qb, gkva, w_kvb, tm, ts, tk):
    b, s, d = x.shape
    pairs = MLA_HEADS // 2
    ropeq = jnp.asarray(_rope_tables(MLA_Q_SCALE))
    ropek = jnp.asarray(_rope_tables(1.0))
    row = lambda bi, i: (bi, i, 0)
    out_shapes = (
        jax.ShapeDtypeStruct((b, s, MLA_QK_COLS), BF16),
        jax.ShapeDtypeStruct((b, s, MLA_QK_COLS), BF16),
        jax.ShapeDtypeStruct((b, pairs, s // tk, LANES, tk), BF16),
        jax.ShapeDtypeStruct((b, SWA_GROUP, s, LANES), BF16),
        jax.ShapeDtypeStruct((b, s, SWA_KV_HEADS * LANES), BF16),
        jax.ShapeDtypeStruct((b, s // WINDOW, SWA_KV_COLS, WINDOW), BF16),
    )
    out_specs = [
        pl.BlockSpec((1, tm, MLA_QK_COLS), row),
        pl.BlockSpec((1, tm, MLA_QK_COLS), row),
        pl.BlockSpec((1, pairs, tm // tk, LANES, tk), lambda bi, i: (bi, 0, i, 0, 0)),
        pl.BlockSpec((1, SWA_GROUP, tm, LANES), lambda bi, i: (bi, 0, i, 0)),
        pl.BlockSpec((1, tm, SWA_KV_HEADS * LANES), row),
        pl.BlockSpec((1, tm // WINDOW, SWA_KV_COLS, WINDOW), lambda bi, i: (bi, i, 0, 0)),
    ]
    return pl.pallas_call(
        functools.partial(_pre_kernel, ts=ts),
        grid=(b, s // tm),
        in_specs=[
            pl.BlockSpec((1, tm, d), row),
            pl.BlockSpec((1, N_MOD, d), lambda bi, i: (bi, 0, 0)),
            _const_spec((1, d)),
            _const_spec(w_in.shape),
            _const_spec((1, Q_LORA)),
            _const_spec(w_qb.shape),
            _const_spec((1, KV_LORA)),
            _const_spec(w_kvb.shape),
            pl.BlockSpec((2, tm, LANES), lambda bi, i: (0, i, 0)),
            pl.BlockSpec((2, tm, LANES), lambda bi, i: (0, i, 0)),
        ],
        out_specs=tuple(out_specs),
        out_shape=out_shapes,
        scratch_shapes=[
            pltpu.VMEM((d, IN_COLS_PAD), BF16),
            pltpu.VMEM((Q_LORA, MLA_QK_COLS), BF16),
            pltpu.VMEM((KV_LORA, MLA_HEADS * MLA_NOPE), BF16),
            pltpu.VMEM((MLA_V_COLS, KV_LORA), BF16),
        ],
        compiler_params=pltpu.CompilerParams(
            dimension_semantics=("arbitrary", "arbitrary"), vmem_limit_bytes=VMEM_LIMIT_BYTES),
        name="pre_attn",
    )(x, mod, gmix, w_in, gqa, w_qb, gkva, w_kvb, ropeq, ropek)


def _mla_kernel(q_ref, k_ref, vt_ref, o_ref, *, tq, qsets):
    if len(qsets) == 1:
        _mla_blocks(q_ref, k_ref, vt_ref, o_ref, qblocks=qsets[0], tq=tq)
        return
    for v, qblocks in enumerate(qsets):
        pl.when(pl.program_id(2) == v)(
            functools.partial(_mla_blocks, q_ref, k_ref, vt_ref, o_ref, qblocks=qblocks, tq=tq))


def _mla_blocks(q_ref, k_ref, vt_ref, o_ref, *, qblocks, tq):
    ki = lax.broadcasted_iota(jnp.int32, (tq, tq), 0)
    qi = lax.broadcasted_iota(jnp.int32, (tq, tq), 1)
    causal = ki <= qi
    heads = range(q_ref.shape[2] // LANES)
    steps = [(i, j) for i in qblocks for j in range(i + 1)]

    def scores(i, j):
        out = []
        for h in heads:
            cols = slice(h * LANES, (h + 1) * LANES)
            q = q_ref[0, i * tq:(i + 1) * tq, cols]
            k = k_ref[0, j * tq:(j + 1) * tq, cols]
            out.append(lax.dot_general(k, q, (((1,), (1,)), ((), ())), preferred_element_type=F32))
        return out

    ones = jnp.ones((BF16_SUBLANES, tq), BF16)
    st_next = scores(*steps[0])
    for n, (i, j) in enumerate(steps):
        st_cur = st_next
        if n + 1 < len(steps):
            st_next = scores(*steps[n + 1])
        if j == 0:
            state = [(jnp.full((1, tq), -jnp.inf, F32), jnp.zeros((MLA_V + BF16_SUBLANES, tq), F32))
                     for _ in heads]
        for h in heads:
            m, acc = state[h]
            st = st_cur[h]
            if j == i:
                st = jnp.where(causal, st, -jnp.inf)
            vt = vt_ref[0, h // 2, j, (h % 2) * MLA_V:(h % 2 + 1) * MLA_V, :]
            vt = jnp.concatenate([vt, ones], axis=0)
            m_new = jnp.maximum(m, jnp.max(st, axis=0, keepdims=True))
            alpha = jnp.exp2(m - m_new)
            pt = jnp.exp2(st - m_new).astype(BF16)
            acc_new = alpha * acc + jnp.dot(vt, pt, preferred_element_type=F32)
            state[h] = (m_new, acc_new)
        if j == i:
            for p in range(len(heads) // 2):
                pair = [state[2 * p][1], state[2 * p + 1][1]]
                out_t = jnp.concatenate([acc[:MLA_V] / acc[MLA_V:MLA_V + 1] for acc in pair], axis=0)
                o_ref[0, i * tq:(i + 1) * tq, p * LANES:(p + 1) * LANES] = out_t.T.astype(BF16)


def _mla_call(qf, kf, vt, tq, hps, qsets):
    b, s, _ = qf.shape
    pairs = MLA_HEADS // 2
    assert vt.shape == (b, pairs, s // tq, LANES, tq) and pairs % hps == 0
    assert sorted(i for qs in qsets for i in qs) == list(range(s // tq))
    seq_cols = lambda bi, g, v: (bi, 0, g)
    return pl.pallas_call(
        functools.partial(_mla_kernel, tq=tq, qsets=qsets),
        grid=(b, pairs // hps, len(qsets)),
        in_specs=[
            pl.BlockSpec((1, s, 2 * hps * LANES), seq_cols),
            pl.BlockSpec((1, s, 2 * hps * LANES), seq_cols),
            pl.BlockSpec((1, hps, s // tq, LANES, tq), lambda bi, g, v: (bi, g, 0, 0, 0)),
        ],
        out_specs=pl.BlockSpec((1, s, hps * LANES), seq_cols),
        out_shape=jax.ShapeDtypeStruct((b, s, MLA_V_COLS), BF16),
        compiler_params=pltpu.CompilerParams(
            dimension_semantics=("parallel", "parallel", "arbitrary"), vmem_limit_bytes=VMEM_LIMIT_BYTES),
        name="mla_attn",
    )(qf, kf, vt)


def _alibi_slope(h):
    return 2.0 ** (-8.0 * (h + 1) / SWA_HEADS)


SWA_STACK = 2
SWA_LOOKAHEAD = 4


def _swa_bias():
    k = np.arange(2 * WINDOW)[:, None]
    q = np.arange(WINDOW)[None, :]
    dist = q + WINDOW - k
    valid = (dist >= 0) & (dist < WINDOW)
    out = np.empty((SWA_KV_HEADS, 2 * WINDOW, SWA_GROUP * WINDOW), np.float32)
    for kv in range(SWA_KV_HEADS):
        for g in range(SWA_GROUP):
            slope = np.float32(_alibi_slope(kv * SWA_GROUP + g))
            out[kv, :, g * WINDOW:(g + 1) * WINDOW] = np.where(valid, -slope * dist.astype(np.float32) * LOG2E, -np.inf)
    return out


def _swa_kernel(sink_ref, bias_ref, q_ref, kp_ref, k_ref, vtp_ref, vt_ref, o_ref):
    i = pl.program_id(1)
    nsub = q_ref.shape[2] // WINDOW
    ones = jnp.ones((BF16_SUBLANES, 2 * WINDOW), BF16)
    width = SWA_STACK * WINDOW
    gone = jnp.where(i == 0, -jnp.inf, 0.0)
    chains = [(c, g0, kv) for c in range(nsub) for g0 in range(0, SWA_GROUP, SWA_STACK)
              for kv in range(SWA_KV_HEADS)]

    def scores(c, g0, kv):
        cols = slice(kv * LANES, (kv + 1) * LANES)
        if c == 0:
            k_win = jnp.concatenate([kp_ref[0, :, cols], k_ref[0, :WINDOW, cols]], axis=0)
        else:
            k_win = k_ref[0, (c - 1) * WINDOW:(c + 1) * WINDOW, cols]
        q_st = q_ref[0, g0:g0 + SWA_STACK, c * WINDOW:(c + 1) * WINDOW, :].reshape(width, LANES)
        st = lax.dot_general(k_win, q_st, (((1,), (1,)), ((), ())), preferred_element_type=F32)
        return st + bias_ref[kv, :, g0 * WINDOW:g0 * WINDOW + width]

    outs = {}
    done = []
    pending = [scores(*ch) for ch in chains[:SWA_LOOKAHEAD]]
    for n, (c, g0, kv) in enumerate(chains):
        st = pending.pop(0)
        if n + SWA_LOOKAHEAD < len(chains):
            pending.append(scores(*chains[n + SWA_LOOKAHEAD]))
        rows = slice(kv * SWA_HEAD_DIM, (kv + 1) * SWA_HEAD_DIM)
        if c == 0:
            vt = jnp.concatenate([vtp_ref[0, 0, rows, :], vt_ref[0, 0, rows, :]], axis=1)
        else:
            vt = jnp.concatenate([vt_ref[0, c - 1, rows, :], vt_ref[0, c, rows, :]], axis=1)
        if c == 0:
            st = jnp.concatenate([st[:WINDOW] + gone, st[WINDOW:]], axis=0)
        sink = sink_ref[kv:kv + 1, g0 * WINDOW:g0 * WINDOW + width]
        m = jnp.maximum(jnp.max(st, axis=0, keepdims=True), sink)
        pt = jnp.exp2(st - m).astype(BF16)
        res = jnp.dot(jnp.concatenate([vt, ones], axis=0), pt, preferred_element_type=F32)
        denom = res[SWA_HEAD_DIM:SWA_HEAD_DIM + 1] + jnp.exp2(sink - m)
        outs[kv] = res[:SWA_HEAD_DIM] / denom
        if kv == SWA_KV_HEADS - 1:
            for gg in range(SWA_STACK):
                gs = slice(gg * WINDOW, (gg + 1) * WINDOW)
                out_t = jnp.concatenate([outs[0][:, gs], outs[1][:, gs]], axis=0)
                done.append((c, g0 + gg, out_t))
    for c, g, out_t in done:
        o_ref[0, c * WINDOW:(c + 1) * WINDOW, g * LANES:(g + 1) * LANES] = out_t.T.astype(BF16)


def _swa_call(sinks, sq, sk, svt, tstep):
    b, _, s, _ = sq.shape
    nsub = tstep // WINDOW
    bias = jnp.asarray(_swa_bias())
    sink_rows = jnp.repeat(sinks.reshape(SWA_KV_HEADS, SWA_GROUP), WINDOW, axis=1) * LOG2E
    prev_blk = lambda i: jnp.maximum(i * nsub - 1, 0)
    return pl.pallas_call(
        _swa_kernel,
        grid=(b, s // tstep),
        in_specs=[
            _const_spec(sink_rows.shape),
            _const_spec(bias.shape),
            pl.BlockSpec((1, SWA_GROUP, tstep, LANES), lambda bi, i: (bi, 0, i, 0)),
            pl.BlockSpec((1, WINDOW, SWA_KV_HEADS * LANES), lambda bi, i: (bi, prev_blk(i), 0)),
            pl.BlockSpec((1, tstep, SWA_KV_HEADS * LANES), lambda bi, i: (bi, i, 0)),
            pl.BlockSpec((1, 1, SWA_KV_COLS, WINDOW), lambda bi, i: (bi, prev_blk(i), 0, 0)),
            pl.BlockSpec((1, nsub, SWA_KV_COLS, WINDOW), lambda bi, i: (bi, i, 0, 0)),
        ],
        out_specs=pl.BlockSpec((1, tstep, SWA_Q_COLS), lambda bi, i: (bi, i, 0)),
        out_shape=jax.ShapeDtypeStruct((b, s, SWA_Q_COLS), BF16),
        compiler_params=pltpu.CompilerParams(
            dimension_semantics=("parallel", "arbitrary"), vmem_limit_bytes=VMEM_LIMIT_BYTES),
        name="swa_attn",
    )(sink_rows, bias, sq, sk, sk, svt, svt)


def _post_kernel(x_ref, mla_ref, swa_ref, mod_ref, wo_ref, gmlp_ref, wup_ref, wdown_ref, gfin_ref,
                 o_ref, *, ts, ff_chunk):
    mod = mod_ref[0]
    g1, sh2, sc2, g2 = mod[2:3], mod[3:4], mod[4:5], mod[5:6]
    woa = wo_ref[:MLA_V_COLS, :].astype(BF16)
    wob = jnp.concatenate(
        [wo_ref[MLA_V_COLS + (kv * SWA_GROUP + g) * SWA_HEAD_DIM:MLA_V_COLS + (kv * SWA_GROUP + g + 1) * SWA_HEAD_DIM, :]
         for g in range(SWA_GROUP) for kv in range(SWA_KV_HEADS)], axis=0).astype(BF16)
    tiles = [slice(r0, r0 + ts) for r0 in range(0, x_ref.shape[1], ts)]
    mid = []
    for rows in tiles:
        attn = (jnp.dot(mla_ref[0, rows], woa, preferred_element_type=F32)
                + jnp.dot(swa_ref[0, rows], wob, preferred_element_type=F32))
        x1 = x_ref[0, rows] + g1 * attn
        h2 = (_rms(x1) * gmlp_ref[...] * (1.0 + sc2) + sh2).astype(BF16)
        mid.append((x1, h2))
    accs = [jnp.zeros_like(x1) for x1, _ in mid]
    for c in range(D_FF // ff_chunk):
        cs = slice(c * ff_chunk, (c + 1) * ff_chunk)
        w_up_c = wup_ref[:, cs].astype(BF16)
        w_down_c = wdown_ref[cs, :].astype(BF16)
        for t, (_, h2) in enumerate(mid):
            u = jnp.maximum(jnp.dot(h2, w_up_c, preferred_element_type=F32), 0.0)
            accs[t] = accs[t] + jnp.dot((u * u).astype(BF16), w_down_c, preferred_element_type=F32)
    for rows, (x1, _), acc in zip(tiles, mid, accs):
        x2 = x1 + g2 * acc
        o_ref[0, rows] = _rms(x2) * gfin_ref[...]


def _post_call(x, mla, swa, mod, w_o, gmlp, wup, wdown, gfin, tm, ts, ff_chunk):
    b, s, d = x.shape
    row = lambda bi, i: (bi, i, 0)
    return pl.pallas_call(
        functools.partial(_post_kernel, ts=ts, ff_chunk=ff_chunk),
        grid=(b, s // tm),
        in_specs=[
            pl.BlockSpec((1, tm, d), row),
            pl.BlockSpec((1, tm, MLA_V_COLS), row),
            pl.BlockSpec((1, tm, SWA_Q_COLS), row),
            pl.BlockSpec((1, N_MOD, d), lambda bi, i: (bi, 0, 0)),
            _const_spec(w_o.shape),
            _const_spec((1, d)),
            _const_spec(wup.shape),
            _const_spec(wdown.shape),
            _const_spec((1, d)),
        ],
        out_specs=pl.BlockSpec((1, tm, d), row),
        out_shape=jax.ShapeDtypeStruct((b, s, d), F32),
        compiler_params=pltpu.CompilerParams(
            dimension_semantics=("parallel", "parallel"), vmem_limit_bytes=VMEM_LIMIT_BYTES),
        name="post_attn",
    )(x, mla, swa, mod, w_o, gmlp, wup, wdown, gfin)


def kernel(x, c, w_ada, b_ada, norm_mix_g, w_in, g_qa, w_qb, g_kva, w_kvb, sinks,
           w_o, norm_mlp_g, w_up, w_down, final_g):
    depth = w_ada.shape[0]
    b = x.shape[0]
    assert depth == 1, "the final rmsnorm is fused into the single layer's post-attention call"
    for l in range(depth):
        mod = _ada_call(c, w_ada[l], b_ada[l]).reshape(b, N_MOD, D_MODEL)
        qf, kf, vt, sq, sk, svt = _pre_call(
            x, mod, norm_mix_g[l][None], w_in[l], g_qa[l][None], w_qb[l].reshape(Q_LORA, -1), g_kva[l][None],
            w_kvb[l].reshape(KV_LORA, -1), tm=1024, ts=256, tk=256)
        mla = _mla_call(qf, kf, vt, tq=256, hps=2, qsets=(tuple(range(SEQ // 256)),))
        swa = _swa_call(sinks[l], sq, sk, svt, tstep=1024)
        x = _post_call(x, mla, swa, mod, w_o[l], norm_mlp_g[l][None],
                       w_up[l], w_down[l], final_g[None], tm=512, ts=256, ff_chunk=1024)
    return x
```

```python
import functools

import numpy as np
import jax
import jax.numpy as jnp
from jax import lax
from jax.experimental import pallas as pl
from jax.experimental.pallas import tpu as pltpu

D_MODEL = 1024
SEQ = 2048
MLA_HEADS = 8
MLA_NOPE = 64
MLA_ROPE = 32
MLA_V = 64
Q_LORA = 384
KV_LORA = 256
ROPE_THETA = 10000.0
SWA_HEADS = 8
SWA_KV_HEADS = 2
SWA_GROUP = SWA_HEADS // SWA_KV_HEADS
SWA_HEAD_DIM = 64
WINDOW = 128
D_FF = 4 * D_MODEL
EPS = 1e-6
N_MOD = 6

LANES = 128
HALF = LANES // 2
BF16_SUBLANES = 16
MLA_QK_COLS = MLA_HEADS * LANES
MLA_V_COLS = MLA_HEADS * MLA_V
SWA_Q_COLS = SWA_HEADS * SWA_HEAD_DIM
SWA_KV_COLS = SWA_KV_HEADS * SWA_HEAD_DIM
C_QLAT = 0
C_KVLAT = C_QLAT + Q_LORA
C_KPE = C_KVLAT + KV_LORA
C_SQ = C_KPE + LANES
C_SK = C_SQ + SWA_Q_COLS
C_SV = C_SK + SWA_KV_COLS
IN_COLS_PAD = C_SV + SWA_KV_COLS

VMEM_LIMIT_BYTES = 56 * 1024 * 1024

LOG2E = float(np.log2(np.e))
MLA_Q_SCALE = (MLA_NOPE + MLA_ROPE) ** -0.5 * LOG2E
SWA_Q_SCALE = SWA_HEAD_DIM ** -0.5 * LOG2E

BF16 = jnp.bfloat16
F32 = jnp.float32


def _const_spec(shape):
    nd = len(shape)
    return pl.BlockSpec(shape, lambda *_: (0,) * nd, pipeline_mode=pl.Buffered(1))


def _rms(x):
    return x * lax.rsqrt(jnp.mean(x * x, axis=-1, keepdims=True) + EPS)


def _ada_kernel(c_ref, w_ref, b_ref, o_ref):
    c = c_ref[...]
    s = c / (1.0 + jnp.exp(-c))
    o_ref[0, :, 0, :] = jnp.dot(s.astype(BF16), w_ref[...].astype(BF16), preferred_element_type=F32) + b_ref[...]


def _ada_call(c, w_ada, b_ada):
    b = c.shape[0]
    n = w_ada.shape[1]
    tn = D_MODEL
    return pl.pallas_call(
        _ada_kernel,
        grid=(n // tn,),
        in_specs=[
            pl.BlockSpec((b, D_MODEL), lambda j: (0, 0)),
            pl.BlockSpec((D_MODEL, tn), lambda j: (0, j)),
            pl.BlockSpec((1, tn), lambda j: (0, j)),
        ],
        out_specs=pl.BlockSpec((1, b, 1, tn), lambda j: (j, 0, 0, 0)),
        out_shape=jax.ShapeDtypeStruct((n // tn, b, 1, tn), F32),
        compiler_params=pltpu.CompilerParams(dimension_semantics=("parallel",)),
        name="ada_mod",
    )(c, w_ada, b_ada.reshape(1, n))


def _mod_spec():
    return pl.BlockSpec((N_MOD, 1, 1, D_MODEL), lambda bi, i: (0, bi, 0, 0))


def _rope_group(xg, keep, swap):
    up = pltpu.roll(xg, LANES - MLA_ROPE // 2, 1)
    return xg * keep + up * swap


def _relayout_weights(w_in_ref, w_qb_ref, w_kvb_ref, win_ref, wq_ref, wk_ref, wv_ref):
    o2 = Q_LORA + KV_LORA
    o3 = o2 + MLA_ROPE
    o4 = o3 + SWA_Q_COLS
    half = MLA_ROPE // 2
    pad = LANES - MLA_NOPE - MLA_ROPE - half
    d_in = w_in_ref.shape[1]
    kpe = [jnp.zeros((MLA_NOPE, d_in), F32), w_in_ref[o2:o3, :], w_in_ref[o2:o2 + half, :],
           jnp.zeros((pad, d_in), F32)]
    sq = [w_in_ref[o3 + (kv * SWA_GROUP + g) * SWA_HEAD_DIM:o3 + (kv * SWA_GROUP + g + 1) * SWA_HEAD_DIM, :]
          for g in range(SWA_GROUP) for kv in range(SWA_KV_HEADS)]
    win_t = jnp.concatenate([w_in_ref[:o2, :]] + kpe + sq + [w_in_ref[o4:, :]], axis=0)
    chunk = 2 * LANES
    for c0 in range(0, IN_COLS_PAD, chunk):
        win_ref[:, c0:c0 + chunk] = win_t[c0:c0 + chunk, :].T.astype(BF16)
    hd_rows = MLA_NOPE + MLA_ROPE
    parts = []
    for hd in range(MLA_HEADS):
        r0 = hd * hd_rows
        parts += [w_qb_ref[r0:r0 + hd_rows, :], w_qb_ref[r0 + MLA_NOPE:r0 + MLA_NOPE + half, :],
                  jnp.zeros((pad, w_qb_ref.shape[1]), F32)]
    wq_ref[...] = jnp.concatenate(parts, axis=0).T.astype(BF16)
    wkvb = w_kvb_ref[...]
    wk_ref[...] = jnp.concatenate([wkvb[:, hd * LANES:hd * LANES + MLA_NOPE] for hd in range(MLA_HEADS)],
                                  axis=1).astype(BF16)
    wv = jnp.concatenate([wkvb[:, hd * LANES + MLA_NOPE:(hd + 1) * LANES] for hd in range(MLA_HEADS)], axis=1)
    wv_ref[...] = wv.T.astype(BF16)


def _pre_kernel(x_ref, mod_ref, gmix_ref, w_in_ref, gqa_ref, w_qb_ref, gkva_ref, w_kvb_ref,
                ropeq_ref, ropek_ref,
                qf_ref, kf_ref, v_ref, sq_ref, sk_ref, sv_ref,
                win_ref, wq_ref, wk_ref, wv_ref, *, ts):
    @pl.when((pl.program_id(0) == 0) & (pl.program_id(1) == 0))
    def _():
        _relayout_weights(w_in_ref, w_qb_ref, w_kvb_ref, win_ref, wq_ref, wk_ref, wv_ref)

    for r0 in range(0, x_ref.shape[1], ts):
        _pre_rows(r0, ts, x_ref, mod_ref, gmix_ref, win_ref, gqa_ref, wq_ref, gkva_ref, wk_ref, wv_ref,
                  ropeq_ref, ropek_ref, qf_ref, kf_ref, v_ref, sq_ref, sk_ref, sv_ref)


def _pre_rows(r0, ts, x_ref, mod_ref, gmix_ref, win_ref, gqa_ref, wq_ref, gkva_ref, wk_ref, wv_ref,
              ropeq_ref, ropek_ref, qf_ref, kf_ref, v_ref, sq_ref, sk_ref, sv_ref):
    rows = slice(r0, r0 + ts)
    x = x_ref[0, rows]
    mod = mod_ref[:, 0, 0, :]
    sh1 = mod[0:1]
    sc1 = mod[1:2]
    h = (_rms(x) * gmix_ref[...] * (1.0 + sc1) + sh1).astype(BF16)
    proj = jnp.dot(h, win_ref[...], preferred_element_type=F32)

    for g in range(SWA_GROUP):
        sq_ref[0, g, rows] = (proj[:, C_SQ + g * LANES:C_SQ + (g + 1) * LANES] * SWA_Q_SCALE).astype(BF16)
    swa_k = proj[:, C_SK:C_SV]
    low = lax.broadcasted_iota(jnp.int32, swa_k.shape, 1) < HALF
    sk_ref[0, rows, :LANES] = jnp.where(low, swa_k, 0.0).astype(BF16)
    sk_ref[0, rows, LANES:] = jnp.where(low, 0.0, swa_k).astype(BF16)
    svt = proj[:, C_SV:IN_COLS_PAD].T.astype(BF16)
    for jb in range(ts // WINDOW):
        sv_ref[0, r0 // WINDOW + jb] = svt[:, jb * WINDOW:(jb + 1) * WINDOW]

    qn = (_rms(proj[:, C_QLAT:C_KVLAT]) * gqa_ref[...]).astype(BF16)
    kvn = (_rms(proj[:, C_KVLAT:C_KPE]) * gkva_ref[...]).astype(BF16)
    q = jnp.dot(qn, wq_ref[...], preferred_element_type=F32)
    kn = jnp.dot(kvn, wk_ref[...], preferred_element_type=F32)
    vt = lax.dot_general(wv_ref[...], kvn, (((1,), (1,)), ((), ())), preferred_element_type=F32).astype(BF16)
    tk = v_ref.shape[-1]
    for hp in range(MLA_HEADS // 2):
        for jb in range(ts // tk):
            v_ref[0, hp, r0 // tk + jb] = vt[hp * LANES:(hp + 1) * LANES, jb * tk:(jb + 1) * tk]

    q_keep, q_swap = ropeq_ref[0, rows], ropeq_ref[1, rows]
    kpe = _rope_group(proj[:, C_KPE:C_SQ], ropek_ref[0, rows], ropek_ref[1, rows])
    nope_lanes = lax.broadcasted_iota(jnp.int32, kpe.shape, 1) < MLA_NOPE
    for hd in range(MLA_HEADS):
        sl = slice(hd * LANES, (hd + 1) * LANES)
        qf_ref[0, rows, sl] = _rope_group(q[:, sl], q_keep, q_swap).astype(BF16)
        kn_pair = kn[:, (hd // 2) * LANES:(hd // 2 + 1) * LANES]
        if hd % 2:
            kn_pair = pltpu.roll(kn_pair, HALF, 1)
        kf_ref[0, rows, sl] = jnp.where(nope_lanes, kn_pair, kpe).astype(BF16)


def _rope_tables(scale_q):
    half = MLA_ROPE // 2
    freqs = ROPE_THETA ** (-np.arange(0, MLA_ROPE, 2, dtype=np.float64) / MLA_ROPE)
    ang = np.arange(SEQ, dtype=np.float64)[:, None] * freqs[None, :]
    cos, sin = np.cos(ang), np.sin(ang)
    keep = np.zeros((SEQ, LANES))
    swap = np.zeros((SEQ, LANES))
    keep[:, :MLA_NOPE] = 1.0
    keep[:, MLA_NOPE:MLA_NOPE + half] = cos
    keep[:, MLA_NOPE + half:MLA_NOPE + MLA_ROPE] = cos
    swap[:, MLA_NOPE:MLA_NOPE + half] = -sin
    swap[:, MLA_NOPE + half:MLA_NOPE + MLA_ROPE] = sin
    return (np.stack([keep, swap]) * scale_q).astype(np.float32)


def _pre_call(x, mod, gmix, w_in, gqa, w_qb, gkva, w_kvb, tm, ts, tk):
    b, s, d = x.shape
    pairs = MLA_HEADS // 2
    ropeq = jnp.asarray(_rope_tables(MLA_Q_SCALE))
    ropek = jnp.asarray(_rope_tables(1.0))
    row = lambda bi, i: (bi, i, 0)
    out_shapes = (
        jax.ShapeDtypeStruct((b, s, MLA_QK_COLS), BF16),
        jax.ShapeDtypeStruct((b, s, MLA_QK_COLS), BF16),
        jax.ShapeDtypeStruct((b, pairs, s // tk, LANES, tk), BF16),
        jax.ShapeDtypeStruct((b, SWA_GROUP, s, LANES), BF16),
        jax.ShapeDtypeStruct((b, s, SWA_KV_HEADS * LANES), BF16),
        jax.ShapeDtypeStruct((b, s // WINDOW, SWA_KV_COLS, WINDOW), BF16),
    )
    out_specs = [
        pl.BlockSpec((1, tm, MLA_QK_COLS), row),
        pl.BlockSpec((1, tm, MLA_QK_COLS), row),
        pl.BlockSpec((1, pairs, tm // tk, LANES, tk), lambda bi, i: (bi, 0, i, 0, 0)),
        pl.BlockSpec((1, SWA_GROUP, tm, LANES), lambda bi, i: (bi, 0, i, 0)),
        pl.BlockSpec((1, tm, SWA_KV_HEADS * LANES), row),
        pl.BlockSpec((1, tm // WINDOW, SWA_KV_COLS, WINDOW), lambda bi, i: (bi, i, 0, 0)),
    ]
    return pl.pallas_call(
        functools.partial(_pre_kernel, ts=ts),
        grid=(b, s // tm),
        in_specs=[
            pl.BlockSpec((1, tm, d), row),
            _mod_spec(),
            _const_spec((1, d)),
            _const_spec(w_in.shape),
            _const_spec((1, Q_LORA)),
            _const_spec(w_qb.shape),
            _const_spec((1, KV_LORA)),
            _const_spec(w_kvb.shape),
            pl.BlockSpec((2, tm, LANES), lambda bi, i: (0, i, 0)),
            pl.BlockSpec((2, tm, LANES), lambda bi, i: (0, i, 0)),
        ],
        out_specs=tuple(out_specs),
        out_shape=out_shapes,
        scratch_shapes=[
            pltpu.VMEM((d, IN_COLS_PAD), BF16),
            pltpu.VMEM((Q_LORA, MLA_QK_COLS), BF16),
            pltpu.VMEM((KV_LORA, MLA_HEADS * MLA_NOPE), BF16),
            pltpu.VMEM((MLA_V_COLS, KV_LORA), BF16),
        ],
        compiler_params=pltpu.CompilerParams(
            dimension_semantics=("arbitrary", "arbitrary"), vmem_limit_bytes=VMEM_LIMIT_BYTES),
        name="pre_attn",
    )(x, mod, gmix, w_in, gqa, w_qb, gkva, w_kvb, ropeq, ropek)


def _mla_kernel(q_ref, k_ref, vt_ref, o_ref, *, tq, qsets):
    if len(qsets) == 1:
        _mla_blocks(q_ref, k_ref, vt_ref, o_ref, qblocks=qsets[0], tq=tq)
        return
    for v, qblocks in enumerate(qsets):
        pl.when(pl.program_id(2) == v)(
            functools.partial(_mla_blocks, q_ref, k_ref, vt_ref, o_ref, qblocks=qblocks, tq=tq))


def _mla_blocks(q_ref, k_ref, vt_ref, o_ref, *, qblocks, tq):
    ki = lax.broadcasted_iota(jnp.int32, (tq, tq), 0)
    qi = lax.broadcasted_iota(jnp.int32, (tq, tq), 1)
    causal = ki <= qi
    heads = range(q_ref.shape[2] // LANES)
    steps = [(i, j) for i in qblocks for j in range(i + 1)]

    def scores(i, j):
        out = []
        for h in heads:
            cols = slice(h * LANES, (h + 1) * LANES)
            q = q_ref[0, i * tq:(i + 1) * tq, cols]
            k = k_ref[0, j * tq:(j + 1) * tq, cols]
            out.append(lax.dot_general(k, q, (((1,), (1,)), ((), ())), preferred_element_type=F32))
        return out

    ones = jnp.ones((BF16_SUBLANES, tq), BF16)
    st_next = scores(*steps[0])
    for n, (i, j) in enumerate(steps):
        st_cur = st_next
        if n + 1 < len(steps):
            st_next = scores(*steps[n + 1])
        if j == 0:
            state = [(jnp.full((1, tq), -jnp.inf, F32), jnp.zeros((MLA_V + BF16_SUBLANES, tq), F32))
                     for _ in heads]
        for h in heads:
            m, acc = state[h]
            st = st_cur[h]
            if j == i:
                st = jnp.where(causal, st, -jnp.inf)
            vt = vt_ref[0, h // 2, j, (h % 2) * MLA_V:(h % 2 + 1) * MLA_V, :]
            vt = jnp.concatenate([vt, ones], axis=0)
            m_new = jnp.maximum(m, jnp.max(st, axis=0, keepdims=True))
            alpha = jnp.exp2(m - m_new)
            pt = jnp.exp2(st - m_new).astype(BF16)
            acc_new = alpha * acc + jnp.dot(vt, pt, preferred_element_type=F32)
            state[h] = (m_new, acc_new)
        if j == i:
            for p in range(len(heads) // 2):
                pair = [state[2 * p][1], state[2 * p + 1][1]]
                out_t = jnp.concatenate([acc[:MLA_V] / acc[MLA_V:MLA_V + 1] for acc in pair], axis=0)
                o_ref[0, i * tq:(i + 1) * tq, p * LANES:(p + 1) * LANES] = out_t.T.astype(BF16)


def _mla_call(qf, kf, vt, tq, hps, qsets):
    b, s, _ = qf.shape
    pairs = MLA_HEADS // 2
    assert vt.shape == (b, pairs, s // tq, LANES, tq) and pairs % hps == 0
    assert sorted(i for qs in qsets for i in qs) == list(range(s // tq))
    seq_cols = lambda bi, g, v: (bi, 0, g)
    return pl.pallas_call(
        functools.partial(_mla_kernel, tq=tq, qsets=qsets),
        grid=(b, pairs // hps, len(qsets)),
        in_specs=[
            pl.BlockSpec((1, s, 2 * hps * LANES), seq_cols),
            pl.BlockSpec((1, s, 2 * hps * LANES), seq_cols),
            pl.BlockSpec((1, hps, s // tq, LANES, tq), lambda bi, g, v: (bi, g, 0, 0, 0)),
        ],
        out_specs=pl.BlockSpec((1, s, hps * LANES), seq_cols),
        out_shape=jax.ShapeDtypeStruct((b, s, MLA_V_COLS), BF16),
        compiler_params=pltpu.CompilerParams(
            dimension_semantics=("parallel", "parallel", "arbitrary"), vmem_limit_bytes=VMEM_LIMIT_BYTES),
        name="mla_attn",
    )(qf, kf, vt)


def _alibi_slope(h):
    return 2.0 ** (-8.0 * (h + 1) / SWA_HEADS)


SWA_STACK = 2
SWA_LOOKAHEAD = 4


def _swa_bias():
    k = np.arange(2 * WINDOW)[:, None]
    q = np.arange(WINDOW)[None, :]
    dist = q + WINDOW - k
    valid = (dist >= 0) & (dist < WINDOW)
    out = np.empty((SWA_KV_HEADS, 2 * WINDOW, SWA_GROUP * WINDOW), np.float32)
    for kv in range(SWA_KV_HEADS):
        for g in range(SWA_GROUP):
            slope = np.float32(_alibi_slope(kv * SWA_GROUP + g))
            out[kv, :, g * WINDOW:(g + 1) * WINDOW] = np.where(valid, -slope * dist.astype(np.float32) * LOG2E, -np.inf)
    return out


def _swa_kernel(sink_ref, bias_ref, q_ref, kp_ref, k_ref, vtp_ref, vt_ref, o_ref):
    i = pl.program_id(1)
    nsub = q_ref.shape[2] // WINDOW
    ones = jnp.ones((BF16_SUBLANES, 2 * WINDOW), BF16)
    width = SWA_STACK * WINDOW
    gone = jnp.where(i == 0, -jnp.inf, 0.0)
    chains = [(c, g0, kv) for c in range(nsub) for g0 in range(0, SWA_GROUP, SWA_STACK)
              for kv in range(SWA_KV_HEADS)]

    def scores(c, g0, kv):
        cols = slice(kv * LANES, (kv + 1) * LANES)
        if c == 0:
            k_win = jnp.concatenate([kp_ref[0, :, cols], k_ref[0, :WINDOW, cols]], axis=0)
        else:
            k_win = k_ref[0, (c - 1) * WINDOW:(c + 1) * WINDOW, cols]
        q_st = q_ref[0, g0:g0 + SWA_STACK, c * WINDOW:(c + 1) * WINDOW, :].reshape(width, LANES)
        st = lax.dot_general(k_win, q_st, (((1,), (1,)), ((), ())), preferred_element_type=F32)
        return st + bias_ref[kv, :, g0 * WINDOW:g0 * WINDOW + width]

    outs = {}
    done = []
    pending = [scores(*ch) for ch in chains[:SWA_LOOKAHEAD]]
    for n, (c, g0, kv) in enumerate(chains):
        st = pending.pop(0)
        if n + SWA_LOOKAHEAD < len(chains):
            pending.append(scores(*chains[n + SWA_LOOKAHEAD]))
        rows = slice(kv * SWA_HEAD_DIM, (kv + 1) * SWA_HEAD_DIM)
        if c == 0:
            vt = jnp.concatenate([vtp_ref[0, 0, rows, :], vt_ref[0, 0, rows, :]], axis=1)
        else:
            vt = jnp.concatenate([vt_ref[0, c - 1, rows, :], vt_ref[0, c, rows, :]], axis=1)
        if c == 0:
            st = jnp.concatenate([st[:WINDOW] + gone, st[WINDOW:]], axis=0)
        sink = sink_ref[kv:kv + 1, g0 * WINDOW:g0 * WINDOW + width]
        m = jnp.maximum(jnp.max(st, axis=0, keepdims=True), sink)
        pt = jnp.exp2(st - m).astype(BF16)
        res = jnp.dot(jnp.concatenate([vt, ones], axis=0), pt, preferred_element_type=F32)
        denom = res[SWA_HEAD_DIM:SWA_HEAD_DIM + 1] + jnp.exp2(sink - m)
        outs[kv] = res[:SWA_HEAD_DIM] / denom
        if kv == SWA_KV_HEADS - 1:
            for gg in range(SWA_STACK):
                gs = slice(gg * WINDOW, (gg + 1) * WINDOW)
                out_t = jnp.concatenate([outs[0][:, gs], outs[1][:, gs]], axis=0)
                done.append((c, g0 + gg, out_t))
    for c, g, out_t in done:
        o_ref[0, c * WINDOW:(c + 1) * WINDOW, g * LANES:(g + 1) * LANES] = out_t.T.astype(BF16)


def _swa_call(sinks, sq, sk, svt, tstep):
    b, _, s, _ = sq.shape
    nsub = tstep // WINDOW
    bias = jnp.asarray(_swa_bias())
    sink_rows = jnp.repeat(sinks.reshape(SWA_KV_HEADS, SWA_GROUP), WINDOW, axis=1) * LOG2E
    prev_blk = lambda i: jnp.maximum(i * nsub - 1, 0)
    return pl.pallas_call(
        _swa_kernel,
        grid=(b, s // tstep),
        in_specs=[
            _const_spec(sink_rows.shape),
            _const_spec(bias.shape),
            pl.BlockSpec((1, SWA_GROUP, tstep, LANES), lambda bi, i: (bi, 0, i, 0)),
            pl.BlockSpec((1, WINDOW, SWA_KV_HEADS * LANES), lambda bi, i: (bi, prev_blk(i), 0)),
            pl.BlockSpec((1, tstep, SWA_KV_HEADS * LANES), lambda bi, i: (bi, i, 0)),
            pl.BlockSpec((1, 1, SWA_KV_COLS, WINDOW), lambda bi, i: (bi, prev_blk(i), 0, 0)),
            pl.BlockSpec((1, nsub, SWA_KV_COLS, WINDOW), lambda bi, i: (bi, i, 0, 0)),
        ],
        out_specs=pl.BlockSpec((1, tstep, SWA_Q_COLS), lambda bi, i: (bi, i, 0)),
        out_shape=jax.ShapeDtypeStruct((b, s, SWA_Q_COLS), BF16),
        compiler_params=pltpu.CompilerParams(
            dimension_semantics=("parallel", "arbitrary"), vmem_limit_bytes=VMEM_LIMIT_BYTES),
        name="swa_attn",
    )(sink_rows, bias, sq, sk, sk, svt, svt)


def _post_kernel(x_ref, mla_ref, swa_ref, mod_ref, wo_ref, gmlp_ref, wup_ref, wdown_ref, gfin_ref,
                 o_ref, *, ts, ff_chunk):
    mod = mod_ref[:, 0, 0, :]
    g1, sh2, sc2, g2 = mod[2:3], mod[3:4], mod[4:5], mod[5:6]
    woa = wo_ref[:MLA_V_COLS, :].astype(BF16)
    wob = jnp.concatenate(
        [wo_ref[MLA_V_COLS + (kv * SWA_GROUP + g) * SWA_HEAD_DIM:MLA_V_COLS + (kv * SWA_GROUP + g + 1) * SWA_HEAD_DIM, :]
         for g in range(SWA_GROUP) for kv in range(SWA_KV_HEADS)], axis=0).astype(BF16)
    tiles = [slice(r0, r0 + ts) for r0 in range(0, x_ref.shape[1], ts)]
    mid = []
    for rows in tiles:
        attn = (jnp.dot(mla_ref[0, rows], woa, preferred_element_type=F32)
                + jnp.dot(swa_ref[0, rows], wob, preferred_element_type=F32))
        x1 = x_ref[0, rows] + g1 * attn
        h2 = (_rms(x1) * gmlp_ref[...] * (1.0 + sc2) + sh2).astype(BF16)
        mid.append((x1, h2))
    accs = [jnp.zeros_like(x1) for x1, _ in mid]
    for c in range(D_FF // ff_chunk):
        cs = slice(c * ff_chunk, (c + 1) * ff_chunk)
        w_up_c = wup_ref[:, cs].astype(BF16)
        w_down_c = wdown_ref[cs, :].astype(BF16)
        for t, (_, h2) in enumerate(mid):
            u = jnp.maximum(jnp.dot(h2, w_up_c, preferred_element_type=F32), 0.0)
            accs[t] = accs[t] + jnp.dot((u * u).astype(BF16), w_down_c, preferred_element_type=F32)
    for rows, (x1, _), acc in zip(tiles, mid, accs):
        x2 = x1 + g2 * acc
        o_ref[0, rows] = _rms(x2) * gfin_ref[...]


def _post_call(x, mla, swa, mod, w_o, gmlp, wup, wdown, gfin, tm, ts, ff_chunk):
    b, s, d = x.shape
    row = lambda bi, i: (bi, i, 0)
    return pl.pallas_call(
        functools.partial(_post_kernel, ts=ts, ff_chunk=ff_chunk),
        grid=(b, s // tm),
        in_specs=[
            pl.BlockSpec((1, tm, d), row),
            pl.BlockSpec((1, tm, MLA_V_COLS), row),
            pl.BlockSpec((1, tm, SWA_Q_COLS), row),
            _mod_spec(),
            _const_spec(w_o.shape),
            _const_spec((1, d)),
            _const_spec(wup.shape),
            _const_spec(wdown.shape),
            _const_spec((1, d)),
        ],
        out_specs=pl.BlockSpec((1, tm, d), row),
        out_shape=jax.ShapeDtypeStruct((b, s, d), F32),
        compiler_params=pltpu.CompilerParams(
            dimension_semantics=("parallel", "parallel"), vmem_limit_bytes=VMEM_LIMIT_BYTES),
        name="post_attn",
    )(x, mla, swa, mod, w_o, gmlp, wup, wdown, gfin)


def kernel(x, c, w_ada, b_ada, norm_mix_g, w_in, g_qa, w_qb, g_kva, w_kvb, sinks,
           w_o, norm_mlp_g, w_up, w_down, final_g):
    depth = w_ada.shape[0]
    b = x.shape[0]
    assert depth == 1, "the final rmsnorm is fused into the single layer's post-attention call"
    for l in range(depth):
        mod = _ada_call(c, w_ada[l], b_ada[l])
        qf, kf, vt, sq, sk, svt = _pre_call(
            x, mod, norm_mix_g[l][None], w_in[l].T, g_qa[l][None],
            w_qb[l].transpose(1, 2, 0).reshape(-1, Q_LORA), g_kva[l][None], w_kvb[l].reshape(KV_LORA, -1),
            tm=1024, ts=256, tk=256)
        mla = _mla_call(qf, kf, vt, tq=256, hps=2, qsets=(tuple(range(SEQ // 256)),))
        swa = _swa_call(sinks[l], sq, sk, svt, tstep=1024)
        x = _post_call(x, mla, swa, mod, w_o[l], norm_mlp_g[l][None],
                       w_up[l], w_down[l], final_g[None], tm=512, ts=256, ff_chunk=1024)
    return x
```

```python
import functools

import numpy as np
import jax
import jax.numpy as jnp
from jax import lax
from jax.experimental import pallas as pl
from jax.experimental.pallas import tpu as pltpu

D_MODEL = 1024
SEQ = 2048
MLA_HEADS = 8
MLA_NOPE = 64
MLA_ROPE = 32
MLA_V = 64
Q_LORA = 384
KV_LORA = 256
ROPE_THETA = 10000.0
SWA_HEADS = 8
SWA_KV_HEADS = 2
SWA_GROUP = SWA_HEADS // SWA_KV_HEADS
SWA_HEAD_DIM = 64
WINDOW = 128
D_FF = 4 * D_MODEL
EPS = 1e-6
N_MOD = 6

LANES = 128
HALF = LANES // 2
BF16_SUBLANES = 16
MLA_QK_COLS = MLA_HEADS * LANES
MLA_V_COLS = MLA_HEADS * MLA_V
SWA_Q_COLS = SWA_HEADS * SWA_HEAD_DIM
SWA_KV_COLS = SWA_KV_HEADS * SWA_HEAD_DIM
C_QLAT = 0
C_KVLAT = C_QLAT + Q_LORA
C_KPE = C_KVLAT + KV_LORA
C_SQ = C_KPE + LANES
C_SK = C_SQ + SWA_Q_COLS
C_SV = C_SK + SWA_KV_COLS
IN_COLS_PAD = C_SV + SWA_KV_COLS

VMEM_LIMIT_BYTES = 62 * 1024 * 1024

LOG2E = float(np.log2(np.e))
MLA_Q_SCALE = (MLA_NOPE + MLA_ROPE) ** -0.5 * LOG2E
SWA_Q_SCALE = SWA_HEAD_DIM ** -0.5 * LOG2E

BF16 = jnp.bfloat16
F32 = jnp.float32


def _const_spec(shape):
    nd = len(shape)
    return pl.BlockSpec(shape, lambda *_: (0,) * nd, pipeline_mode=pl.Buffered(1))


def _rms(x):
    return x * lax.rsqrt(jnp.mean(x * x, axis=-1, keepdims=True) + EPS)


def _ada_kernel(c_ref, w_ref, b_ref, o_ref):
    c = c_ref[...]
    s = c / (1.0 + jnp.exp(-c))
    o_ref[0, :, 0, :] = jnp.dot(s.astype(BF16), w_ref[...].astype(BF16), preferred_element_type=F32) + b_ref[...]


def _ada_call(c, w_ada, b_ada):
    b = c.shape[0]
    n = w_ada.shape[1]
    tn = D_MODEL
    return pl.pallas_call(
        _ada_kernel,
        grid=(n // tn,),
        in_specs=[
            pl.BlockSpec((b, D_MODEL), lambda j: (0, 0)),
            pl.BlockSpec((D_MODEL, tn), lambda j: (0, j)),
            pl.BlockSpec((1, tn), lambda j: (0, j)),
        ],
        out_specs=pl.BlockSpec((1, b, 1, tn), lambda j: (j, 0, 0, 0)),
        out_shape=jax.ShapeDtypeStruct((n // tn, b, 1, tn), F32),
        compiler_params=pltpu.CompilerParams(dimension_semantics=("parallel",)),
        name="ada_mod",
    )(c, w_ada, b_ada.reshape(1, n))


def _mod_spec():
    return pl.BlockSpec((N_MOD, 1, 1, D_MODEL), lambda bi, i: (0, bi, 0, 0))


def _rope_group(xg, keep, swap):
    up = pltpu.roll(xg, LANES - MLA_ROPE // 2, 1)
    return xg * keep + up * swap


def _relayout_weights(w_in_ref, w_qb_ref, w_kvb_ref, win_ref, wq_ref, wk_ref, wv_ref):
    o2 = Q_LORA + KV_LORA
    o3 = o2 + MLA_ROPE
    o4 = o3 + SWA_Q_COLS
    half = MLA_ROPE // 2
    pad = LANES - MLA_NOPE - MLA_ROPE - half
    d_in = w_in_ref.shape[1]
    kpe = [jnp.zeros((MLA_NOPE, d_in), F32), w_in_ref[o2:o3, :], w_in_ref[o2:o2 + half, :],
           jnp.zeros((pad, d_in), F32)]
    sq = [w_in_ref[o3 + (kv * SWA_GROUP + g) * SWA_HEAD_DIM:o3 + (kv * SWA_GROUP + g + 1) * SWA_HEAD_DIM, :]
          for g in range(SWA_GROUP) for kv in range(SWA_KV_HEADS)]
    win_t = jnp.concatenate([w_in_ref[:o2, :]] + kpe + sq + [w_in_ref[o4:, :]], axis=0)
    chunk = 2 * LANES
    for c0 in range(0, IN_COLS_PAD, chunk):
        win_ref[:, c0:c0 + chunk] = win_t[c0:c0 + chunk, :].T.astype(BF16)
    hd_rows = MLA_NOPE + MLA_ROPE
    parts = []
    for hd in range(MLA_HEADS):
        r0 = hd * hd_rows
        parts += [w_qb_ref[r0:r0 + hd_rows, :], w_qb_ref[r0 + MLA_NOPE:r0 + MLA_NOPE + half, :],
                  jnp.zeros((pad, w_qb_ref.shape[1]), F32)]
    wq_ref[...] = jnp.concatenate(parts, axis=0).T.astype(BF16)
    wkvb = w_kvb_ref[...]
    wk_ref[...] = jnp.concatenate([wkvb[:, hd * LANES:hd * LANES + MLA_NOPE] for hd in range(MLA_HEADS)],
                                  axis=1).astype(BF16)
    wv = jnp.concatenate([wkvb[:, hd * LANES + MLA_NOPE:(hd + 1) * LANES] for hd in range(MLA_HEADS)], axis=1)
    wv_ref[...] = wv.T.astype(BF16)


def _pre_kernel(x_ref, mod_ref, gmix_ref, w_in_ref, gqa_ref, w_qb_ref, gkva_ref, w_kvb_ref,
                ropeq_ref, ropek_ref,
                qf_ref, kf_ref, v_ref, sq_ref, sk_ref, sv_ref,
                win_ref, wq_ref, wk_ref, wv_ref, *, ts):
    @pl.when((pl.program_id(0) == 0) & (pl.program_id(1) == 0))
    def _():
        _relayout_weights(w_in_ref, w_qb_ref, w_kvb_ref, win_ref, wq_ref, wk_ref, wv_ref)

    for r0 in range(0, x_ref.shape[1], ts):
        _pre_rows(r0, ts, x_ref, mod_ref, gmix_ref, win_ref, gqa_ref, wq_ref, gkva_ref, wk_ref, wv_ref,
                  ropeq_ref, ropek_ref, qf_ref, kf_ref, v_ref, sq_ref, sk_ref, sv_ref)


def _pre_rows(r0, ts, x_ref, mod_ref, gmix_ref, win_ref, gqa_ref, wq_ref, gkva_ref, wk_ref, wv_ref,
              ropeq_ref, ropek_ref, qf_ref, kf_ref, v_ref, sq_ref, sk_ref, sv_ref):
    rows = slice(r0, r0 + ts)
    x = x_ref[0, rows]
    mod = mod_ref[:, 0, 0, :]
    sh1 = mod[0:1]
    sc1 = mod[1:2]
    h = (_rms(x) * gmix_ref[...] * (1.0 + sc1) + sh1).astype(BF16)
    proj = jnp.dot(h, win_ref[...], preferred_element_type=F32)

    for g in range(SWA_GROUP):
        sq_ref[0, g, rows] = (proj[:, C_SQ + g * LANES:C_SQ + (g + 1) * LANES] * SWA_Q_SCALE).astype(BF16)
    swa_k = proj[:, C_SK:C_SV]
    low = lax.broadcasted_iota(jnp.int32, swa_k.shape, 1) < HALF
    sk_ref[0, rows, :LANES] = jnp.where(low, swa_k, 0.0).astype(BF16)
    sk_ref[0, rows, LANES:] = jnp.where(low, 0.0, swa_k).astype(BF16)
    svt = proj[:, C_SV:IN_COLS_PAD].T.astype(BF16)
    for jb in range(ts // WINDOW):
        sv_ref[0, r0 // WINDOW + jb] = svt[:, jb * WINDOW:(jb + 1) * WINDOW]

    qn = (_rms(proj[:, C_QLAT:C_KVLAT]) * gqa_ref[...]).astype(BF16)
    kvn = (_rms(proj[:, C_KVLAT:C_KPE]) * gkva_ref[...]).astype(BF16)
    q = jnp.dot(qn, wq_ref[...], preferred_element_type=F32)
    kn = jnp.dot(kvn, wk_ref[...], preferred_element_type=F32)
    vt = lax.dot_general(wv_ref[...], kvn, (((1,), (1,)), ((), ())), preferred_element_type=F32).astype(BF16)
    tk = v_ref.shape[-1]
    for hp in range(MLA_HEADS // 2):
        for jb in range(ts // tk):
            v_ref[0, hp, r0 // tk + jb] = vt[hp * LANES:(hp + 1) * LANES, jb * tk:(jb + 1) * tk]

    q_keep, q_swap = ropeq_ref[0, rows], ropeq_ref[1, rows]
    kpe = _rope_group(proj[:, C_KPE:C_SQ], ropek_ref[0, rows], ropek_ref[1, rows])
    nope_lanes = lax.broadcasted_iota(jnp.int32, kpe.shape, 1) < MLA_NOPE
    for hd in range(MLA_HEADS):
        sl = slice(hd * LANES, (hd + 1) * LANES)
        qf_ref[0, rows, sl] = _rope_group(q[:, sl], q_keep, q_swap).astype(BF16)
        kn_pair = kn[:, (hd // 2) * LANES:(hd // 2 + 1) * LANES]
        if hd % 2:
            kn_pair = pltpu.roll(kn_pair, HALF, 1)
        kf_ref[0, rows, sl] = jnp.where(nope_lanes, kn_pair, kpe).astype(BF16)


def _rope_tables(scale_q):
    half = MLA_ROPE // 2
    freqs = ROPE_THETA ** (-np.arange(0, MLA_ROPE, 2, dtype=np.float64) / MLA_ROPE)
    ang = np.arange(SEQ, dtype=np.float64)[:, None] * freqs[None, :]
    cos, sin = np.cos(ang), np.sin(ang)
    keep = np.zeros((SEQ, LANES))
    swap = np.zeros((SEQ, LANES))
    keep[:, :MLA_NOPE] = 1.0
    keep[:, MLA_NOPE:MLA_NOPE + half] = cos
    keep[:, MLA_NOPE + half:MLA_NOPE + MLA_ROPE] = cos
    swap[:, MLA_NOPE:MLA_NOPE + half] = -sin
    swap[:, MLA_NOPE + half:MLA_NOPE + MLA_ROPE] = sin
    return (np.stack([keep, swap]) * scale_q).astype(np.float32)


def _pre_call(x, mod, gmix, w_in, gqa, w_qb, gkva, w_kvb, tm, ts, tk):
    b, s, d = x.shape
    pairs = MLA_HEADS // 2
    ropeq = jnp.asarray(_rope_tables(MLA_Q_SCALE))
    ropek = jnp.asarray(_rope_tables(1.0))
    row = lambda bi, i: (bi, i, 0)
    out_shapes = (
        jax.ShapeDtypeStruct((b, s, MLA_QK_COLS), BF16),
        jax.ShapeDtypeStruct((b, s, MLA_QK_COLS), BF16),
        jax.ShapeDtypeStruct((b, pairs, s // tk, LANES, tk), BF16),
        jax.ShapeDtypeStruct((b, SWA_GROUP, s, LANES), BF16),
        jax.ShapeDtypeStruct((b, s, SWA_KV_HEADS * LANES), BF16),
        jax.ShapeDtypeStruct((b, s // WINDOW, SWA_KV_COLS, WINDOW), BF16),
    )
    out_specs = [
        pl.BlockSpec((1, tm, MLA_QK_COLS), row),
        pl.BlockSpec((1, tm, MLA_QK_COLS), row),
        pl.BlockSpec((1, pairs, tm // tk, LANES, tk), lambda bi, i: (bi, 0, i, 0, 0)),
        pl.BlockSpec((1, SWA_GROUP, tm, LANES), lambda bi, i: (bi, 0, i, 0)),
        pl.BlockSpec((1, tm, SWA_KV_HEADS * LANES), row),
        pl.BlockSpec((1, tm // WINDOW, SWA_KV_COLS, WINDOW), lambda bi, i: (bi, i, 0, 0)),
    ]
    return pl.pallas_call(
        functools.partial(_pre_kernel, ts=ts),
        grid=(b, s // tm),
        in_specs=[
            pl.BlockSpec((1, tm, d), row),
            _mod_spec(),
            _const_spec((1, d)),
            _const_spec(w_in.shape),
            _const_spec((1, Q_LORA)),
            _const_spec(w_qb.shape),
            _const_spec((1, KV_LORA)),
            _const_spec(w_kvb.shape),
            pl.BlockSpec((2, tm, LANES), lambda bi, i: (0, i, 0)),
            pl.BlockSpec((2, tm, LANES), lambda bi, i: (0, i, 0)),
        ],
        out_specs=tuple(out_specs),
        out_shape=out_shapes,
        scratch_shapes=[
            pltpu.VMEM((d, IN_COLS_PAD), BF16),
            pltpu.VMEM((Q_LORA, MLA_QK_COLS), BF16),
            pltpu.VMEM((KV_LORA, MLA_HEADS * MLA_NOPE), BF16),
            pltpu.VMEM((MLA_V_COLS, KV_LORA), BF16),
        ],
        compiler_params=pltpu.CompilerParams(
            dimension_semantics=("arbitrary", "arbitrary"), vmem_limit_bytes=VMEM_LIMIT_BYTES),
        name="pre_attn",
    )(x, mod, gmix, w_in, gqa, w_qb, gkva, w_kvb, ropeq, ropek)


def _mla_kernel(q_ref, k_ref, vt_ref, o_ref, *, tq, qsets):
    if len(qsets) == 1:
        _mla_blocks(q_ref, k_ref, vt_ref, o_ref, qblocks=qsets[0], tq=tq)
        return
    for v, qblocks in enumerate(qsets):
        pl.when(pl.program_id(2) == v)(
            functools.partial(_mla_blocks, q_ref, k_ref, vt_ref, o_ref, qblocks=qblocks, tq=tq))


def _mla_blocks(q_ref, k_ref, vt_ref, o_ref, *, qblocks, tq):
    ki = lax.broadcasted_iota(jnp.int32, (tq, tq), 0)
    qi = lax.broadcasted_iota(jnp.int32, (tq, tq), 1)
    causal = ki <= qi
    heads = range(q_ref.shape[2] // LANES)
    steps = [(i, j) for i in qblocks for j in range(i + 1)]

    def scores(i, j):
        out = []
        for h in heads:
            cols = slice(h * LANES, (h + 1) * LANES)
            q = q_ref[0, i * tq:(i + 1) * tq, cols]
            k = k_ref[0, j * tq:(j + 1) * tq, cols]
            out.append(lax.dot_general(k, q, (((1,), (1,)), ((), ())), preferred_element_type=F32))
        return out

    ones = jnp.ones((BF16_SUBLANES, tq), BF16)
    st_next = scores(*steps[0])
    for n, (i, j) in enumerate(steps):
        st_cur = st_next
        if n + 1 < len(steps):
            st_next = scores(*steps[n + 1])
        if j == 0:
            state = [(jnp.full((1, tq), -jnp.inf, F32), jnp.zeros((MLA_V + BF16_SUBLANES, tq), F32))
                     for _ in heads]
        for h in heads:
            m, acc = state[h]
            st = st_cur[h]
            if j == i:
                st = jnp.where(causal, st, -jnp.inf)
            vt = vt_ref[0, h // 2, j, (h % 2) * MLA_V:(h % 2 + 1) * MLA_V, :]
            vt = jnp.concatenate([vt, ones], axis=0)
            m_new = jnp.maximum(m, jnp.max(st, axis=0, keepdims=True))
            alpha = jnp.exp2(m - m_new)
            pt = jnp.exp2(st - m_new).astype(BF16)
            acc_new = alpha * acc + jnp.dot(vt, pt, preferred_element_type=F32)
            state[h] = (m_new, acc_new)
        if j == i:
            for p in range(len(heads) // 2):
                pair = [state[2 * p][1], state[2 * p + 1][1]]
                out_t = jnp.concatenate([acc[:MLA_V] / acc[MLA_V:MLA_V + 1] for acc in pair], axis=0)
                o_ref[0, i * tq:(i + 1) * tq, p * LANES:(p + 1) * LANES] = out_t.T.astype(BF16)


def _mla_call(qf, kf, vt, tq, hps, qsets):
    b, s, _ = qf.shape
    pairs = MLA_HEADS // 2
    assert vt.shape == (b, pairs, s // tq, LANES, tq) and pairs % hps == 0
    assert sorted(i for qs in qsets for i in qs) == list(range(s // tq))
    seq_cols = lambda bi, g, v: (bi, 0, g)
    return pl.pallas_call(
        functools.partial(_mla_kernel, tq=tq, qsets=qsets),
        grid=(b, pairs // hps, len(qsets)),
        in_specs=[
            pl.BlockSpec((1, s, 2 * hps * LANES), seq_cols),
            pl.BlockSpec((1, s, 2 * hps * LANES), seq_cols),
            pl.BlockSpec((1, hps, s // tq, LANES, tq), lambda bi, g, v: (bi, g, 0, 0, 0)),
        ],
        out_specs=pl.BlockSpec((1, s, hps * LANES), seq_cols),
        out_shape=jax.ShapeDtypeStruct((b, s, MLA_V_COLS), BF16),
        compiler_params=pltpu.CompilerParams(
            dimension_semantics=("parallel", "parallel", "arbitrary"), vmem_limit_bytes=VMEM_LIMIT_BYTES),
        name="mla_attn",
    )(qf, kf, vt)


def _alibi_slope(h):
    return 2.0 ** (-8.0 * (h + 1) / SWA_HEADS)


SWA_STACK = 2
SWA_LOOKAHEAD = 4
SWA_CHAINS_PER_MLP_SLOT = 2


def _swa_bias():
    k = np.arange(2 * WINDOW)[:, None]
    q = np.arange(WINDOW)[None, :]
    dist = q + WINDOW - k
    valid = (dist >= 0) & (dist < WINDOW)
    out = np.empty((SWA_KV_HEADS, 2 * WINDOW, SWA_GROUP * WINDOW), np.float32)
    for kv in range(SWA_KV_HEADS):
        for g in range(SWA_GROUP):
            slope = np.float32(_alibi_slope(kv * SWA_GROUP + g))
            out[kv, :, g * WINDOW:(g + 1) * WINDOW] = np.where(valid, -slope * dist.astype(np.float32) * LOG2E, -np.inf)
    return out


def _swa_tile_chains(sink_ref, bias_ref, q_ref, k_prev, k_ref, vt_prev, vt_ref, gone, store):
    nsub = q_ref.shape[2] // WINDOW
    ones = jnp.ones((BF16_SUBLANES, 2 * WINDOW), BF16)
    width = SWA_STACK * WINDOW
    chains = [(c, g0, kv) for c in range(nsub) for g0 in range(0, SWA_GROUP, SWA_STACK)
              for kv in range(SWA_KV_HEADS)]

    def scores(c, g0, kv):
        cols = slice(kv * LANES, (kv + 1) * LANES)
        if c == 0:
            k_win = jnp.concatenate([k_prev(cols), k_ref[0, :WINDOW, cols]], axis=0)
        else:
            k_win = k_ref[0, (c - 1) * WINDOW:(c + 1) * WINDOW, cols]
        q_st = q_ref[0, g0:g0 + SWA_STACK, c * WINDOW:(c + 1) * WINDOW, :].reshape(width, LANES)
        st = lax.dot_general(k_win, q_st, (((1,), (1,)), ((), ())), preferred_element_type=F32)
        return st + bias_ref[kv, :, g0 * WINDOW:g0 * WINDOW + width]

    outs = {}
    pending = [scores(*ch) for ch in chains[:SWA_LOOKAHEAD]]
    for n, (c, g0, kv) in enumerate(chains):
        st = pending.pop(0)
        if n + SWA_LOOKAHEAD < len(chains):
            pending.append(scores(*chains[n + SWA_LOOKAHEAD]))
        rows = slice(kv * SWA_HEAD_DIM, (kv + 1) * SWA_HEAD_DIM)
        if c == 0:
            vt = jnp.concatenate([vt_prev(rows), vt_ref[0, 0, rows, :]], axis=1)
            st = jnp.concatenate([st[:WINDOW] + gone, st[WINDOW:]], axis=0)
        else:
            vt = jnp.concatenate([vt_ref[0, c - 1, rows, :], vt_ref[0, c, rows, :]], axis=1)
        sink = sink_ref[kv:kv + 1, g0 * WINDOW:g0 * WINDOW + width]
        m = jnp.maximum(jnp.max(st, axis=0, keepdims=True), sink)
        pt = jnp.exp2(st - m).astype(BF16)
        res = jnp.dot(jnp.concatenate([vt, ones], axis=0), pt, preferred_element_type=F32)
        denom = res[SWA_HEAD_DIM:SWA_HEAD_DIM + 1] + jnp.exp2(sink - m)
        outs[kv] = res[:SWA_HEAD_DIM] / denom
        if kv == SWA_KV_HEADS - 1:
            for gg in range(SWA_STACK):
                gs = slice(gg * WINDOW, (gg + 1) * WINDOW)
                out_t = jnp.concatenate([outs[0][:, gs], outs[1][:, gs]], axis=0)
                store(c, g0 + gg, out_t.T.astype(BF16))
        yield


def _post_kernel(x_ref, mla_ref, mod_ref, wo_ref, gmlp_ref, wup_ref, wdown_ref, gfin_ref,
                 sink_ref, bias_ref, qn_ref, kpn_ref, kn_ref, vtpn_ref, vtn_ref, q0_ref, k0_ref, vt0_ref,
                 o_ref, swa_ref, *, ts, ff_chunk):
    step = pl.program_id(0) * pl.num_programs(1) + pl.program_id(1)
    n_steps = pl.num_programs(0) * pl.num_programs(1)

    def store(c, g, tile):
        swa_ref[c * WINDOW:(c + 1) * WINDOW, g * LANES:(g + 1) * LANES] = tile

    @pl.when(step == 0)
    def _():
        for _ in _swa_tile_chains(sink_ref, bias_ref, q0_ref, lambda cols: k0_ref[0, :WINDOW, cols], k0_ref,
                                  lambda rows: vt0_ref[0, 0, rows, :], vt0_ref, -jnp.inf, store):
            pass

    mod = mod_ref[:, 0, 0, :]
    g1, sh2, sc2, g2 = mod[2:3], mod[3:4], mod[4:5], mod[5:6]
    woa = wo_ref[:MLA_V_COLS, :].astype(BF16)
    wob = jnp.concatenate(
        [wo_ref[MLA_V_COLS + (kv * SWA_GROUP + g) * SWA_HEAD_DIM:MLA_V_COLS + (kv * SWA_GROUP + g + 1) * SWA_HEAD_DIM, :]
         for g in range(SWA_GROUP) for kv in range(SWA_KV_HEADS)], axis=0).astype(BF16)
    tiles = [slice(r0, r0 + ts) for r0 in range(0, x_ref.shape[1], ts)]
    mid = []
    for rows in tiles:
        attn = (jnp.dot(mla_ref[0, rows], woa, preferred_element_type=F32)
                + jnp.dot(swa_ref[rows, :], wob, preferred_element_type=F32))
        x1 = x_ref[0, rows] + g1 * attn
        h2 = (_rms(x1) * gmlp_ref[...] * (1.0 + sc2) + sh2).astype(BF16)
        mid.append((x1, h2))
    nxt = jnp.minimum(step + 1, n_steps - 1) % pl.num_programs(1)
    gone = jnp.where(nxt == 0, -jnp.inf, 0.0)
    chains = _swa_tile_chains(sink_ref, bias_ref, qn_ref, lambda cols: kpn_ref[0, :, cols], kn_ref,
                              lambda rows: vtpn_ref[0, 0, rows, :], vtn_ref, gone, store)
    accs = [jnp.zeros_like(x1) for x1, _ in mid]
    for c in range(D_FF // ff_chunk):
        cs = slice(c * ff_chunk, (c + 1) * ff_chunk)
        w_up_c = wup_ref[:, cs].astype(BF16)
        w_down_c = wdown_ref[cs, :].astype(BF16)
        for t, (_, h2) in enumerate(mid):
            u = jnp.maximum(jnp.dot(h2, w_up_c, preferred_element_type=F32), 0.0)
            next(chains, None)
            accs[t] = accs[t] + jnp.dot((u * u).astype(BF16), w_down_c, preferred_element_type=F32)
            next(chains, None)
    for _ in chains:
        pass
    for rows, (x1, _), acc in zip(tiles, mid, accs):
        x2 = x1 + g2 * acc
        o_ref[0, rows] = _rms(x2) * gfin_ref[...]


def _post_call(x, mla, mod, w_o, gmlp, wup, wdown, gfin, sinks, sq, sk, svt, tm, ts, ff_chunk):
    b, s, d = x.shape
    nt = s // tm
    nsub = tm // WINDOW
    bias = jnp.asarray(_swa_bias())
    sink_rows = jnp.repeat(sinks.reshape(SWA_KV_HEADS, SWA_GROUP), WINDOW, axis=1) * LOG2E
    row = lambda bi, i: (bi, i, 0)

    def nxt(bi, i):
        n = jnp.minimum(bi * nt + i + 1, b * nt - 1)
        return n // nt, n % nt

    def nxt_prev_blk(bi, i):
        bn, tn = nxt(bi, i)
        return bn, jnp.maximum(tn * nsub - 1, 0)

    return pl.pallas_call(
        functools.partial(_post_kernel, ts=ts, ff_chunk=ff_chunk),
        grid=(b, nt),
        in_specs=[
            pl.BlockSpec((1, tm, d), row),
            pl.BlockSpec((1, tm, MLA_V_COLS), row),
            _mod_spec(),
            _const_spec(w_o.shape),
            _const_spec((1, d)),
            _const_spec(wup.shape),
            _const_spec(wdown.shape),
            _const_spec((1, d)),
            _const_spec(sink_rows.shape),
            _const_spec(bias.shape),
            pl.BlockSpec((1, SWA_GROUP, tm, LANES), lambda bi, i: (nxt(bi, i)[0], 0, nxt(bi, i)[1], 0)),
            pl.BlockSpec((1, WINDOW, SWA_KV_HEADS * LANES), lambda bi, i: nxt_prev_blk(bi, i) + (0,)),
            pl.BlockSpec((1, tm, SWA_KV_HEADS * LANES), lambda bi, i: nxt(bi, i) + (0,)),
            pl.BlockSpec((1, 1, SWA_KV_COLS, WINDOW), lambda bi, i: nxt_prev_blk(bi, i) + (0, 0)),
            pl.BlockSpec((1, nsub, SWA_KV_COLS, WINDOW), lambda bi, i: nxt(bi, i) + (0, 0)),
            pl.BlockSpec((1, SWA_GROUP, tm, LANES), lambda bi, i: (0, 0, 0, 0), pipeline_mode=pl.Buffered(1)),
            pl.BlockSpec((1, tm, SWA_KV_HEADS * LANES), lambda bi, i: (0, 0, 0), pipeline_mode=pl.Buffered(1)),
            pl.BlockSpec((1, nsub, SWA_KV_COLS, WINDOW), lambda bi, i: (0, 0, 0, 0), pipeline_mode=pl.Buffered(1)),
        ],
        out_specs=pl.BlockSpec((1, tm, d), row),
        out_shape=jax.ShapeDtypeStruct((b, s, d), F32),
        scratch_shapes=[pltpu.VMEM((tm, SWA_Q_COLS), BF16)],
        compiler_params=pltpu.CompilerParams(
            dimension_semantics=("arbitrary", "arbitrary"), vmem_limit_bytes=VMEM_LIMIT_BYTES),
        name="post_attn",
    )(x, mla, mod, w_o, gmlp, wup, wdown, gfin, sink_rows, bias, sq, sk, sk, svt, svt, sq, sk, svt)


def kernel(x, c, w_ada, b_ada, norm_mix_g, w_in, g_qa, w_qb, g_kva, w_kvb, sinks,
           w_o, norm_mlp_g, w_up, w_down, final_g):
    depth = w_ada.shape[0]
    b = x.shape[0]
    assert depth == 1, "the final rmsnorm is fused into the single layer's post-attention call"
    for l in range(depth):
        mod = _ada_call(c, w_ada[l], b_ada[l])
        qf, kf, vt, sq, sk, svt = _pre_call(
            x, mod, norm_mix_g[l][None], w_in[l].T, g_qa[l][None],
            w_qb[l].transpose(1, 2, 0).reshape(-1, Q_LORA), g_kva[l][None], w_kvb[l].reshape(KV_LORA, -1),
            tm=1024, ts=256, tk=256)
        mla = _mla_call(qf, kf, vt, tq=256, hps=2, qsets=(tuple(range(SEQ // 256)),))
        x = _post_call(x, mla, mod, w_o[l], norm_mlp_g[l][None], w_up[l], w_down[l], final_g[None],
                       sinks[l], sq, sk, svt, tm=512, ts=256, ff_chunk=1024)
    return x
```

```python
import functools

import numpy as np
import jax
import jax.numpy as jnp
from jax import lax
from jax.experimental import pallas as pl
from jax.experimental.pallas import tpu as pltpu

D_MODEL = 1024
SEQ = 2048
MLA_HEADS = 8
MLA_NOPE = 64
MLA_ROPE = 32
MLA_V = 64
Q_LORA = 384
KV_LORA = 256
ROPE_THETA = 10000.0
SWA_HEADS = 8
SWA_KV_HEADS = 2
SWA_GROUP = SWA_HEADS // SWA_KV_HEADS
SWA_HEAD_DIM = 64
WINDOW = 128
D_FF = 4 * D_MODEL
EPS = 1e-6
N_MOD = 6

LANES = 128
HALF = LANES // 2
BF16_SUBLANES = 16
MLA_QK_COLS = MLA_HEADS * LANES
MLA_V_COLS = MLA_HEADS * MLA_V
SWA_Q_COLS = SWA_HEADS * SWA_HEAD_DIM
SWA_KV_COLS = SWA_KV_HEADS * SWA_HEAD_DIM
C_QLAT = 0
C_KVLAT = C_QLAT + Q_LORA
C_KPE = C_KVLAT + KV_LORA
C_SQ = C_KPE + LANES
C_SK = C_SQ + SWA_Q_COLS
C_SV = C_SK + SWA_KV_COLS
IN_COLS_PAD = C_SV + SWA_KV_COLS

VMEM_LIMIT_BYTES = 62 * 1024 * 1024

LOG2E = float(np.log2(np.e))
MLA_Q_SCALE = (MLA_NOPE + MLA_ROPE) ** -0.5 * LOG2E
SWA_Q_SCALE = SWA_HEAD_DIM ** -0.5 * LOG2E

BF16 = jnp.bfloat16
F32 = jnp.float32


def _const_spec(shape):
    nd = len(shape)
    return pl.BlockSpec(shape, lambda *_: (0,) * nd, pipeline_mode=pl.Buffered(1))


def _rms(x):
    return x * lax.rsqrt(jnp.mean(x * x, axis=-1, keepdims=True) + EPS)


def _ada_kernel(c_ref, w_ref, b_ref, o_ref):
    c = c_ref[...]
    s = c / (1.0 + jnp.exp(-c))
    o_ref[0, :, 0, :] = jnp.dot(s.astype(BF16), w_ref[...].astype(BF16), preferred_element_type=F32) + b_ref[...]


def _ada_call(c, w_ada, b_ada):
    b = c.shape[0]
    n = w_ada.shape[1]
    tn = D_MODEL
    return pl.pallas_call(
        _ada_kernel,
        grid=(n // tn,),
        in_specs=[
            pl.BlockSpec((b, D_MODEL), lambda j: (0, 0)),
            pl.BlockSpec((D_MODEL, tn), lambda j: (0, j)),
            pl.BlockSpec((1, tn), lambda j: (0, j)),
        ],
        out_specs=pl.BlockSpec((1, b, 1, tn), lambda j: (j, 0, 0, 0)),
        out_shape=jax.ShapeDtypeStruct((n // tn, b, 1, tn), F32),
        compiler_params=pltpu.CompilerParams(dimension_semantics=("parallel",)),
        name="ada_mod",
    )(c, w_ada, b_ada.reshape(1, n))


def _mod_spec():
    return pl.BlockSpec((N_MOD, 1, 1, D_MODEL), lambda bi, i: (0, bi, 0, 0))


def _rope_group(xg, keep, swap):
    up = pltpu.roll(xg, LANES - MLA_ROPE // 2, 1)
    return xg * keep + up * swap


def _relayout_weights(w_in_ref, w_qb_ref, w_kvb_ref, win_ref, wq_ref, wk_ref, wv_ref):
    o2 = Q_LORA + KV_LORA
    o3 = o2 + MLA_ROPE
    o4 = o3 + SWA_Q_COLS
    half = MLA_ROPE // 2
    pad = LANES - MLA_NOPE - MLA_ROPE - half
    d_in = w_in_ref.shape[1]
    kpe = [jnp.zeros((MLA_NOPE, d_in), F32), w_in_ref[o2:o3, :], w_in_ref[o2:o2 + half, :],
           jnp.zeros((pad, d_in), F32)]
    sq = [w_in_ref[o3 + (kv * SWA_GROUP + g) * SWA_HEAD_DIM:o3 + (kv * SWA_GROUP + g + 1) * SWA_HEAD_DIM, :]
          for g in range(SWA_GROUP) for kv in range(SWA_KV_HEADS)]
    win_t = jnp.concatenate([w_in_ref[:o2, :]] + kpe + sq + [w_in_ref[o4:, :]], axis=0)
    chunk = 2 * LANES
    for c0 in range(0, IN_COLS_PAD, chunk):
        win_ref[:, c0:c0 + chunk] = win_t[c0:c0 + chunk, :].T.astype(BF16)
    hd_rows = MLA_NOPE + MLA_ROPE
    parts = []
    for hd in range(MLA_HEADS):
        r0 = hd * hd_rows
        parts += [w_qb_ref[r0:r0 + hd_rows, :], w_qb_ref[r0 + MLA_NOPE:r0 + MLA_NOPE + half, :],
                  jnp.zeros((pad, w_qb_ref.shape[1]), F32)]
    wq_ref[...] = jnp.concatenate(parts, axis=0).T.astype(BF16)
    wkvb = w_kvb_ref[...]
    wk_ref[...] = jnp.concatenate([wkvb[:, hd * LANES:hd * LANES + MLA_NOPE] for hd in range(MLA_HEADS)],
                                  axis=1).astype(BF16)
    wv = jnp.concatenate([wkvb[:, hd * LANES + MLA_NOPE:(hd + 1) * LANES] for hd in range(MLA_HEADS)], axis=1)
    wv_ref[...] = wv.T.astype(BF16)


def _pre_kernel(x_ref, mod_ref, gmix_ref, w_in_ref, gqa_ref, w_qb_ref, gkva_ref, w_kvb_ref,
                ropeq_ref, ropek_ref,
                qf_ref, kf_ref, v_ref, sq_ref, sk_ref, sv_ref,
                win_ref, wq_ref, wk_ref, wv_ref, *, ts):
    @pl.when((pl.program_id(0) == 0) & (pl.program_id(1) == 0))
    def _():
        _relayout_weights(w_in_ref, w_qb_ref, w_kvb_ref, win_ref, wq_ref, wk_ref, wv_ref)

    for r0 in range(0, x_ref.shape[1], ts):
        _pre_rows(r0, ts, x_ref, mod_ref, gmix_ref, win_ref, gqa_ref, wq_ref, gkva_ref, wk_ref, wv_ref,
                  ropeq_ref, ropek_ref, qf_ref, kf_ref, v_ref, sq_ref, sk_ref, sv_ref)


def _pre_rows(r0, ts, x_ref, mod_ref, gmix_ref, win_ref, gqa_ref, wq_ref, gkva_ref, wk_ref, wv_ref,
              ropeq_ref, ropek_ref, qf_ref, kf_ref, v_ref, sq_ref, sk_ref, sv_ref):
    rows = slice(r0, r0 + ts)
    x = x_ref[0, rows]
    mod = mod_ref[:, 0, 0, :]
    sh1 = mod[0:1]
    sc1 = mod[1:2]
    h = (_rms(x) * gmix_ref[...] * (1.0 + sc1) + sh1).astype(BF16)
    proj = jnp.dot(h, win_ref[...], preferred_element_type=F32)

    for g in range(SWA_GROUP):
        sq_ref[0, g, rows] = (proj[:, C_SQ + g * LANES:C_SQ + (g + 1) * LANES] * SWA_Q_SCALE).astype(BF16)
    swa_k = proj[:, C_SK:C_SV]
    low = lax.broadcasted_iota(jnp.int32, swa_k.shape, 1) < HALF
    sk_ref[0, rows, :LANES] = jnp.where(low, swa_k, 0.0).astype(BF16)
    sk_ref[0, rows, LANES:] = jnp.where(low, 0.0, swa_k).astype(BF16)
    svt = proj[:, C_SV:IN_COLS_PAD].T.astype(BF16)
    for jb in range(ts // WINDOW):
        sv_ref[0, r0 // WINDOW + jb] = svt[:, jb * WINDOW:(jb + 1) * WINDOW]

    qn = (_rms(proj[:, C_QLAT:C_KVLAT]) * gqa_ref[...]).astype(BF16)
    kvn = (_rms(proj[:, C_KVLAT:C_KPE]) * gkva_ref[...]).astype(BF16)
    q = jnp.dot(qn, wq_ref[...], preferred_element_type=F32)
    kn = jnp.dot(kvn, wk_ref[...], preferred_element_type=F32)
    vt = lax.dot_general(wv_ref[...], kvn, (((1,), (1,)), ((), ())), preferred_element_type=F32).astype(BF16)
    tk = v_ref.shape[-1]
    for hp in range(MLA_HEADS // 2):
        for jb in range(ts // tk):
            v_ref[0, hp, r0 // tk + jb] = vt[hp * LANES:(hp + 1) * LANES, jb * tk:(jb + 1) * tk]

    q_keep, q_swap = ropeq_ref[0, rows], ropeq_ref[1, rows]
    kpe = _rope_group(proj[:, C_KPE:C_SQ], ropek_ref[0, rows], ropek_ref[1, rows])
    nope_lanes = lax.broadcasted_iota(jnp.int32, kpe.shape, 1) < MLA_NOPE
    for hd in range(MLA_HEADS):
        sl = slice(hd * LANES, (hd + 1) * LANES)
        qf_ref[0, rows, sl] = _rope_group(q[:, sl], q_keep, q_swap).astype(BF16)
        kn_pair = kn[:, (hd // 2) * LANES:(hd // 2 + 1) * LANES]
        if hd % 2:
            kn_pair = pltpu.roll(kn_pair, HALF, 1)
        kf_ref[0, rows, sl] = jnp.where(nope_lanes, kn_pair, kpe).astype(BF16)


def _rope_tables(scale_q):
    half = MLA_ROPE // 2
    freqs = ROPE_THETA ** (-np.arange(0, MLA_ROPE, 2, dtype=np.float64) / MLA_ROPE)
    ang = np.arange(SEQ, dtype=np.float64)[:, None] * freqs[None, :]
    cos, sin = np.cos(ang), np.sin(ang)
    keep = np.zeros((SEQ, LANES))
    swap = np.zeros((SEQ, LANES))
    keep[:, :MLA_NOPE] = 1.0
    keep[:, MLA_NOPE:MLA_NOPE + half] = cos
    keep[:, MLA_NOPE + half:MLA_NOPE + MLA_ROPE] = cos
    swap[:, MLA_NOPE:MLA_NOPE + half] = -sin
    swap[:, MLA_NOPE + half:MLA_NOPE + MLA_ROPE] = sin
    return (np.stack([keep, swap]) * scale_q).astype(np.float32)


def _pre_call(x, mod, gmix, w_in, gqa, w_qb, gkva, w_kvb, tm, ts, tk):
    b, s, d = x.shape
    pairs = MLA_HEADS // 2
    ropeq = jnp.asarray(_rope_tables(MLA_Q_SCALE))
    ropek = jnp.asarray(_rope_tables(1.0))
    row = lambda bi, i: (bi, i, 0)
    out_shapes = (
        jax.ShapeDtypeStruct((b, s, MLA_QK_COLS), BF16),
        jax.ShapeDtypeStruct((b, s, MLA_QK_COLS), BF16),
        jax.ShapeDtypeStruct((b, pairs, s // tk, LANES, tk), BF16),
        jax.ShapeDtypeStruct((b, SWA_GROUP, s, LANES), BF16),
        jax.ShapeDtypeStruct((b, s, SWA_KV_HEADS * LANES), BF16),
        jax.ShapeDtypeStruct((b, s // WINDOW, SWA_KV_COLS, WINDOW), BF16),
    )
    out_specs = [
        pl.BlockSpec((1, tm, MLA_QK_COLS), row),
        pl.BlockSpec((1, tm, MLA_QK_COLS), row),
        pl.BlockSpec((1, pairs, tm // tk, LANES, tk), lambda bi, i: (bi, 0, i, 0, 0)),
        pl.BlockSpec((1, SWA_GROUP, tm, LANES), lambda bi, i: (bi, 0, i, 0)),
        pl.BlockSpec((1, tm, SWA_KV_HEADS * LANES), row),
        pl.BlockSpec((1, tm // WINDOW, SWA_KV_COLS, WINDOW), lambda bi, i: (bi, i, 0, 0)),
    ]
    return pl.pallas_call(
        functools.partial(_pre_kernel, ts=ts),
        grid=(b, s // tm),
        in_specs=[
            pl.BlockSpec((1, tm, d), row),
            _mod_spec(),
            _const_spec((1, d)),
            _const_spec(w_in.shape),
            _const_spec((1, Q_LORA)),
            _const_spec(w_qb.shape),
            _const_spec((1, KV_LORA)),
            _const_spec(w_kvb.shape),
            pl.BlockSpec((2, tm, LANES), lambda bi, i: (0, i, 0)),
            pl.BlockSpec((2, tm, LANES), lambda bi, i: (0, i, 0)),
        ],
        out_specs=tuple(out_specs),
        out_shape=out_shapes,
        scratch_shapes=[
            pltpu.VMEM((d, IN_COLS_PAD), BF16),
            pltpu.VMEM((Q_LORA, MLA_QK_COLS), BF16),
            pltpu.VMEM((KV_LORA, MLA_HEADS * MLA_NOPE), BF16),
            pltpu.VMEM((MLA_V_COLS, KV_LORA), BF16),
        ],
        compiler_params=pltpu.CompilerParams(
            dimension_semantics=("arbitrary", "arbitrary"), vmem_limit_bytes=VMEM_LIMIT_BYTES),
        name="pre_attn",
    )(x, mod, gmix, w_in, gqa, w_qb, gkva, w_kvb, ropeq, ropek)


MLA_LOOKAHEAD = 6


def _mla_kernel(q_ref, k_ref, vt_ref, o_ref, *, tq):
    ki = lax.broadcasted_iota(jnp.int32, (tq, tq), 0)
    qi = lax.broadcasted_iota(jnp.int32, (tq, tq), 1)
    causal = ki <= qi
    heads = range(q_ref.shape[2] // LANES)
    chains = [(i, j, h) for i in range(q_ref.shape[1] // tq) for j in range(i + 1) for h in heads]

    def scores(i, j, h):
        cols = slice(h * LANES, (h + 1) * LANES)
        q = q_ref[0, i * tq:(i + 1) * tq, cols]
        k = k_ref[0, j * tq:(j + 1) * tq, cols]
        return lax.dot_general(k, q, (((1,), (1,)), ((), ())), preferred_element_type=F32)

    ones = jnp.ones((BF16_SUBLANES, tq), BF16)
    pending = [scores(*ch) for ch in chains[:MLA_LOOKAHEAD]]
    for n, (i, j, h) in enumerate(chains):
        st = pending.pop(0)
        if n + MLA_LOOKAHEAD < len(chains):
            pending.append(scores(*chains[n + MLA_LOOKAHEAD]))
        if j == 0 and h == 0:
            state = [(jnp.full((1, tq), -jnp.inf, F32), jnp.zeros((MLA_V + BF16_SUBLANES, tq), F32))
                     for _ in heads]
        m, acc = state[h]
        if j == i:
            st = jnp.where(causal, st, -jnp.inf)
        vt = vt_ref[0, h // 2, j, (h % 2) * MLA_V:(h % 2 + 1) * MLA_V, :]
        vt = jnp.concatenate([vt, ones], axis=0)
        m_new = jnp.maximum(m, jnp.max(st, axis=0, keepdims=True))
        alpha = jnp.exp2(m - m_new)
        pt = jnp.exp2(st - m_new).astype(BF16)
        acc_new = alpha * acc + jnp.dot(vt, pt, preferred_element_type=F32)
        state[h] = (m_new, acc_new)
        if j == i and h == len(heads) - 1:
            for p in range(len(heads) // 2):
                pair = [state[2 * p][1], state[2 * p + 1][1]]
                out_t = jnp.concatenate([acc[:MLA_V] / acc[MLA_V:MLA_V + 1] for acc in pair], axis=0)
                o_ref[0, i * tq:(i + 1) * tq, p * LANES:(p + 1) * LANES] = out_t.T.astype(BF16)


def _mla_call(qf, kf, vt, tq, hps):
    b, s, _ = qf.shape
    pairs = MLA_HEADS // 2
    assert vt.shape == (b, pairs, s // tq, LANES, tq) and pairs % hps == 0
    seq_cols = lambda bi, g: (bi, 0, g)
    return pl.pallas_call(
        functools.partial(_mla_kernel, tq=tq),
        grid=(b, pairs // hps),
        in_specs=[
            pl.BlockSpec((1, s, 2 * hps * LANES), seq_cols),
            pl.BlockSpec((1, s, 2 * hps * LANES), seq_cols),
            pl.BlockSpec((1, hps, s // tq, LANES, tq), lambda bi, g: (bi, g, 0, 0, 0)),
        ],
        out_specs=pl.BlockSpec((1, s, hps * LANES), seq_cols),
        out_shape=jax.ShapeDtypeStruct((b, s, MLA_V_COLS), BF16),
        compiler_params=pltpu.CompilerParams(
            dimension_semantics=("parallel", "parallel"), vmem_limit_bytes=VMEM_LIMIT_BYTES),
        name="mla_attn",
    )(qf, kf, vt)


def _alibi_slope(h):
    return 2.0 ** (-8.0 * (h + 1) / SWA_HEADS)


SWA_STACK = 2
SWA_LOOKAHEAD = 6


def _swa_bias():
    k = np.arange(2 * WINDOW)[:, None]
    q = np.arange(WINDOW)[None, :]
    dist = q + WINDOW - k
    valid = (dist >= 0) & (dist < WINDOW)
    out = np.empty((SWA_KV_HEADS, 2 * WINDOW, SWA_GROUP * WINDOW), np.float32)
    for kv in range(SWA_KV_HEADS):
        for g in range(SWA_GROUP):
            slope = np.float32(_alibi_slope(kv * SWA_GROUP + g))
            out[kv, :, g * WINDOW:(g + 1) * WINDOW] = np.where(valid, -slope * dist.astype(np.float32) * LOG2E, -np.inf)
    return out


def _swa_tile_chains(sink_ref, bias_ref, q_ref, k_prev, k_ref, vt_prev, vt_ref, gone, store):
    nsub = q_ref.shape[2] // WINDOW
    ones = jnp.ones((BF16_SUBLANES, 2 * WINDOW), BF16)
    width = SWA_STACK * WINDOW
    chains = [(c, g0, kv) for c in range(nsub) for g0 in range(0, SWA_GROUP, SWA_STACK)
              for kv in range(SWA_KV_HEADS)]

    def scores(c, g0, kv):
        cols = slice(kv * LANES, (kv + 1) * LANES)
        if c == 0:
            k_win = jnp.concatenate([k_prev(cols), k_ref[0, :WINDOW, cols]], axis=0)
        else:
            k_win = k_ref[0, (c - 1) * WINDOW:(c + 1) * WINDOW, cols]
        q_st = q_ref[0, g0:g0 + SWA_STACK, c * WINDOW:(c + 1) * WINDOW, :].reshape(width, LANES)
        st = lax.dot_general(k_win, q_st, (((1,), (1,)), ((), ())), preferred_element_type=F32)
        return st + bias_ref[kv, :, g0 * WINDOW:g0 * WINDOW + width]

    outs = {}
    pending = [scores(*ch) for ch in chains[:SWA_LOOKAHEAD]]
    for n, (c, g0, kv) in enumerate(chains):
        st = pending.pop(0)
        if n + SWA_LOOKAHEAD < len(chains):
            pending.append(scores(*chains[n + SWA_LOOKAHEAD]))
        rows = slice(kv * SWA_HEAD_DIM, (kv + 1) * SWA_HEAD_DIM)
        if c == 0:
            vt = jnp.concatenate([vt_prev(rows), vt_ref[0, 0, rows, :]], axis=1)
            st = jnp.concatenate([st[:WINDOW] + gone, st[WINDOW:]], axis=0)
        else:
            vt = jnp.concatenate([vt_ref[0, c - 1, rows, :], vt_ref[0, c, rows, :]], axis=1)
        sink = sink_ref[kv:kv + 1, g0 * WINDOW:g0 * WINDOW + width]
        m = jnp.maximum(jnp.max(st, axis=0, keepdims=True), sink)
        pt = jnp.exp2(st - m).astype(BF16)
        res = jnp.dot(jnp.concatenate([vt, ones], axis=0), pt, preferred_element_type=F32)
        denom = res[SWA_HEAD_DIM:SWA_HEAD_DIM + 1] + jnp.exp2(sink - m)
        outs[kv] = res[:SWA_HEAD_DIM] / denom
        if kv == SWA_KV_HEADS - 1:
            for gg in range(SWA_STACK):
                gs = slice(gg * WINDOW, (gg + 1) * WINDOW)
                out_t = jnp.concatenate([outs[0][:, gs], outs[1][:, gs]], axis=0)
                store(c, g0 + gg, out_t.T.astype(BF16))
        yield


def _post_kernel(x_ref, mla_ref, mod_ref, wo_ref, gmlp_ref, wup_ref, wdown_ref, gfin_ref,
                 sink_ref, bias_ref, qn_ref, kpn_ref, kn_ref, vtpn_ref, vtn_ref, q0_ref, k0_ref, vt0_ref,
                 o_ref, swa_ref, *, ts, ff_chunk):
    step = pl.program_id(0) * pl.num_programs(1) + pl.program_id(1)
    n_steps = pl.num_programs(0) * pl.num_programs(1)
    n_chunks = D_FF // ff_chunk

    def store(c, g, tile):
        swa_ref[c * WINDOW:(c + 1) * WINDOW, g * LANES:(g + 1) * LANES] = tile

    @pl.when(step == 0)
    def _():
        for _ in _swa_tile_chains(sink_ref, bias_ref, q0_ref, lambda cols: k0_ref[0, :WINDOW, cols], k0_ref,
                                  lambda rows: vt0_ref[0, 0, rows, :], vt0_ref, -jnp.inf, store):
            pass

    mod = mod_ref[:, 0, 0, :]
    g1, sh2, sc2, g2 = mod[2:3], mod[3:4], mod[4:5], mod[5:6]
    woa = wo_ref[:MLA_V_COLS, :].astype(BF16)
    wob = jnp.concatenate(
        [wo_ref[MLA_V_COLS + (kv * SWA_GROUP + g) * SWA_HEAD_DIM:MLA_V_COLS + (kv * SWA_GROUP + g + 1) * SWA_HEAD_DIM, :]
         for g in range(SWA_GROUP) for kv in range(SWA_KV_HEADS)], axis=0).astype(BF16)
    tiles = [slice(r0, r0 + ts) for r0 in range(0, x_ref.shape[1], ts)]
    mid = []
    for rows in tiles:
        attn = (jnp.dot(mla_ref[0, rows], woa, preferred_element_type=F32)
                + jnp.dot(swa_ref[rows, :], wob, preferred_element_type=F32))
        x1 = x_ref[0, rows] + g1 * attn
        h2 = (_rms(x1) * gmlp_ref[...] * (1.0 + sc2) + sh2).astype(BF16)
        mid.append((x1, h2))
    nxt = jnp.minimum(step + 1, n_steps - 1) % pl.num_programs(1)
    gone = jnp.where(nxt == 0, -jnp.inf, 0.0)
    chains = _swa_tile_chains(sink_ref, bias_ref, qn_ref, lambda cols: kpn_ref[0, :, cols], kn_ref,
                              lambda rows: vtpn_ref[0, 0, rows, :], vtn_ref, gone, store)
    n_slots = 2 * n_chunks * len(tiles)
    n_attn = (qn_ref.shape[2] // WINDOW) * (SWA_GROUP // SWA_STACK) * SWA_KV_HEADS
    emitted = 0
    accs = [jnp.zeros_like(x1) for x1, _ in mid]
    for c in range(n_chunks):
        cs = slice(c * ff_chunk, (c + 1) * ff_chunk)
        w_up_c = wup_ref[:, cs].astype(BF16)
        w_down_c = wdown_ref[cs, :].astype(BF16)
        us = []
        for t in range(2 * len(mid)):
            if t < len(mid):
                us.append(jnp.maximum(jnp.dot(mid[t][1], w_up_c, preferred_element_type=F32), 0.0))
            else:
                u = us[t - len(mid)]
                accs[t - len(mid)] = accs[t - len(mid)] + jnp.dot((u * u).astype(BF16), w_down_c,
                                                                  preferred_element_type=F32)
            slot = c * 2 * len(mid) + t + 1
            while emitted < slot * n_attn // n_slots:
                next(chains)
                emitted += 1
    for rows, (x1, _), acc in zip(tiles, mid, accs):
        x2 = x1 + g2 * acc
        o_ref[0, rows] = _rms(x2) * gfin_ref[...]


def _post_call(x, mla, mod, w_o, gmlp, wup, wdown, gfin, sinks, sq, sk, svt, tm, ts, ff_chunk):
    b, s, d = x.shape
    nt = s // tm
    nsub = tm // WINDOW
    bias = jnp.asarray(_swa_bias())
    sink_rows = jnp.repeat(sinks.reshape(SWA_KV_HEADS, SWA_GROUP), WINDOW, axis=1) * LOG2E
    row = lambda bi, i: (bi, i, 0)

    def nxt(bi, i):
        n = jnp.minimum(bi * nt + i + 1, b * nt - 1)
        return n // nt, n % nt

    def nxt_prev_blk(bi, i):
        bn, tn = nxt(bi, i)
        return bn, jnp.maximum(tn * nsub - 1, 0)

    return pl.pallas_call(
        functools.partial(_post_kernel, ts=ts, ff_chunk=ff_chunk),
        grid=(b, nt),
        in_specs=[
            pl.BlockSpec((1, tm, d), row),
            pl.BlockSpec((1, tm, MLA_V_COLS), row),
            _mod_spec(),
            _const_spec(w_o.shape),
            _const_spec((1, d)),
            _const_spec(wup.shape),
            _const_spec(wdown.shape),
            _const_spec((1, d)),
            _const_spec(sink_rows.shape),
            _const_spec(bias.shape),
            pl.BlockSpec((1, SWA_GROUP, tm, LANES), lambda bi, i: (nxt(bi, i)[0], 0, nxt(bi, i)[1], 0)),
            pl.BlockSpec((1, WINDOW, SWA_KV_HEADS * LANES), lambda bi, i: nxt_prev_blk(bi, i) + (0,)),
            pl.BlockSpec((1, tm, SWA_KV_HEADS * LANES), lambda bi, i: nxt(bi, i) + (0,)),
            pl.BlockSpec((1, 1, SWA_KV_COLS, WINDOW), lambda bi, i: nxt_prev_blk(bi, i) + (0, 0)),
            pl.BlockSpec((1, nsub, SWA_KV_COLS, WINDOW), lambda bi, i: nxt(bi, i) + (0, 0)),
            pl.BlockSpec((1, SWA_GROUP, tm, LANES), lambda bi, i: (0, 0, 0, 0), pipeline_mode=pl.Buffered(1)),
            pl.BlockSpec((1, tm, SWA_KV_HEADS * LANES), lambda bi, i: (0, 0, 0), pipeline_mode=pl.Buffered(1)),
            pl.BlockSpec((1, nsub, SWA_KV_COLS, WINDOW), lambda bi, i: (0, 0, 0, 0), pipeline_mode=pl.Buffered(1)),
        ],
        out_specs=pl.BlockSpec((1, tm, d), row),
        out_shape=jax.ShapeDtypeStruct((b, s, d), F32),
        scratch_shapes=[pltpu.VMEM((tm, SWA_Q_COLS), BF16)],
        compiler_params=pltpu.CompilerParams(
            dimension_semantics=("arbitrary", "arbitrary"), vmem_limit_bytes=VMEM_LIMIT_BYTES),
        name="post_attn",
    )(x, mla, mod, w_o, gmlp, wup, wdown, gfin, sink_rows, bias, sq, sk, sk, svt, svt, sq, sk, svt)


def kernel(x, c, w_ada, b_ada, norm_mix_g, w_in, g_qa, w_qb, g_kva, w_kvb, sinks,
           w_o, norm_mlp_g, w_up, w_down, final_g):
    depth = w_ada.shape[0]
    b = x.shape[0]
    assert depth == 1, "the final rmsnorm is fused into the single layer's post-attention call"
    for l in range(depth):
        mod = _ada_call(c, w_ada[l], b_ada[l])
        qf, kf, vt, sq, sk, svt = _pre_call(
            x, mod, norm_mix_g[l][None], w_in[l].T, g_qa[l][None],
            w_qb[l].transpose(1, 2, 0).reshape(-1, Q_LORA), g_kva[l][None], w_kvb[l].reshape(KV_LORA, -1),
            tm=1024, ts=256, tk=256)
        mla = _mla_call(qf, kf, vt, tq=256, hps=2)
        x = _post_call(x, mla, mod, w_o[l], norm_mlp_g[l][None], w_up[l], w_down[l], final_g[None],
                       sinks[l], sq, sk, svt, tm=512, ts=256, ff_chunk=1024)
    return x
```

```python
import functools

import numpy as np
import jax
import jax.numpy as jnp
from jax import lax
from jax.experimental import pallas as pl
from jax.experimental.pallas import tpu as pltpu

D_MODEL = 1024
SEQ = 2048
MLA_HEADS = 8
MLA_NOPE = 64
MLA_ROPE = 32
MLA_V = 64
Q_LORA = 384
KV_LORA = 256
ROPE_THETA = 10000.0
SWA_HEADS = 8
SWA_KV_HEADS = 2
SWA_GROUP = SWA_HEADS // SWA_KV_HEADS
SWA_HEAD_DIM = 64
WINDOW = 128
D_FF = 4 * D_MODEL
EPS = 1e-6
N_MOD = 6

LANES = 128
HALF = LANES // 2
BF16_SUBLANES = 16
MLA_QK_COLS = MLA_HEADS * LANES
MLA_V_COLS = MLA_HEADS * MLA_V
SWA_Q_COLS = SWA_HEADS * SWA_HEAD_DIM
SWA_KV_COLS = SWA_KV_HEADS * SWA_HEAD_DIM
C_QLAT = 0
C_KVLAT = C_QLAT + Q_LORA
C_KPE = C_KVLAT + KV_LORA
C_SQ = C_KPE + LANES
C_SK = C_SQ + SWA_Q_COLS
C_SV = C_SK + SWA_KV_COLS
IN_COLS_PAD = C_SV + SWA_KV_COLS

VMEM_LIMIT_BYTES = 62 * 1024 * 1024

LOG2E = float(np.log2(np.e))
MLA_Q_SCALE = (MLA_NOPE + MLA_ROPE) ** -0.5 * LOG2E
SWA_Q_SCALE = SWA_HEAD_DIM ** -0.5 * LOG2E

BF16 = jnp.bfloat16
F32 = jnp.float32


def _const_spec(shape):
    nd = len(shape)
    return pl.BlockSpec(shape, lambda *_: (0,) * nd, pipeline_mode=pl.Buffered(1))


def _rms(x):
    return x * lax.rsqrt(jnp.mean(x * x, axis=-1, keepdims=True) + EPS)


def _ada_kernel(c_ref, w_ref, b_ref, o_ref):
    c = c_ref[...]
    s = c / (1.0 + jnp.exp(-c))
    o_ref[0, :, 0, :] = jnp.dot(s.astype(BF16), w_ref[...].astype(BF16), preferred_element_type=F32) + b_ref[...]


def _ada_call(c, w_ada, b_ada):
    b = c.shape[0]
    n = w_ada.shape[1]
    tn = D_MODEL
    return pl.pallas_call(
        _ada_kernel,
        grid=(n // tn,),
        in_specs=[
            pl.BlockSpec((b, D_MODEL), lambda j: (0, 0)),
            pl.BlockSpec((D_MODEL, tn), lambda j: (0, j)),
            pl.BlockSpec((1, tn), lambda j: (0, j)),
        ],
        out_specs=pl.BlockSpec((1, b, 1, tn), lambda j: (j, 0, 0, 0)),
        out_shape=jax.ShapeDtypeStruct((n // tn, b, 1, tn), F32),
        compiler_params=pltpu.CompilerParams(dimension_semantics=("parallel",)),
        name="ada_mod",
    )(c, w_ada, b_ada.reshape(1, n))


def _mod_spec():
    return pl.BlockSpec((N_MOD, 1, 1, D_MODEL), lambda bi, i: (0, bi, 0, 0))


def _rope_group(xg, keep, swap):
    up = pltpu.roll(xg, LANES - MLA_ROPE // 2, 1)
    return xg * keep + up * swap


def _relayout_weights(w_in_ref, w_qb_ref, w_kvb_ref, win_ref, wq_ref, wk_ref, wv_ref):
    o2 = Q_LORA + KV_LORA
    o3 = o2 + MLA_ROPE
    o4 = o3 + SWA_Q_COLS
    half = MLA_ROPE // 2
    pad = LANES - MLA_NOPE - MLA_ROPE - half
    d_in = w_in_ref.shape[1]
    kpe = [jnp.zeros((MLA_NOPE, d_in), F32), w_in_ref[o2:o3, :], w_in_ref[o2:o2 + half, :],
           jnp.zeros((pad, d_in), F32)]
    sq = [w_in_ref[o3 + (kv * SWA_GROUP + g) * SWA_HEAD_DIM:o3 + (kv * SWA_GROUP + g + 1) * SWA_HEAD_DIM, :]
          for g in range(SWA_GROUP) for kv in range(SWA_KV_HEADS)]
    win_t = jnp.concatenate([w_in_ref[:o2, :]] + kpe + sq + [w_in_ref[o4:, :]], axis=0)
    chunk = 2 * LANES
    for c0 in range(0, IN_COLS_PAD, chunk):
        win_ref[:, c0:c0 + chunk] = win_t[c0:c0 + chunk, :].T.astype(BF16)
    hd_rows = MLA_NOPE + MLA_ROPE
    parts = []
    for hd in range(MLA_HEADS):
        r0 = hd * hd_rows
        parts += [w_qb_ref[r0:r0 + hd_rows, :], w_qb_ref[r0 + MLA_NOPE:r0 + MLA_NOPE + half, :],
                  jnp.zeros((pad, w_qb_ref.shape[1]), F32)]
    wq_ref[...] = jnp.concatenate(parts, axis=0).T.astype(BF16)
    wkvb = w_kvb_ref[...]
    wk_ref[...] = jnp.concatenate([wkvb[:, hd * LANES:hd * LANES + MLA_NOPE] for hd in range(MLA_HEADS)],
                                  axis=1).astype(BF16)
    wv = jnp.concatenate([wkvb[:, hd * LANES + MLA_NOPE:(hd + 1) * LANES] for hd in range(MLA_HEADS)], axis=1)
    wv_ref[...] = wv.T.astype(BF16)


def _pre_kernel(x_ref, mod_ref, gmix_ref, w_in_ref, gqa_ref, w_qb_ref, gkva_ref, w_kvb_ref,
                ropeq_ref, ropek_ref, wup_ref, wdown_ref,
                qf_ref, kf_ref, v_ref, sq_ref, sk_ref, sv_ref, wup_bf_ref, wdown_bf_ref,
                win_ref, wq_ref, wk_ref, wv_ref, *, ts):
    @pl.when((pl.program_id(0) == 0) & (pl.program_id(1) == 0))
    def _():
        _relayout_weights(w_in_ref, w_qb_ref, w_kvb_ref, win_ref, wq_ref, wk_ref, wv_ref)

    wup_bf_ref[...] = wup_ref[...].astype(BF16)
    wdown_bf_ref[...] = wdown_ref[...].astype(BF16)

    for r0 in range(0, x_ref.shape[1], ts):
        _pre_rows(r0, ts, x_ref, mod_ref, gmix_ref, win_ref, gqa_ref, wq_ref, gkva_ref, wk_ref, wv_ref,
                  ropeq_ref, ropek_ref, qf_ref, kf_ref, v_ref, sq_ref, sk_ref, sv_ref)


def _pre_rows(r0, ts, x_ref, mod_ref, gmix_ref, win_ref, gqa_ref, wq_ref, gkva_ref, wk_ref, wv_ref,
              ropeq_ref, ropek_ref, qf_ref, kf_ref, v_ref, sq_ref, sk_ref, sv_ref):
    rows = slice(r0, r0 + ts)
    x = x_ref[0, rows]
    mod = mod_ref[:, 0, 0, :]
    sh1 = mod[0:1]
    sc1 = mod[1:2]
    h = (_rms(x) * gmix_ref[...] * (1.0 + sc1) + sh1).astype(BF16)
    proj = jnp.dot(h, win_ref[...], preferred_element_type=F32)

    for g in range(SWA_GROUP):
        sq_ref[0, g, rows] = (proj[:, C_SQ + g * LANES:C_SQ + (g + 1) * LANES] * SWA_Q_SCALE).astype(BF16)
    swa_k = proj[:, C_SK:C_SV]
    low = lax.broadcasted_iota(jnp.int32, swa_k.shape, 1) < HALF
    sk_ref[0, rows, :LANES] = jnp.where(low, swa_k, 0.0).astype(BF16)
    sk_ref[0, rows, LANES:] = jnp.where(low, 0.0, swa_k).astype(BF16)
    svt = proj[:, C_SV:IN_COLS_PAD].T.astype(BF16)
    for jb in range(ts // WINDOW):
        sv_ref[0, r0 // WINDOW + jb] = svt[:, jb * WINDOW:(jb + 1) * WINDOW]

    qn = (_rms(proj[:, C_QLAT:C_KVLAT]) * gqa_ref[...]).astype(BF16)
    kvn = (_rms(proj[:, C_KVLAT:C_KPE]) * gkva_ref[...]).astype(BF16)
    q = jnp.dot(qn, wq_ref[...], preferred_element_type=F32)
    kn = jnp.dot(kvn, wk_ref[...], preferred_element_type=F32)
    vt = lax.dot_general(wv_ref[...], kvn, (((1,), (1,)), ((), ())), preferred_element_type=F32).astype(BF16)
    tk = v_ref.shape[-1]
    for hp in range(MLA_HEADS // 2):
        for jb in range(ts // tk):
            v_ref[0, hp, r0 // tk + jb] = vt[hp * LANES:(hp + 1) * LANES, jb * tk:(jb + 1) * tk]

    q_keep, q_swap = ropeq_ref[0, rows], ropeq_ref[1, rows]
    kpe = _rope_group(proj[:, C_KPE:C_SQ], ropek_ref[0, rows], ropek_ref[1, rows])
    nope_lanes = lax.broadcasted_iota(jnp.int32, kpe.shape, 1) < MLA_NOPE
    for hd in range(MLA_HEADS):
        sl = slice(hd * LANES, (hd + 1) * LANES)
        qf_ref[0, rows, sl] = _rope_group(q[:, sl], q_keep, q_swap).astype(BF16)
        kn_pair = kn[:, (hd // 2) * LANES:(hd // 2 + 1) * LANES]
        if hd % 2:
            kn_pair = pltpu.roll(kn_pair, HALF, 1)
        kf_ref[0, rows, sl] = jnp.where(nope_lanes, kn_pair, kpe).astype(BF16)


def _rope_tables(scale_q):
    half = MLA_ROPE // 2
    freqs = ROPE_THETA ** (-np.arange(0, MLA_ROPE, 2, dtype=np.float64) / MLA_ROPE)
    ang = np.arange(SEQ, dtype=np.float64)[:, None] * freqs[None, :]
    cos, sin = np.cos(ang), np.sin(ang)
    keep = np.zeros((SEQ, LANES))
    swap = np.zeros((SEQ, LANES))
    keep[:, :MLA_NOPE] = 1.0
    keep[:, MLA_NOPE:MLA_NOPE + half] = cos
    keep[:, MLA_NOPE + half:MLA_NOPE + MLA_ROPE] = cos
    swap[:, MLA_NOPE:MLA_NOPE + half] = -sin
    swap[:, MLA_NOPE + half:MLA_NOPE + MLA_ROPE] = sin
    return (np.stack([keep, swap]) * scale_q).astype(np.float32)


def _pre_call(x, mod, gmix, w_in, gqa, w_qb, gkva, w_kvb, w_up, w_down, tm, ts, tk):
    b, s, d = x.shape
    pairs = MLA_HEADS // 2
    n_steps = b * (s // tm)
    step_rows = lambda bi, i: (bi * (s // tm) + i, 0)
    ropeq = jnp.asarray(_rope_tables(MLA_Q_SCALE))
    ropek = jnp.asarray(_rope_tables(1.0))
    row = lambda bi, i: (bi, i, 0)
    out_shapes = (
        jax.ShapeDtypeStruct((b, s, MLA_QK_COLS), BF16),
        jax.ShapeDtypeStruct((b, s, MLA_QK_COLS), BF16),
        jax.ShapeDtypeStruct((b, pairs, s // tk, LANES, tk), BF16),
        jax.ShapeDtypeStruct((b, SWA_GROUP, s, LANES), BF16),
        jax.ShapeDtypeStruct((b, s, SWA_KV_HEADS * LANES), BF16),
        jax.ShapeDtypeStruct((b, s // WINDOW, SWA_KV_COLS, WINDOW), BF16),
        jax.ShapeDtypeStruct(w_up.shape, BF16),
        jax.ShapeDtypeStruct(w_down.shape, BF16),
    )
    out_specs = [
        pl.BlockSpec((1, tm, MLA_QK_COLS), row),
        pl.BlockSpec((1, tm, MLA_QK_COLS), row),
        pl.BlockSpec((1, pairs, tm // tk, LANES, tk), lambda bi, i: (bi, 0, i, 0, 0)),
        pl.BlockSpec((1, SWA_GROUP, tm, LANES), lambda bi, i: (bi, 0, i, 0)),
        pl.BlockSpec((1, tm, SWA_KV_HEADS * LANES), row),
        pl.BlockSpec((1, tm // WINDOW, SWA_KV_COLS, WINDOW), lambda bi, i: (bi, i, 0, 0)),
        pl.BlockSpec((w_up.shape[0] // n_steps, w_up.shape[1]), step_rows),
        pl.BlockSpec((w_down.shape[0] // n_steps, w_down.shape[1]), step_rows),
    ]
    return pl.pallas_call(
        functools.partial(_pre_kernel, ts=ts),
        grid=(b, s // tm),
        in_specs=[
            pl.BlockSpec((1, tm, d), row),
            _mod_spec(),
            _const_spec((1, d)),
            _const_spec(w_in.shape),
            _const_spec((1, Q_LORA)),
            _const_spec(w_qb.shape),
            _const_spec((1, KV_LORA)),
            _const_spec(w_kvb.shape),
            pl.BlockSpec((2, tm, LANES), lambda bi, i: (0, i, 0)),
            pl.BlockSpec((2, tm, LANES), lambda bi, i: (0, i, 0)),
            pl.BlockSpec((w_up.shape[0] // n_steps, w_up.shape[1]), step_rows),
            pl.BlockSpec((w_down.shape[0] // n_steps, w_down.shape[1]), step_rows),
        ],
        out_specs=tuple(out_specs),
        out_shape=out_shapes,
        scratch_shapes=[
            pltpu.VMEM((d, IN_COLS_PAD), BF16),
            pltpu.VMEM((Q_LORA, MLA_QK_COLS), BF16),
            pltpu.VMEM((KV_LORA, MLA_HEADS * MLA_NOPE), BF16),
            pltpu.VMEM((MLA_V_COLS, KV_LORA), BF16),
        ],
        compiler_params=pltpu.CompilerParams(
            dimension_semantics=("arbitrary", "arbitrary"), vmem_limit_bytes=VMEM_LIMIT_BYTES),
        name="pre_attn",
    )(x, mod, gmix, w_in, gqa, w_qb, gkva, w_kvb, ropeq, ropek, w_up, w_down)


MLA_LOOKAHEAD = 6


def _mla_kernel(q_ref, k_ref, vt_ref, o_ref, *, tq):
    ki = lax.broadcasted_iota(jnp.int32, (tq, tq), 0)
    qi = lax.broadcasted_iota(jnp.int32, (tq, tq), 1)
    causal = ki <= qi
    heads = range(q_ref.shape[2] // LANES)
    chains = [(i, j, h) for i in range(q_ref.shape[1] // tq) for j in range(i + 1) for h in heads]

    def scores(i, j, h):
        cols = slice(h * LANES, (h + 1) * LANES)
        q = q_ref[0, i * tq:(i + 1) * tq, cols]
        k = k_ref[0, j * tq:(j + 1) * tq, cols]
        return lax.dot_general(k, q, (((1,), (1,)), ((), ())), preferred_element_type=F32)

    ones = jnp.ones((BF16_SUBLANES, tq), BF16)
    pending = [scores(*ch) for ch in chains[:MLA_LOOKAHEAD]]
    for n, (i, j, h) in enumerate(chains):
        st = pending.pop(0)
        if n + MLA_LOOKAHEAD < len(chains):
            pending.append(scores(*chains[n + MLA_LOOKAHEAD]))
        if j == 0 and h == 0:
            state = [(jnp.full((1, tq), -jnp.inf, F32), jnp.zeros((MLA_V + BF16_SUBLANES, tq), F32))
                     for _ in heads]
        m, acc = state[h]
        if j == i:
            st = jnp.where(causal, st, -jnp.inf)
        vt = vt_ref[0, h // 2, j, (h % 2) * MLA_V:(h % 2 + 1) * MLA_V, :]
        vt = jnp.concatenate([vt, ones], axis=0)
        m_new = jnp.maximum(m, jnp.max(st, axis=0, keepdims=True))
        alpha = jnp.exp2(m - m_new)
        pt = jnp.exp2(st - m_new).astype(BF16)
        acc_new = alpha * acc + jnp.dot(vt, pt, preferred_element_type=F32)
        state[h] = (m_new, acc_new)
        if j == i and h == len(heads) - 1:
            for p in range(len(heads) // 2):
                pair = [state[2 * p][1], state[2 * p + 1][1]]
                out_t = jnp.concatenate([acc[:MLA_V] / acc[MLA_V:MLA_V + 1] for acc in pair], axis=0)
                o_ref[0, i * tq:(i + 1) * tq, p * LANES:(p + 1) * LANES] = out_t.T.astype(BF16)


def _mla_call(qf, kf, vt, tq, hps):
    b, s, _ = qf.shape
    pairs = MLA_HEADS // 2
    assert vt.shape == (b, pairs, s // tq, LANES, tq) and pairs % hps == 0
    seq_cols = lambda bi, g: (bi, 0, g)
    return pl.pallas_call(
        functools.partial(_mla_kernel, tq=tq),
        grid=(b, pairs // hps),
        in_specs=[
            pl.BlockSpec((1, s, 2 * hps * LANES), seq_cols),
            pl.BlockSpec((1, s, 2 * hps * LANES), seq_cols),
            pl.BlockSpec((1, hps, s // tq, LANES, tq), lambda bi, g: (bi, g, 0, 0, 0)),
        ],
        out_specs=pl.BlockSpec((1, s, hps * LANES), seq_cols),
        out_shape=jax.ShapeDtypeStruct((b, s, MLA_V_COLS), BF16),
        compiler_params=pltpu.CompilerParams(
            dimension_semantics=("parallel", "parallel"), vmem_limit_bytes=VMEM_LIMIT_BYTES),
        name="mla_attn",
    )(qf, kf, vt)


def _alibi_slope(h):
    return 2.0 ** (-8.0 * (h + 1) / SWA_HEADS)


SWA_STACK = 2
SWA_LOOKAHEAD = 6


def _swa_bias():
    k = np.arange(2 * WINDOW)[:, None]
    q = np.arange(WINDOW)[None, :]
    dist = q + WINDOW - k
    valid = (dist >= 0) & (dist < WINDOW)
    out = np.empty((SWA_KV_HEADS, 2 * WINDOW, SWA_GROUP * WINDOW), np.float32)
    for kv in range(SWA_KV_HEADS):
        for g in range(SWA_GROUP):
            slope = np.float32(_alibi_slope(kv * SWA_GROUP + g))
            out[kv, :, g * WINDOW:(g + 1) * WINDOW] = np.where(valid, -slope * dist.astype(np.float32) * LOG2E, -np.inf)
    return out


def _swa_tile_chains(sink_ref, bias_ref, q_ref, k_prev, k_ref, vt_prev, vt_ref, gone, store):
    nsub = q_ref.shape[2] // WINDOW
    ones = jnp.ones((BF16_SUBLANES, 2 * WINDOW), BF16)
    width = SWA_STACK * WINDOW
    chains = [(c, g0, kv) for c in range(nsub) for g0 in range(0, SWA_GROUP, SWA_STACK)
              for kv in range(SWA_KV_HEADS)]

    def scores(c, g0, kv):
        cols = slice(kv * LANES, (kv + 1) * LANES)
        if c == 0:
            k_win = jnp.concatenate([k_prev(cols), k_ref[0, :WINDOW, cols]], axis=0)
        else:
            k_win = k_ref[0, (c - 1) * WINDOW:(c + 1) * WINDOW, cols]
        q_st = q_ref[0, g0:g0 + SWA_STACK, c * WINDOW:(c + 1) * WINDOW, :].reshape(width, LANES)
        st = lax.dot_general(k_win, q_st, (((1,), (1,)), ((), ())), preferred_element_type=F32)
        return st + bias_ref[kv, :, g0 * WINDOW:g0 * WINDOW + width]

    outs = {}
    pending = [scores(*ch) for ch in chains[:SWA_LOOKAHEAD]]
    for n, (c, g0, kv) in enumerate(chains):
        st = pending.pop(0)
        if n + SWA_LOOKAHEAD < len(chains):
            pending.append(scores(*chains[n + SWA_LOOKAHEAD]))
        rows = slice(kv * SWA_HEAD_DIM, (kv + 1) * SWA_HEAD_DIM)
        if c == 0:
            vt = jnp.concatenate([vt_prev(rows), vt_ref[0, 0, rows, :]], axis=1)
            st = jnp.concatenate([st[:WINDOW] + gone, st[WINDOW:]], axis=0)
        else:
            vt = jnp.concatenate([vt_ref[0, c - 1, rows, :], vt_ref[0, c, rows, :]], axis=1)
        sink = sink_ref[kv:kv + 1, g0 * WINDOW:g0 * WINDOW + width]
        m = jnp.maximum(jnp.max(st, axis=0, keepdims=True), sink)
        pt = jnp.exp2(st - m).astype(BF16)
        res = jnp.dot(jnp.concatenate([vt, ones], axis=0), pt, preferred_element_type=F32)
        denom = res[SWA_HEAD_DIM:SWA_HEAD_DIM + 1] + jnp.exp2(sink - m)
        outs[kv] = res[:SWA_HEAD_DIM] / denom
        if kv == SWA_KV_HEADS - 1:
            for gg in range(SWA_STACK):
                gs = slice(gg * WINDOW, (gg + 1) * WINDOW)
                out_t = jnp.concatenate([outs[0][:, gs], outs[1][:, gs]], axis=0)
                store(c, g0 + gg, out_t.T.astype(BF16))
        yield


def _post_kernel(x_ref, mla_ref, mod_ref, wo_ref, gmlp_ref, wup_ref, wdown_ref, gfin_ref,
                 sink_ref, bias_ref, qn_ref, kpn_ref, kn_ref, vtpn_ref, vtn_ref, q0_ref, k0_ref, vt0_ref,
                 o_ref, swa_ref, *, ts, ff_chunk):
    step = pl.program_id(0) * pl.num_programs(1) + pl.program_id(1)
    n_steps = pl.num_programs(0) * pl.num_programs(1)
    n_chunks = D_FF // ff_chunk

    def store(c, g, tile):
        swa_ref[c * WINDOW:(c + 1) * WINDOW, g * LANES:(g + 1) * LANES] = tile

    @pl.when(step == 0)
    def _():
        for _ in _swa_tile_chains(sink_ref, bias_ref, q0_ref, lambda cols: k0_ref[0, :WINDOW, cols], k0_ref,
                                  lambda rows: vt0_ref[0, 0, rows, :], vt0_ref, -jnp.inf, store):
            pass

    mod = mod_ref[:, 0, 0, :]
    g1, sh2, sc2, g2 = mod[2:3], mod[3:4], mod[4:5], mod[5:6]
    woa = wo_ref[:MLA_V_COLS, :].astype(BF16)
    wob = jnp.concatenate(
        [wo_ref[MLA_V_COLS + (kv * SWA_GROUP + g) * SWA_HEAD_DIM:MLA_V_COLS + (kv * SWA_GROUP + g + 1) * SWA_HEAD_DIM, :]
         for g in range(SWA_GROUP) for kv in range(SWA_KV_HEADS)], axis=0).astype(BF16)
    tiles = [slice(r0, r0 + ts) for r0 in range(0, x_ref.shape[1], ts)]
    mid = []
    for rows in tiles:
        attn = (jnp.dot(mla_ref[0, rows], woa, preferred_element_type=F32)
                + jnp.dot(swa_ref[rows, :], wob, preferred_element_type=F32))
        x1 = x_ref[0, rows] + g1 * attn
        h2 = (_rms(x1) * gmlp_ref[...] * (1.0 + sc2) + sh2).astype(BF16)
        mid.append((x1, h2))
    nxt = jnp.minimum(step + 1, n_steps - 1) % pl.num_programs(1)
    gone = jnp.where(nxt == 0, -jnp.inf, 0.0)
    chains = _swa_tile_chains(sink_ref, bias_ref, qn_ref, lambda cols: kpn_ref[0, :, cols], kn_ref,
                              lambda rows: vtpn_ref[0, 0, rows, :], vtn_ref, gone, store)
    n_slots = 2 * n_chunks * len(tiles)
    n_attn = (qn_ref.shape[2] // WINDOW) * (SWA_GROUP // SWA_STACK) * SWA_KV_HEADS
    emitted = 0
    accs = [jnp.zeros_like(x1) for x1, _ in mid]
    for c in range(n_chunks):
        cs = slice(c * ff_chunk, (c + 1) * ff_chunk)
        w_up_c = wup_ref[:, cs]
        w_down_c = wdown_ref[cs, :]
        us = []
        for t in range(2 * len(mid)):
            if t < len(mid):
                us.append(jnp.maximum(jnp.dot(mid[t][1], w_up_c, preferred_element_type=F32), 0.0))
            else:
                u = us[t - len(mid)]
                accs[t - len(mid)] = accs[t - len(mid)] + jnp.dot((u * u).astype(BF16), w_down_c,
                                                                  preferred_element_type=F32)
            slot = c * 2 * len(mid) + t + 1
            while emitted < slot * n_attn // n_slots:
                next(chains)
                emitted += 1
    for rows, (x1, _), acc in zip(tiles, mid, accs):
        x2 = x1 + g2 * acc
        o_ref[0, rows] = _rms(x2) * gfin_ref[...]


def _post_call(x, mla, mod, w_o, gmlp, wup, wdown, gfin, sinks, sq, sk, svt, tm, ts, ff_chunk):
    b, s, d = x.shape
    nt = s // tm
    nsub = tm // WINDOW
    bias = jnp.asarray(_swa_bias())
    sink_rows = jnp.repeat(sinks.reshape(SWA_KV_HEADS, SWA_GROUP), WINDOW, axis=1) * LOG2E
    row = lambda bi, i: (bi, i, 0)

    def nxt(bi, i):
        n = jnp.minimum(bi * nt + i + 1, b * nt - 1)
        return n // nt, n % nt

    def nxt_prev_blk(bi, i):
        bn, tn = nxt(bi, i)
        return bn, jnp.maximum(tn * nsub - 1, 0)

    return pl.pallas_call(
        functools.partial(_post_kernel, ts=ts, ff_chunk=ff_chunk),
        grid=(b, nt),
        in_specs=[
            pl.BlockSpec((1, tm, d), row),
            pl.BlockSpec((1, tm, MLA_V_COLS), row),
            _mod_spec(),
            _const_spec(w_o.shape),
            _const_spec((1, d)),
            _const_spec(wup.shape),
            _const_spec(wdown.shape),
            _const_spec((1, d)),
            _const_spec(sink_rows.shape),
            _const_spec(bias.shape),
            pl.BlockSpec((1, SWA_GROUP, tm, LANES), lambda bi, i: (nxt(bi, i)[0], 0, nxt(bi, i)[1], 0)),
            pl.BlockSpec((1, WINDOW, SWA_KV_HEADS * LANES), lambda bi, i: nxt_prev_blk(bi, i) + (0,)),
            pl.BlockSpec((1, tm, SWA_KV_HEADS * LANES), lambda bi, i: nxt(bi, i) + (0,)),
            pl.BlockSpec((1, 1, SWA_KV_COLS, WINDOW), lambda bi, i: nxt_prev_blk(bi, i) + (0, 0)),
            pl.BlockSpec((1, nsub, SWA_KV_COLS, WINDOW), lambda bi, i: nxt(bi, i) + (0, 0)),
            pl.BlockSpec((1, SWA_GROUP, tm, LANES), lambda bi, i: (0, 0, 0, 0), pipeline_mode=pl.Buffered(1)),
            pl.BlockSpec((1, tm, SWA_KV_HEADS * LANES), lambda bi, i: (0, 0, 0), pipeline_mode=pl.Buffered(1)),
            pl.BlockSpec((1, nsub, SWA_KV_COLS, WINDOW), lambda bi, i: (0, 0, 0, 0), pipeline_mode=pl.Buffered(1)),
        ],
        out_specs=pl.BlockSpec((1, tm, d), row),
        out_shape=jax.ShapeDtypeStruct((b, s, d), F32),
        scratch_shapes=[pltpu.VMEM((tm, SWA_Q_COLS), BF16)],
        compiler_params=pltpu.CompilerParams(
            dimension_semantics=("arbitrary", "arbitrary"), vmem_limit_bytes=VMEM_LIMIT_BYTES),
        name="post_attn",
    )(x, mla, mod, w_o, gmlp, wup, wdown, gfin, sink_rows, bias, sq, sk, sk, svt, svt, sq, sk, svt)


def kernel(x, c, w_ada, b_ada, norm_mix_g, w_in, g_qa, w_qb, g_kva, w_kvb, sinks,
           w_o, norm_mlp_g, w_up, w_down, final_g):
    depth = w_ada.shape[0]
    b = x.shape[0]
    assert depth == 1, "the final rmsnorm is fused into the single layer's post-attention call"
    for l in range(depth):
        mod = _ada_call(c, w_ada[l], b_ada[l])
        qf, kf, vt, sq, sk, svt, w_up_bf, w_down_bf = _pre_call(
            x, mod, norm_mix_g[l][None], w_in[l].T, g_qa[l][None],
            w_qb[l].transpose(1, 2, 0).reshape(-1, Q_LORA), g_kva[l][None], w_kvb[l].reshape(KV_LORA, -1),
            w_up[l], w_down[l], tm=1024, ts=256, tk=256)
        mla = _mla_call(qf, kf, vt, tq=256, hps=2)
        x = _post_call(x, mla, mod, w_o[l], norm_mlp_g[l][None], w_up_bf, w_down_bf, final_g[None],
                       sinks[l], sq, sk, svt, tm=1024, ts=256, ff_chunk=1024)
    return x
```

```python
import functools

import numpy as np
import jax
import jax.numpy as jnp
from jax import lax
from jax.experimental import pallas as pl
from jax.experimental.pallas import tpu as pltpu

D_MODEL = 1024
SEQ = 2048
MLA_HEADS = 8
MLA_NOPE = 64
MLA_ROPE = 32
MLA_V = 64
Q_LORA = 384
KV_LORA = 256
ROPE_THETA = 10000.0
SWA_HEADS = 8
SWA_KV_HEADS = 2
SWA_GROUP = SWA_HEADS // SWA_KV_HEADS
SWA_HEAD_DIM = 64
WINDOW = 128
D_FF = 4 * D_MODEL
EPS = 1e-6
N_MOD = 6

LANES = 128
HALF = LANES // 2
BF16_SUBLANES = 16
MLA_QK_COLS = MLA_HEADS * LANES
MLA_V_COLS = MLA_HEADS * MLA_V
SWA_Q_COLS = SWA_HEADS * SWA_HEAD_DIM
SWA_KV_COLS = SWA_KV_HEADS * SWA_HEAD_DIM
C_QLAT = 0
C_KVLAT = C_QLAT + Q_LORA
C_KPE = C_KVLAT + KV_LORA
C_SQ = C_KPE + LANES
C_SK = C_SQ + SWA_Q_COLS
C_SV = C_SK + SWA_KV_COLS
IN_COLS_PAD = C_SV + SWA_KV_COLS

VMEM_LIMIT_BYTES = 62 * 1024 * 1024

LOG2E = float(np.log2(np.e))
MLA_Q_SCALE = (MLA_NOPE + MLA_ROPE) ** -0.5 * LOG2E
SWA_Q_SCALE = SWA_HEAD_DIM ** -0.5 * LOG2E

BF16 = jnp.bfloat16
F32 = jnp.float32


def _const_spec(shape):
    nd = len(shape)
    return pl.BlockSpec(shape, lambda *_: (0,) * nd, pipeline_mode=pl.Buffered(1))


def _rms(x):
    return x * lax.rsqrt(jnp.mean(x * x, axis=-1, keepdims=True) + EPS)


def _ada_kernel(c_ref, w_ref, b_ref, o_ref):
    c = c_ref[...]
    s = c / (1.0 + jnp.exp(-c))
    o_ref[0, :, 0, :] = jnp.dot(s.astype(BF16), w_ref[...].astype(BF16), preferred_element_type=F32) + b_ref[...]


def _ada_call(c, w_ada, b_ada):
    b = c.shape[0]
    n = w_ada.shape[1]
    tn = D_MODEL
    return pl.pallas_call(
        _ada_kernel,
        grid=(n // tn,),
        in_specs=[
            pl.BlockSpec((b, D_MODEL), lambda j: (0, 0)),
            pl.BlockSpec((D_MODEL, tn), lambda j: (0, j)),
            pl.BlockSpec((1, tn), lambda j: (0, j)),
        ],
        out_specs=pl.BlockSpec((1, b, 1, tn), lambda j: (j, 0, 0, 0)),
        out_shape=jax.ShapeDtypeStruct((n // tn, b, 1, tn), F32),
        compiler_params=pltpu.CompilerParams(dimension_semantics=("parallel",)),
        name="ada_mod",
    )(c, w_ada, b_ada.reshape(1, n))


def _mod_spec():
    return pl.BlockSpec((N_MOD, 1, 1, D_MODEL), lambda bi, i: (0, bi, 0, 0))


def _rope_group(xg, keep, swap):
    up = pltpu.roll(xg, LANES - MLA_ROPE // 2, 1)
    return xg * keep + up * swap


def _relayout_weights(w_in_ref, w_qb_ref, w_kvb_ref, win_ref, wq_ref, wk_ref, wv_ref):
    o2 = Q_LORA + KV_LORA
    o3 = o2 + MLA_ROPE
    o4 = o3 + SWA_Q_COLS
    half = MLA_ROPE // 2
    pad = LANES - MLA_NOPE - MLA_ROPE - half
    d_in = w_in_ref.shape[1]
    kpe = [jnp.zeros((MLA_NOPE, d_in), F32), w_in_ref[o2:o3, :], w_in_ref[o2:o2 + half, :],
           jnp.zeros((pad, d_in), F32)]
    sq = [w_in_ref[o3 + (kv * SWA_GROUP + g) * SWA_HEAD_DIM:o3 + (kv * SWA_GROUP + g + 1) * SWA_HEAD_DIM, :]
          for g in range(SWA_GROUP) for kv in range(SWA_KV_HEADS)]
    win_t = jnp.concatenate([w_in_ref[:o2, :]] + kpe + sq + [w_in_ref[o4:, :]], axis=0)
    chunk = 2 * LANES
    for c0 in range(0, IN_COLS_PAD, chunk):
        win_ref[:, c0:c0 + chunk] = win_t[c0:c0 + chunk, :].T.astype(BF16)
    hd_rows = MLA_NOPE + MLA_ROPE
    parts = []
    for hd in range(MLA_HEADS):
        r0 = hd * hd_rows
        parts += [w_qb_ref[r0:r0 + hd_rows, :], w_qb_ref[r0 + MLA_NOPE:r0 + MLA_NOPE + half, :],
                  jnp.zeros((pad, w_qb_ref.shape[1]), F32)]
    wq_ref[...] = jnp.concatenate(parts, axis=0).T.astype(BF16)
    wkvb = w_kvb_ref[...]
    wk_ref[...] = jnp.concatenate([wkvb[:, hd * LANES:hd * LANES + MLA_NOPE] for hd in range(MLA_HEADS)],
                                  axis=1).astype(BF16)
    wv = jnp.concatenate([wkvb[:, hd * LANES + MLA_NOPE:(hd + 1) * LANES] for hd in range(MLA_HEADS)], axis=1)
    wv_ref[...] = wv.T.astype(BF16)


def _pre_kernel(x_ref, mod_ref, gmix_ref, w_in_ref, gqa_ref, w_qb_ref, gkva_ref, w_kvb_ref,
                ropeq_ref, ropek_ref,
                qf_ref, kf_ref, v_ref, sq_ref, sk_ref, sv_ref,
                win_ref, wq_ref, wk_ref, wv_ref, *, ts):
    @pl.when((pl.program_id(0) == 0) & (pl.program_id(1) == 0))
    def _():
        _relayout_weights(w_in_ref, w_qb_ref, w_kvb_ref, win_ref, wq_ref, wk_ref, wv_ref)

    for r0 in range(0, x_ref.shape[1], ts):
        _pre_rows(r0, ts, x_ref, mod_ref, gmix_ref, win_ref, gqa_ref, wq_ref, gkva_ref, wk_ref, wv_ref,
                  ropeq_ref, ropek_ref, qf_ref, kf_ref, v_ref, sq_ref, sk_ref, sv_ref)


def _pre_rows(r0, ts, x_ref, mod_ref, gmix_ref, win_ref, gqa_ref, wq_ref, gkva_ref, wk_ref, wv_ref,
              ropeq_ref, ropek_ref, qf_ref, kf_ref, v_ref, sq_ref, sk_ref, sv_ref):
    rows = slice(r0, r0 + ts)
    x = x_ref[0, rows]
    mod = mod_ref[:, 0, 0, :]
    sh1 = mod[0:1]
    sc1 = mod[1:2]
    h = (_rms(x) * gmix_ref[...] * (1.0 + sc1) + sh1).astype(BF16)
    proj = jnp.dot(h, win_ref[...], preferred_element_type=F32)

    for g in range(SWA_GROUP):
        sq_ref[0, g, rows] = (proj[:, C_SQ + g * LANES:C_SQ + (g + 1) * LANES] * SWA_Q_SCALE).astype(BF16)
    swa_k = proj[:, C_SK:C_SV]
    low = lax.broadcasted_iota(jnp.int32, swa_k.shape, 1) < HALF
    sk_ref[0, rows, :LANES] = jnp.where(low, swa_k, 0.0).astype(BF16)
    sk_ref[0, rows, LANES:] = jnp.where(low, 0.0, swa_k).astype(BF16)
    svt = proj[:, C_SV:IN_COLS_PAD].T.astype(BF16)
    for jb in range(ts // WINDOW):
        sv_ref[0, r0 // WINDOW + jb] = svt[:, jb * WINDOW:(jb + 1) * WINDOW]

    qn = (_rms(proj[:, C_QLAT:C_KVLAT]) * gqa_ref[...]).astype(BF16)
    kvn = (_rms(proj[:, C_KVLAT:C_KPE]) * gkva_ref[...]).astype(BF16)
    q = jnp.dot(qn, wq_ref[...], preferred_element_type=F32)
    kn = jnp.dot(kvn, wk_ref[...], preferred_element_type=F32)
    vt = lax.dot_general(wv_ref[...], kvn, (((1,), (1,)), ((), ())), preferred_element_type=F32).astype(BF16)
    tk = v_ref.shape[-1]
    for hp in range(MLA_HEADS // 2):
        for jb in range(ts // tk):
            v_ref[0, hp, r0 // tk + jb] = vt[hp * LANES:(hp + 1) * LANES, jb * tk:(jb + 1) * tk]

    q_keep, q_swap = ropeq_ref[0, rows], ropeq_ref[1, rows]
    kpe = _rope_group(proj[:, C_KPE:C_SQ], ropek_ref[0, rows], ropek_ref[1, rows])
    nope_lanes = lax.broadcasted_iota(jnp.int32, kpe.shape, 1) < MLA_NOPE
    for hd in range(MLA_HEADS):
        sl = slice(hd * LANES, (hd + 1) * LANES)
        qf_ref[0, rows, sl] = _rope_group(q[:, sl], q_keep, q_swap).astype(BF16)
        kn_pair = kn[:, (hd // 2) * LANES:(hd // 2 + 1) * LANES]
        if hd % 2:
            kn_pair = pltpu.roll(kn_pair, HALF, 1)
        kf_ref[0, rows, sl] = jnp.where(nope_lanes, kn_pair, kpe).astype(BF16)


def _rope_tables(scale_q):
    half = MLA_ROPE // 2
    freqs = ROPE_THETA ** (-np.arange(0, MLA_ROPE, 2, dtype=np.float64) / MLA_ROPE)
    ang = np.arange(SEQ, dtype=np.float64)[:, None] * freqs[None, :]
    cos, sin = np.cos(ang), np.sin(ang)
    keep = np.zeros((SEQ, LANES))
    swap = np.zeros((SEQ, LANES))
    keep[:, :MLA_NOPE] = 1.0
    keep[:, MLA_NOPE:MLA_NOPE + half] = cos
    keep[:, MLA_NOPE + half:MLA_NOPE + MLA_ROPE] = cos
    swap[:, MLA_NOPE:MLA_NOPE + half] = -sin
    swap[:, MLA_NOPE + half:MLA_NOPE + MLA_ROPE] = sin
    return (np.stack([keep, swap]) * scale_q).astype(np.float32)


def _pre_call(x, mod, gmix, w_in, gqa, w_qb, gkva, w_kvb, tm, ts, tk):
    b, s, d = x.shape
    pairs = MLA_HEADS // 2
    ropeq = jnp.asarray(_rope_tables(MLA_Q_SCALE))
    ropek = jnp.asarray(_rope_tables(1.0))
    row = lambda bi, i: (bi, i, 0)
    out_shapes = (
        jax.ShapeDtypeStruct((b, s, MLA_QK_COLS), BF16),
        jax.ShapeDtypeStruct((b, s, MLA_QK_COLS), BF16),
        jax.ShapeDtypeStruct((b, pairs, s // tk, LANES, tk), BF16),
        jax.ShapeDtypeStruct((b, SWA_GROUP, s, LANES), BF16),
        jax.ShapeDtypeStruct((b, s, SWA_KV_HEADS * LANES), BF16),
        jax.ShapeDtypeStruct((b, s // WINDOW, SWA_KV_COLS, WINDOW), BF16),
    )
    out_specs = [
        pl.BlockSpec((1, tm, MLA_QK_COLS), row),
        pl.BlockSpec((1, tm, MLA_QK_COLS), row),
        pl.BlockSpec((1, pairs, tm // tk, LANES, tk), lambda bi, i: (bi, 0, i, 0, 0)),
        pl.BlockSpec((1, SWA_GROUP, tm, LANES), lambda bi, i: (bi, 0, i, 0)),
        pl.BlockSpec((1, tm, SWA_KV_HEADS * LANES), row),
        pl.BlockSpec((1, tm // WINDOW, SWA_KV_COLS, WINDOW), lambda bi, i: (bi, i, 0, 0)),
    ]
    return pl.pallas_call(
        functools.partial(_pre_kernel, ts=ts),
        grid=(b, s // tm),
        in_specs=[
            pl.BlockSpec((1, tm, d), row),
            _mod_spec(),
            _const_spec((1, d)),
            _const_spec(w_in.shape),
            _const_spec((1, Q_LORA)),
            _const_spec(w_qb.shape),
            _const_spec((1, KV_LORA)),
            _const_spec(w_kvb.shape),
            pl.BlockSpec((2, tm, LANES), lambda bi, i: (0, i, 0)),
            pl.BlockSpec((2, tm, LANES), lambda bi, i: (0, i, 0)),
        ],
        out_specs=tuple(out_specs),
        out_shape=out_shapes,
        scratch_shapes=[
            pltpu.VMEM((d, IN_COLS_PAD), BF16),
            pltpu.VMEM((Q_LORA, MLA_QK_COLS), BF16),
            pltpu.VMEM((KV_LORA, MLA_HEADS * MLA_NOPE), BF16),
            pltpu.VMEM((MLA_V_COLS, KV_LORA), BF16),
        ],
        compiler_params=pltpu.CompilerParams(
            dimension_semantics=("arbitrary", "arbitrary"), vmem_limit_bytes=VMEM_LIMIT_BYTES),
        name="pre_attn",
    )(x, mod, gmix, w_in, gqa, w_qb, gkva, w_kvb, ropeq, ropek)


MLA_LOOKAHEAD = 6


def _mla_kernel(q_ref, k_ref, vt_ref, wup_ref, wdown_ref, o_ref, wup_bf_ref, wdown_bf_ref, *, tq):
    wup_bf_ref[...] = wup_ref[...].astype(BF16)
    wdown_bf_ref[...] = wdown_ref[...].astype(BF16)
    ki = lax.broadcasted_iota(jnp.int32, (tq, tq), 0)
    qi = lax.broadcasted_iota(jnp.int32, (tq, tq), 1)
    causal = ki <= qi
    heads = range(q_ref.shape[2] // LANES)
    chains = [(i, j, h) for i in range(q_ref.shape[1] // tq) for j in range(i + 1) for h in heads]

    def scores(i, j, h):
        cols = slice(h * LANES, (h + 1) * LANES)
        q = q_ref[0, i * tq:(i + 1) * tq, cols]
        k = k_ref[0, j * tq:(j + 1) * tq, cols]
        return lax.dot_general(k, q, (((1,), (1,)), ((), ())), preferred_element_type=F32)

    ones = jnp.ones((BF16_SUBLANES, tq), BF16)
    pending = [scores(*ch) for ch in chains[:MLA_LOOKAHEAD]]
    for n, (i, j, h) in enumerate(chains):
        st = pending.pop(0)
        if n + MLA_LOOKAHEAD < len(chains):
            pending.append(scores(*chains[n + MLA_LOOKAHEAD]))
        if j == 0 and h == 0:
            state = [(jnp.full((1, tq), -jnp.inf, F32), jnp.zeros((MLA_V + BF16_SUBLANES, tq), F32))
                     for _ in heads]
        m, acc = state[h]
        if j == i:
            st = jnp.where(causal, st, -jnp.inf)
        vt = vt_ref[0, h // 2, j, (h % 2) * MLA_V:(h % 2 + 1) * MLA_V, :]
        vt = jnp.concatenate([vt, ones], axis=0)
        m_new = jnp.maximum(m, jnp.max(st, axis=0, keepdims=True))
        alpha = jnp.exp2(m - m_new)
        pt = jnp.exp2(st - m_new).astype(BF16)
        acc_new = alpha * acc + jnp.dot(vt, pt, preferred_element_type=F32)
        state[h] = (m_new, acc_new)
        if j == i and h == len(heads) - 1:
            for p in range(len(heads) // 2):
                pair = [state[2 * p][1], state[2 * p + 1][1]]
                out_t = jnp.concatenate([acc[:MLA_V] / acc[MLA_V:MLA_V + 1] for acc in pair], axis=0)
                o_ref[0, i * tq:(i + 1) * tq, p * LANES:(p + 1) * LANES] = out_t.T.astype(BF16)


def _mla_call(qf, kf, vt, w_up, w_down, tq, hps):
    b, s, _ = qf.shape
    pairs = MLA_HEADS // 2
    assert vt.shape == (b, pairs, s // tq, LANES, tq) and pairs % hps == 0
    seq_cols = lambda bi, g: (bi, 0, g)
    n_steps = b * (pairs // hps)
    step_rows = lambda bi, g: (bi * (pairs // hps) + g, 0)
    w_specs = [pl.BlockSpec((w.shape[0] // n_steps, w.shape[1]), step_rows) for w in (w_up, w_down)]
    return pl.pallas_call(
        functools.partial(_mla_kernel, tq=tq),
        grid=(b, pairs // hps),
        in_specs=[
            pl.BlockSpec((1, s, 2 * hps * LANES), seq_cols),
            pl.BlockSpec((1, s, 2 * hps * LANES), seq_cols),
            pl.BlockSpec((1, hps, s // tq, LANES, tq), lambda bi, g: (bi, g, 0, 0, 0)),
        ] + w_specs,
        out_specs=[pl.BlockSpec((1, s, hps * LANES), seq_cols)] + w_specs,
        out_shape=[jax.ShapeDtypeStruct((b, s, MLA_V_COLS), BF16),
                   jax.ShapeDtypeStruct(w_up.shape, BF16), jax.ShapeDtypeStruct(w_down.shape, BF16)],
        compiler_params=pltpu.CompilerParams(
            dimension_semantics=("parallel", "parallel"), vmem_limit_bytes=VMEM_LIMIT_BYTES),
        name="mla_attn",
    )(qf, kf, vt, w_up, w_down)


def _alibi_slope(h):
    return 2.0 ** (-8.0 * (h + 1) / SWA_HEADS)


SWA_STACK = 2
SWA_LOOKAHEAD = 6


def _swa_bias():
    k = np.arange(2 * WINDOW)[:, None]
    q = np.arange(WINDOW)[None, :]
    dist = q + WINDOW - k
    valid = (dist >= 0) & (dist < WINDOW)
    out = np.empty((SWA_KV_HEADS, 2 * WINDOW, SWA_GROUP * WINDOW), np.float32)
    for kv in range(SWA_KV_HEADS):
        for g in range(SWA_GROUP):
            slope = np.float32(_alibi_slope(kv * SWA_GROUP + g))
            out[kv, :, g * WINDOW:(g + 1) * WINDOW] = np.where(valid, -slope * dist.astype(np.float32) * LOG2E, -np.inf)
    return out


def _swa_tile_chains(sink_ref, bias_ref, q_ref, k_prev, k_ref, vt_prev, vt_ref, gone, store):
    nsub = q_ref.shape[2] // WINDOW
    ones = jnp.ones((BF16_SUBLANES, 2 * WINDOW), BF16)
    width = SWA_STACK * WINDOW
    chains = [(c, g0, kv) for c in range(nsub) for g0 in range(0, SWA_GROUP, SWA_STACK)
              for kv in range(SWA_KV_HEADS)]

    def scores(c, g0, kv):
        cols = slice(kv * LANES, (kv + 1) * LANES)
        if c == 0:
            k_win = jnp.concatenate([k_prev(cols), k_ref[0, :WINDOW, cols]], axis=0)
        else:
            k_win = k_ref[0, (c - 1) * WINDOW:(c + 1) * WINDOW, cols]
        q_st = q_ref[0, g0:g0 + SWA_STACK, c * WINDOW:(c + 1) * WINDOW, :].reshape(width, LANES)
        st = lax.dot_general(k_win, q_st, (((1,), (1,)), ((), ())), preferred_element_type=F32)
        return st + bias_ref[kv, :, g0 * WINDOW:g0 * WINDOW + width]

    outs = {}
    pending = [scores(*ch) for ch in chains[:SWA_LOOKAHEAD]]
    for n, (c, g0, kv) in enumerate(chains):
        st = pending.pop(0)
        if n + SWA_LOOKAHEAD < len(chains):
            pending.append(scores(*chains[n + SWA_LOOKAHEAD]))
        rows = slice(kv * SWA_HEAD_DIM, (kv + 1) * SWA_HEAD_DIM)
        if c == 0:
            vt = jnp.concatenate([vt_prev(rows), vt_ref[0, 0, rows, :]], axis=1)
            st = jnp.concatenate([st[:WINDOW] + gone, st[WINDOW:]], axis=0)
        else:
            vt = jnp.concatenate([vt_ref[0, c - 1, rows, :], vt_ref[0, c, rows, :]], axis=1)
        sink = sink_ref[kv:kv + 1, g0 * WINDOW:g0 * WINDOW + width]
        m = jnp.maximum(jnp.max(st, axis=0, keepdims=True), sink)
        pt = jnp.exp2(st - m).astype(BF16)
        res = jnp.dot(jnp.concatenate([vt, ones], axis=0), pt, preferred_element_type=F32)
        denom = res[SWA_HEAD_DIM:SWA_HEAD_DIM + 1] + jnp.exp2(sink - m)
        outs[kv] = res[:SWA_HEAD_DIM] / denom
        if kv == SWA_KV_HEADS - 1:
            for gg in range(SWA_STACK):
                gs = slice(gg * WINDOW, (gg + 1) * WINDOW)
                out_t = jnp.concatenate([outs[0][:, gs], outs[1][:, gs]], axis=0)
                store(c, g0 + gg, out_t.T.astype(BF16))
        yield


def _post_kernel(x_ref, mla_ref, mod_ref, wo_ref, gmlp_ref, wup_ref, wdown_ref, gfin_ref,
                 sink_ref, bias_ref, qn_ref, kpn_ref, kn_ref, vtpn_ref, vtn_ref, q0_ref, k0_ref, vt0_ref,
                 o_ref, swa_ref, *, ts, ff_chunk):
    step = pl.program_id(0) * pl.num_programs(1) + pl.program_id(1)
    n_steps = pl.num_programs(0) * pl.num_programs(1)
    n_chunks = D_FF // ff_chunk

    def store(c, g, tile):
        swa_ref[c * WINDOW:(c + 1) * WINDOW, g * LANES:(g + 1) * LANES] = tile

    @pl.when(step == 0)
    def _():
        for _ in _swa_tile_chains(sink_ref, bias_ref, q0_ref, lambda cols: k0_ref[0, :WINDOW, cols], k0_ref,
                                  lambda rows: vt0_ref[0, 0, rows, :], vt0_ref, -jnp.inf, store):
            pass

    mod = mod_ref[:, 0, 0, :]
    g1, sh2, sc2, g2 = mod[2:3], mod[3:4], mod[4:5], mod[5:6]
    woa = wo_ref[:MLA_V_COLS, :].astype(BF16)
    wob = jnp.concatenate(
        [wo_ref[MLA_V_COLS + (kv * SWA_GROUP + g) * SWA_HEAD_DIM:MLA_V_COLS + (kv * SWA_GROUP + g + 1) * SWA_HEAD_DIM, :]
         for g in range(SWA_GROUP) for kv in range(SWA_KV_HEADS)], axis=0).astype(BF16)
    tiles = [slice(r0, r0 + ts) for r0 in range(0, x_ref.shape[1], ts)]
    mid = []
    for rows in tiles:
        attn = (jnp.dot(mla_ref[0, rows], woa, preferred_element_type=F32)
                + jnp.dot(swa_ref[rows, :], wob, preferred_element_type=F32))
        x1 = x_ref[0, rows] + g1 * attn
        h2 = (_rms(x1) * gmlp_ref[...] * (1.0 + sc2) + sh2).astype(BF16)
        mid.append((x1, h2))
    nxt = jnp.minimum(step + 1, n_steps - 1) % pl.num_programs(1)
    gone = jnp.where(nxt == 0, -jnp.inf, 0.0)
    chains = _swa_tile_chains(sink_ref, bias_ref, qn_ref, lambda cols: kpn_ref[0, :, cols], kn_ref,
                              lambda rows: vtpn_ref[0, 0, rows, :], vtn_ref, gone, store)
    n_slots = 2 * n_chunks * len(tiles)
    n_attn = (qn_ref.shape[2] // WINDOW) * (SWA_GROUP // SWA_STACK) * SWA_KV_HEADS
    emitted = 0
    accs = [jnp.zeros_like(x1) for x1, _ in mid]
    for c in range(n_chunks):
        cs = slice(c * ff_chunk, (c + 1) * ff_chunk)
        w_up_c = wup_ref[:, cs]
        w_down_c = wdown_ref[cs, :]
        us = []
        for t in range(2 * len(mid)):
            if t < len(mid):
                us.append(jnp.maximum(jnp.dot(mid[t][1], w_up_c, preferred_element_type=F32), 0.0))
            else:
                u = us[t - len(mid)]
                accs[t - len(mid)] = accs[t - len(mid)] + jnp.dot((u * u).astype(BF16), w_down_c,
                                                                  preferred_element_type=F32)
            slot = c * 2 * len(mid) + t + 1
            while emitted < slot * n_attn // n_slots:
                next(chains)
                emitted += 1
    for rows, (x1, _), acc in zip(tiles, mid, accs):
        x2 = x1 + g2 * acc
        o_ref[0, rows] = _rms(x2) * gfin_ref[...]


def _post_call(x, mla, mod, w_o, gmlp, wup, wdown, gfin, sinks, sq, sk, svt, tm, ts, ff_chunk):
    b, s, d = x.shape
    nt = s // tm
    nsub = tm // WINDOW
    bias = jnp.asarray(_swa_bias())
    sink_rows = jnp.repeat(sinks.reshape(SWA_KV_HEADS, SWA_GROUP), WINDOW, axis=1) * LOG2E
    row = lambda bi, i: (bi, i, 0)

    def nxt(bi, i):
        n = jnp.minimum(bi * nt + i + 1, b * nt - 1)
        return n // nt, n % nt

    def nxt_prev_blk(bi, i):
        bn, tn = nxt(bi, i)
        return bn, jnp.maximum(tn * nsub - 1, 0)

    return pl.pallas_call(
        functools.partial(_post_kernel, ts=ts, ff_chunk=ff_chunk),
        grid=(b, nt),
        in_specs=[
            pl.BlockSpec((1, tm, d), row),
            pl.BlockSpec((1, tm, MLA_V_COLS), row),
            _mod_spec(),
            _const_spec(w_o.shape),
            _const_spec((1, d)),
            _const_spec(wup.shape),
            _const_spec(wdown.shape),
            _const_spec((1, d)),
            _const_spec(sink_rows.shape),
            _const_spec(bias.shape),
            pl.BlockSpec((1, SWA_GROUP, tm, LANES), lambda bi, i: (nxt(bi, i)[0], 0, nxt(bi, i)[1], 0)),
            pl.BlockSpec((1, WINDOW, SWA_KV_HEADS * LANES), lambda bi, i: nxt_prev_blk(bi, i) + (0,)),
            pl.BlockSpec((1, tm, SWA_KV_HEADS * LANES), lambda bi, i: nxt(bi, i) + (0,)),
            pl.BlockSpec((1, 1, SWA_KV_COLS, WINDOW), lambda bi, i: nxt_prev_blk(bi, i) + (0, 0)),
            pl.BlockSpec((1, nsub, SWA_KV_COLS, WINDOW), lambda bi, i: nxt(bi, i) + (0, 0)),
            pl.BlockSpec((1, SWA_GROUP, tm, LANES), lambda bi, i: (0, 0, 0, 0), pipeline_mode=pl.Buffered(1)),
            pl.BlockSpec((1, tm, SWA_KV_HEADS * LANES), lambda bi, i: (0, 0, 0), pipeline_mode=pl.Buffered(1)),
            pl.BlockSpec((1, nsub, SWA_KV_COLS, WINDOW), lambda bi, i: (0, 0, 0, 0), pipeline_mode=pl.Buffered(1)),
        ],
        out_specs=pl.BlockSpec((1, tm, d), row),
        out_shape=jax.ShapeDtypeStruct((b, s, d), F32),
        scratch_shapes=[pltpu.VMEM((tm, SWA_Q_COLS), BF16)],
        compiler_params=pltpu.CompilerParams(
            dimension_semantics=("arbitrary", "arbitrary"), vmem_limit_bytes=VMEM_LIMIT_BYTES),
        name="post_attn",
    )(x, mla, mod, w_o, gmlp, wup, wdown, gfin, sink_rows, bias, sq, sk, sk, svt, svt, sq, sk, svt)


def kernel(x, c, w_ada, b_ada, norm_mix_g, w_in, g_qa, w_qb, g_kva, w_kvb, sinks,
           w_o, norm_mlp_g, w_up, w_down, final_g):
    depth = w_ada.shape[0]
    b = x.shape[0]
    assert depth == 1, "the final rmsnorm is fused into the single layer's post-attention call"
    for l in range(depth):
        mod = _ada_call(c, w_ada[l], b_ada[l])
        qf, kf, vt, sq, sk, svt = _pre_call(
            x, mod, norm_mix_g[l][None], w_in[l].T, g_qa[l][None],
            w_qb[l].transpose(1, 2, 0).reshape(-1, Q_LORA), g_kva[l][None], w_kvb[l].reshape(KV_LORA, -1),
            tm=1024, ts=256, tk=256)
        mla, w_up_bf, w_down_bf = _mla_call(qf, kf, vt, w_up[l], w_down[l], tq=256, hps=2)
        x = _post_call(x, mla, mod, w_o[l], norm_mlp_g[l][None], w_up_bf, w_down_bf, final_g[None],
                       sinks[l], sq, sk, svt, tm=1024, ts=256, ff_chunk=1024)
    return x
```

```python
import functools

import numpy as np
import jax
import jax.numpy as jnp
from jax import lax
from jax.experimental import pallas as pl
from jax.experimental.pallas import tpu as pltpu

D_MODEL = 1024
SEQ = 2048
MLA_HEADS = 8
MLA_NOPE = 64
MLA_ROPE = 32
MLA_V = 64
Q_LORA = 384
KV_LORA = 256
ROPE_THETA = 10000.0
SWA_HEADS = 8
SWA_KV_HEADS = 2
SWA_GROUP = SWA_HEADS // SWA_KV_HEADS
SWA_HEAD_DIM = 64
WINDOW = 128
D_FF = 4 * D_MODEL
EPS = 1e-6
N_MOD = 6

LANES = 128
HALF = LANES // 2
BF16_SUBLANES = 16
MLA_QK_COLS = MLA_HEADS * LANES
MLA_V_COLS = MLA_HEADS * MLA_V
SWA_Q_COLS = SWA_HEADS * SWA_HEAD_DIM
SWA_KV_COLS = SWA_KV_HEADS * SWA_HEAD_DIM
C_QLAT = 0
C_KVLAT = C_QLAT + Q_LORA
C_KPE = C_KVLAT + KV_LORA
C_SQ = C_KPE + LANES
C_SK = C_SQ + SWA_Q_COLS
C_SV = C_SK + SWA_KV_COLS
IN_COLS_PAD = C_SV + SWA_KV_COLS

VMEM_LIMIT_BYTES = 62 * 1024 * 1024

LOG2E = float(np.log2(np.e))
MLA_Q_SCALE = (MLA_NOPE + MLA_ROPE) ** -0.5 * LOG2E
SWA_Q_SCALE = SWA_HEAD_DIM ** -0.5 * LOG2E

BF16 = jnp.bfloat16
F32 = jnp.float32


def _const_spec(shape):
    nd = len(shape)
    return pl.BlockSpec(shape, lambda *_: (0,) * nd, pipeline_mode=pl.Buffered(1))


def _rms(x):
    return x * lax.rsqrt(jnp.mean(x * x, axis=-1, keepdims=True) + EPS)


def _ada_kernel(c_ref, w_ref, b_ref, o_ref):
    c = c_ref[...]
    s = c / (1.0 + jnp.exp(-c))
    o_ref[0, :, 0, :] = jnp.dot(s.astype(BF16), w_ref[...].astype(BF16), preferred_element_type=F32) + b_ref[...]


def _ada_call(c, w_ada, b_ada):
    b = c.shape[0]
    n = w_ada.shape[1]
    tn = D_MODEL
    return pl.pallas_call(
        _ada_kernel,
        grid=(n // tn,),
        in_specs=[
            pl.BlockSpec((b, D_MODEL), lambda j: (0, 0)),
            pl.BlockSpec((D_MODEL, tn), lambda j: (0, j)),
            pl.BlockSpec((1, tn), lambda j: (0, j)),
        ],
        out_specs=pl.BlockSpec((1, b, 1, tn), lambda j: (j, 0, 0, 0)),
        out_shape=jax.ShapeDtypeStruct((n // tn, b, 1, tn), F32),
        compiler_params=pltpu.CompilerParams(dimension_semantics=("parallel",)),
        name="ada_mod",
    )(c, w_ada, b_ada.reshape(1, n))


def _mod_spec():
    return pl.BlockSpec((N_MOD, 1, 1, D_MODEL), lambda bi, i: (0, bi, 0, 0))


def _rope_group(xg, keep, swap):
    up = pltpu.roll(xg, LANES - MLA_ROPE // 2, 1)
    return xg * keep + up * swap


def _relayout_weights(w_in_ref, w_qb_ref, w_kvb_ref, win_ref, wq_ref, wk_ref, wv_ref):
    o2 = Q_LORA + KV_LORA
    o3 = o2 + MLA_ROPE
    o4 = o3 + SWA_Q_COLS
    half = MLA_ROPE // 2
    pad = LANES - MLA_NOPE - MLA_ROPE - half
    d_in = w_in_ref.shape[1]
    kpe = [jnp.zeros((MLA_NOPE, d_in), F32), w_in_ref[o2:o3, :], w_in_ref[o2:o2 + half, :],
           jnp.zeros((pad, d_in), F32)]
    sq = [w_in_ref[o3 + (kv * SWA_GROUP + g) * SWA_HEAD_DIM:o3 + (kv * SWA_GROUP + g + 1) * SWA_HEAD_DIM, :]
          for g in range(SWA_GROUP) for kv in range(SWA_KV_HEADS)]
    win_t = jnp.concatenate([w_in_ref[:o2, :]] + kpe + sq + [w_in_ref[o4:, :]], axis=0)
    chunk = 2 * LANES
    for c0 in range(0, IN_COLS_PAD, chunk):
        win_ref[:, c0:c0 + chunk] = win_t[c0:c0 + chunk, :].T.astype(BF16)
    hd_rows = MLA_NOPE + MLA_ROPE
    parts = []
    for hd in range(MLA_HEADS):
        r0 = hd * hd_rows
        parts += [w_qb_ref[r0:r0 + hd_rows, :], w_qb_ref[r0 + MLA_NOPE:r0 + MLA_NOPE + half, :],
                  jnp.zeros((pad, w_qb_ref.shape[1]), F32)]
    wq_ref[...] = jnp.concatenate(parts, axis=0).T.astype(BF16)
    wkvb = w_kvb_ref[...]
    wk_ref[...] = jnp.concatenate([wkvb[:, hd * LANES:hd * LANES + MLA_NOPE] for hd in range(MLA_HEADS)],
                                  axis=1).astype(BF16)
    wv = jnp.concatenate([wkvb[:, hd * LANES + MLA_NOPE:(hd + 1) * LANES] for hd in range(MLA_HEADS)], axis=1)
    wv_ref[...] = wv.T.astype(BF16)


def _pre_kernel(x_ref, mod_ref, gmix_ref, w_in_ref, gqa_ref, w_qb_ref, gkva_ref, w_kvb_ref,
                ropeq_ref, ropek_ref,
                qf_ref, kf_ref, v_ref, sq_ref, sk_ref, sv_ref,
                win_ref, wq_ref, wk_ref, wv_ref, *, ts):
    @pl.when((pl.program_id(0) == 0) & (pl.program_id(1) == 0))
    def _():
        _relayout_weights(w_in_ref, w_qb_ref, w_kvb_ref, win_ref, wq_ref, wk_ref, wv_ref)

    for r0 in range(0, x_ref.shape[1], ts):
        _pre_rows(r0, ts, x_ref, mod_ref, gmix_ref, win_ref, gqa_ref, wq_ref, gkva_ref, wk_ref, wv_ref,
                  ropeq_ref, ropek_ref, qf_ref, kf_ref, v_ref, sq_ref, sk_ref, sv_ref)


def _pre_rows(r0, ts, x_ref, mod_ref, gmix_ref, win_ref, gqa_ref, wq_ref, gkva_ref, wk_ref, wv_ref,
              ropeq_ref, ropek_ref, qf_ref, kf_ref, v_ref, sq_ref, sk_ref, sv_ref):
    rows = slice(r0, r0 + ts)
    x = x_ref[0, rows]
    mod = mod_ref[:, 0, 0, :]
    sh1 = mod[0:1]
    sc1 = mod[1:2]
    h = (_rms(x) * gmix_ref[...] * (1.0 + sc1) + sh1).astype(BF16)
    proj = jnp.dot(h, win_ref[...], preferred_element_type=F32)

    for g in range(SWA_GROUP):
        sq_ref[0, g, rows] = (proj[:, C_SQ + g * LANES:C_SQ + (g + 1) * LANES] * SWA_Q_SCALE).astype(BF16)
    swa_k = proj[:, C_SK:C_SV]
    low = lax.broadcasted_iota(jnp.int32, swa_k.shape, 1) < HALF
    sk_ref[0, rows, :LANES] = jnp.where(low, swa_k, 0.0).astype(BF16)
    sk_ref[0, rows, LANES:] = jnp.where(low, 0.0, swa_k).astype(BF16)
    svt = proj[:, C_SV:IN_COLS_PAD].T.astype(BF16)
    for jb in range(ts // WINDOW):
        sv_ref[0, r0 // WINDOW + jb] = svt[:, jb * WINDOW:(jb + 1) * WINDOW]

    qn = (_rms(proj[:, C_QLAT:C_KVLAT]) * gqa_ref[...]).astype(BF16)
    kvn = (_rms(proj[:, C_KVLAT:C_KPE]) * gkva_ref[...]).astype(BF16)
    q = jnp.dot(qn, wq_ref[...], preferred_element_type=F32)
    kn = jnp.dot(kvn, wk_ref[...], preferred_element_type=F32)
    vt = lax.dot_general(wv_ref[...], kvn, (((1,), (1,)), ((), ())), preferred_element_type=F32).astype(BF16)
    tk = v_ref.shape[-1]
    for hp in range(MLA_HEADS // 2):
        for jb in range(ts // tk):
            v_ref[0, hp, r0 // tk + jb] = vt[hp * LANES:(hp + 1) * LANES, jb * tk:(jb + 1) * tk]

    q_keep, q_swap = ropeq_ref[0, rows], ropeq_ref[1, rows]
    kpe = _rope_group(proj[:, C_KPE:C_SQ], ropek_ref[0, rows], ropek_ref[1, rows])
    nope_lanes = lax.broadcasted_iota(jnp.int32, kpe.shape, 1) < MLA_NOPE
    for hd in range(MLA_HEADS):
        sl = slice(hd * LANES, (hd + 1) * LANES)
        qf_ref[0, rows, sl] = _rope_group(q[:, sl], q_keep, q_swap).astype(BF16)
        kn_pair = kn[:, (hd // 2) * LANES:(hd // 2 + 1) * LANES]
        if hd % 2:
            kn_pair = pltpu.roll(kn_pair, HALF, 1)
        kf_ref[0, rows, sl] = jnp.where(nope_lanes, kn_pair, kpe).astype(BF16)


def _rope_tables(scale_q):
    half = MLA_ROPE // 2
    freqs = ROPE_THETA ** (-np.arange(0, MLA_ROPE, 2, dtype=np.float64) / MLA_ROPE)
    ang = np.arange(SEQ, dtype=np.float64)[:, None] * freqs[None, :]
    cos, sin = np.cos(ang), np.sin(ang)
    keep = np.zeros((SEQ, LANES))
    swap = np.zeros((SEQ, LANES))
    keep[:, :MLA_NOPE] = 1.0
    keep[:, MLA_NOPE:MLA_NOPE + half] = cos
    keep[:, MLA_NOPE + half:MLA_NOPE + MLA_ROPE] = cos
    swap[:, MLA_NOPE:MLA_NOPE + half] = -sin
    swap[:, MLA_NOPE + half:MLA_NOPE + MLA_ROPE] = sin
    return (np.stack([keep, swap]) * scale_q).astype(np.float32)


def _pre_call(x, mod, gmix, w_in, gqa, w_qb, gkva, w_kvb, tm, ts, tk):
    b, s, d = x.shape
    pairs = MLA_HEADS // 2
    ropeq = jnp.asarray(_rope_tables(MLA_Q_SCALE))
    ropek = jnp.asarray(_rope_tables(1.0))
    row = lambda bi, i: (bi, i, 0)
    out_shapes = (
        jax.ShapeDtypeStruct((b, s, MLA_QK_COLS), BF16),
        jax.ShapeDtypeStruct((b, s, MLA_QK_COLS), BF16),
        jax.ShapeDtypeStruct((b, pairs, s // tk, LANES, tk), BF16),
        jax.ShapeDtypeStruct((b, SWA_GROUP, s, LANES), BF16),
        jax.ShapeDtypeStruct((b, s, SWA_KV_HEADS * LANES), BF16),
        jax.ShapeDtypeStruct((b, s // WINDOW, SWA_KV_COLS, WINDOW), BF16),
    )
    out_specs = [
        pl.BlockSpec((1, tm, MLA_QK_COLS), row),
        pl.BlockSpec((1, tm, MLA_QK_COLS), row),
        pl.BlockSpec((1, pairs, tm // tk, LANES, tk), lambda bi, i: (bi, 0, i, 0, 0)),
        pl.BlockSpec((1, SWA_GROUP, tm, LANES), lambda bi, i: (bi, 0, i, 0)),
        pl.BlockSpec((1, tm, SWA_KV_HEADS * LANES), row),
        pl.BlockSpec((1, tm // WINDOW, SWA_KV_COLS, WINDOW), lambda bi, i: (bi, i, 0, 0)),
    ]
    return pl.pallas_call(
        functools.partial(_pre_kernel, ts=ts),
        grid=(b, s // tm),
        in_specs=[
            pl.BlockSpec((1, tm, d), row),
            _mod_spec(),
            _const_spec((1, d)),
            _const_spec(w_in.shape),
            _const_spec((1, Q_LORA)),
            _const_spec(w_qb.shape),
            _const_spec((1, KV_LORA)),
            _const_spec(w_kvb.shape),
            pl.BlockSpec((2, tm, LANES), lambda bi, i: (0, i, 0)),
            pl.BlockSpec((2, tm, LANES), lambda bi, i: (0, i, 0)),
        ],
        out_specs=tuple(out_specs),
        out_shape=out_shapes,
        scratch_shapes=[
            pltpu.VMEM((d, IN_COLS_PAD), BF16),
            pltpu.VMEM((Q_LORA, MLA_QK_COLS), BF16),
            pltpu.VMEM((KV_LORA, MLA_HEADS * MLA_NOPE), BF16),
            pltpu.VMEM((MLA_V_COLS, KV_LORA), BF16),
        ],
        compiler_params=pltpu.CompilerParams(
            dimension_semantics=("arbitrary", "arbitrary"), vmem_limit_bytes=VMEM_LIMIT_BYTES),
        name="pre_attn",
    )(x, mod, gmix, w_in, gqa, w_qb, gkva, w_kvb, ropeq, ropek)


MLA_LOOKAHEAD = 6


def _mla_kernel(q_ref, k_ref, vt_ref, wup_ref, wdown_ref, wo_ref, o_ref, wup_bf_ref, wdown_bf_ref, wo_bf_ref, *, tq):
    wup_bf_ref[...] = wup_ref[...].astype(BF16)
    wdown_bf_ref[...] = wdown_ref[...].astype(BF16)
    wo_bf_ref[...] = wo_ref[...].astype(BF16)
    ki = lax.broadcasted_iota(jnp.int32, (tq, tq), 0)
    qi = lax.broadcasted_iota(jnp.int32, (tq, tq), 1)
    causal = ki <= qi
    heads = range(q_ref.shape[2] // LANES)
    chains = [(i, j, h) for i in range(q_ref.shape[1] // tq) for j in range(i + 1) for h in heads]

    def scores(i, j, h):
        cols = slice(h * LANES, (h + 1) * LANES)
        q = q_ref[0, i * tq:(i + 1) * tq, cols]
        k = k_ref[0, j * tq:(j + 1) * tq, cols]
        return lax.dot_general(k, q, (((1,), (1,)), ((), ())), preferred_element_type=F32)

    ones = jnp.ones((BF16_SUBLANES, tq), BF16)
    pending = [scores(*ch) for ch in chains[:MLA_LOOKAHEAD]]
    for n, (i, j, h) in enumerate(chains):
        st = pending.pop(0)
        if n + MLA_LOOKAHEAD < len(chains):
            pending.append(scores(*chains[n + MLA_LOOKAHEAD]))
        if j == 0 and h == 0:
            state = [(jnp.full((1, tq), -jnp.inf, F32), jnp.zeros((MLA_V + BF16_SUBLANES, tq), F32))
                     for _ in heads]
        m, acc = state[h]
        if j == i:
            st = jnp.where(causal, st, -jnp.inf)
        vt = vt_ref[0, h // 2, j, (h % 2) * MLA_V:(h % 2 + 1) * MLA_V, :]
        vt = jnp.concatenate([vt, ones], axis=0)
        m_new = jnp.maximum(m, jnp.max(st, axis=0, keepdims=True))
        alpha = jnp.exp2(m - m_new)
        pt = jnp.exp2(st - m_new).astype(BF16)
        acc_new = alpha * acc + jnp.dot(vt, pt, preferred_element_type=F32)
        state[h] = (m_new, acc_new)
        if j == i and h == len(heads) - 1:
            for p in range(len(heads) // 2):
                pair = [state[2 * p][1], state[2 * p + 1][1]]
                out_t = jnp.concatenate([acc[:MLA_V] / acc[MLA_V:MLA_V + 1] for acc in pair], axis=0)
                o_ref[0, i * tq:(i + 1) * tq, p * LANES:(p + 1) * LANES] = out_t.T.astype(BF16)


def _mla_call(qf, kf, vt, w_up, w_down, w_o, tq, hps):
    b, s, _ = qf.shape
    pairs = MLA_HEADS // 2
    assert vt.shape == (b, pairs, s // tq, LANES, tq) and pairs % hps == 0
    seq_cols = lambda bi, g: (bi, 0, g)
    n_steps = b * (pairs // hps)
    step_rows = lambda bi, g: (bi * (pairs // hps) + g, 0)
    w_specs = [pl.BlockSpec((w.shape[0] // n_steps, w.shape[1]), step_rows) for w in (w_up, w_down, w_o)]
    assert w_o.shape[0] // n_steps == SWA_HEAD_DIM and MLA_V_COLS == SWA_Q_COLS

    def wo_dest(bi, g):
        blk = bi * (pairs // hps) + g
        swa = blk - MLA_HEADS
        moved = MLA_HEADS + (swa % SWA_GROUP) * SWA_KV_HEADS + swa // SWA_GROUP
        return jnp.where(blk < MLA_HEADS, blk, moved), 0

    w_out_specs = w_specs[:2] + [pl.BlockSpec((SWA_HEAD_DIM, w_o.shape[1]), wo_dest)]
    return pl.pallas_call(
        functools.partial(_mla_kernel, tq=tq),
        grid=(b, pairs // hps),
        in_specs=[
            pl.BlockSpec((1, s, 2 * hps * LANES), seq_cols),
            pl.BlockSpec((1, s, 2 * hps * LANES), seq_cols),
            pl.BlockSpec((1, hps, s // tq, LANES, tq), lambda bi, g: (bi, g, 0, 0, 0)),
        ] + w_specs,
        out_specs=[pl.BlockSpec((1, s, hps * LANES), seq_cols)] + w_out_specs,
        out_shape=[jax.ShapeDtypeStruct((b, s, MLA_V_COLS), BF16)]
        + [jax.ShapeDtypeStruct(w.shape, BF16) for w in (w_up, w_down, w_o)],
        compiler_params=pltpu.CompilerParams(
            dimension_semantics=("parallel", "parallel"), vmem_limit_bytes=VMEM_LIMIT_BYTES),
        name="mla_attn",
    )(qf, kf, vt, w_up, w_down, w_o)


def _alibi_slope(h):
    return 2.0 ** (-8.0 * (h + 1) / SWA_HEADS)


SWA_STACK = 2
SWA_LOOKAHEAD = 6


def _swa_bias():
    k = np.arange(2 * WINDOW)[:, None]
    q = np.arange(WINDOW)[None, :]
    dist = q + WINDOW - k
    valid = (dist >= 0) & (dist < WINDOW)
    out = np.empty((SWA_KV_HEADS, 2 * WINDOW, SWA_GROUP * WINDOW), np.float32)
    for kv in range(SWA_KV_HEADS):
        for g in range(SWA_GROUP):
            slope = np.float32(_alibi_slope(kv * SWA_GROUP + g))
            out[kv, :, g * WINDOW:(g + 1) * WINDOW] = np.where(valid, -slope * dist.astype(np.float32) * LOG2E, -np.inf)
    return out


def _swa_tile_chains(sink_ref, bias_ref, q_ref, k_prev, k_ref, vt_prev, vt_ref, gone, store):
    nsub = q_ref.shape[2] // WINDOW
    ones = jnp.ones((BF16_SUBLANES, 2 * WINDOW), BF16)
    width = SWA_STACK * WINDOW
    chains = [(c, g0, kv) for c in range(nsub) for g0 in range(0, SWA_GROUP, SWA_STACK)
              for kv in range(SWA_KV_HEADS)]

    def scores(c, g0, kv):
        cols = slice(kv * LANES, (kv + 1) * LANES)
        if c == 0:
            k_win = jnp.concatenate([k_prev(cols), k_ref[0, :WINDOW, cols]], axis=0)
        else:
            k_win = k_ref[0, (c - 1) * WINDOW:(c + 1) * WINDOW, cols]
        q_st = q_ref[0, g0:g0 + SWA_STACK, c * WINDOW:(c + 1) * WINDOW, :].reshape(width, LANES)
        st = lax.dot_general(k_win, q_st, (((1,), (1,)), ((), ())), preferred_element_type=F32)
        return st + bias_ref[kv, :, g0 * WINDOW:g0 * WINDOW + width]

    outs = {}
    pending = [scores(*ch) for ch in chains[:SWA_LOOKAHEAD]]
    for n, (c, g0, kv) in enumerate(chains):
        st = pending.pop(0)
        if n + SWA_LOOKAHEAD < len(chains):
            pending.append(scores(*chains[n + SWA_LOOKAHEAD]))
        rows = slice(kv * SWA_HEAD_DIM, (kv + 1) * SWA_HEAD_DIM)
        if c == 0:
            vt = jnp.concatenate([vt_prev(rows), vt_ref[0, 0, rows, :]], axis=1)
            st = jnp.concatenate([st[:WINDOW] + gone, st[WINDOW:]], axis=0)
        else:
            vt = jnp.concatenate([vt_ref[0, c - 1, rows, :], vt_ref[0, c, rows, :]], axis=1)
        sink = sink_ref[kv:kv + 1, g0 * WINDOW:g0 * WINDOW + width]
        m = jnp.maximum(jnp.max(st, axis=0, keepdims=True), sink)
        pt = jnp.exp2(st - m).astype(BF16)
        res = jnp.dot(jnp.concatenate([vt, ones], axis=0), pt, preferred_element_type=F32)
        denom = res[SWA_HEAD_DIM:SWA_HEAD_DIM + 1] + jnp.exp2(sink - m)
        outs[kv] = res[:SWA_HEAD_DIM] / denom
        if kv == SWA_KV_HEADS - 1:
            for gg in range(SWA_STACK):
                gs = slice(gg * WINDOW, (gg + 1) * WINDOW)
                out_t = jnp.concatenate([outs[0][:, gs], outs[1][:, gs]], axis=0)
                store(c, g0 + gg, out_t.T.astype(BF16))
        yield


def _post_kernel(x_ref, mla_ref, mod_ref, wo_ref, gmlp_ref, wup_ref, wdown_ref, gfin_ref,
                 sink_ref, bias_ref, qn_ref, kpn_ref, kn_ref, vtpn_ref, vtn_ref, q0_ref, k0_ref, vt0_ref,
                 o_ref, swa_ref, *, ts, ff_chunk):
    step = pl.program_id(0) * pl.num_programs(1) + pl.program_id(1)
    n_steps = pl.num_programs(0) * pl.num_programs(1)
    n_chunks = D_FF // ff_chunk

    def store(c, g, tile):
        swa_ref[c * WINDOW:(c + 1) * WINDOW, g * LANES:(g + 1) * LANES] = tile

    @pl.when(step == 0)
    def _():
        for _ in _swa_tile_chains(sink_ref, bias_ref, q0_ref, lambda cols: k0_ref[0, :WINDOW, cols], k0_ref,
                                  lambda rows: vt0_ref[0, 0, rows, :], vt0_ref, -jnp.inf, store):
            pass

    mod = mod_ref[:, 0, 0, :]
    g1, sh2, sc2, g2 = mod[2:3], mod[3:4], mod[4:5], mod[5:6]
    woa = wo_ref[:MLA_V_COLS, :]
    wob = wo_ref[MLA_V_COLS:, :]
    tiles = [slice(r0, r0 + ts) for r0 in range(0, x_ref.shape[1], ts)]
    mid = []
    for rows in tiles:
        attn = (jnp.dot(mla_ref[0, rows], woa, preferred_element_type=F32)
                + jnp.dot(swa_ref[rows, :], wob, preferred_element_type=F32))
        x1 = x_ref[0, rows] + g1 * attn
        h2 = (_rms(x1) * gmlp_ref[...] * (1.0 + sc2) + sh2).astype(BF16)
        mid.append((x1, h2))
    nxt = jnp.minimum(step + 1, n_steps - 1) % pl.num_programs(1)
    gone = jnp.where(nxt == 0, -jnp.inf, 0.0)
    chains = _swa_tile_chains(sink_ref, bias_ref, qn_ref, lambda cols: kpn_ref[0, :, cols], kn_ref,
                              lambda rows: vtpn_ref[0, 0, rows, :], vtn_ref, gone, store)
    n_slots = 2 * n_chunks * len(tiles)
    n_attn = (qn_ref.shape[2] // WINDOW) * (SWA_GROUP // SWA_STACK) * SWA_KV_HEADS
    emitted = 0
    accs = [jnp.zeros_like(x1) for x1, _ in mid]
    for c in range(n_chunks):
        cs = slice(c * ff_chunk, (c + 1) * ff_chunk)
        w_up_c = wup_ref[:, cs]
        w_down_c = wdown_ref[cs, :]
        us = []
        for t in range(2 * len(mid)):
            if t < len(mid):
                us.append(jnp.maximum(jnp.dot(mid[t][1], w_up_c, preferred_element_type=F32), 0.0))
            else:
                u = us[t - len(mid)]
                accs[t - len(mid)] = accs[t - len(mid)] + jnp.dot((u * u).astype(BF16), w_down_c,
                                                                  preferred_element_type=F32)
            slot = c * 2 * len(mid) + t + 1
            while emitted < slot * n_attn // n_slots:
                next(chains)
                emitted += 1
    for rows, (x1, _), acc in zip(tiles, mid, accs):
        x2 = x1 + g2 * acc
        o_ref[0, rows] = _rms(x2) * gfin_ref[...]


def _post_call(x, mla, mod, w_o, gmlp, wup, wdown, gfin, sinks, sq, sk, svt, tm, ts, ff_chunk):
    b, s, d = x.shape
    nt = s // tm
    nsub = tm // WINDOW
    bias = jnp.asarray(_swa_bias())
    sink_rows = jnp.repeat(sinks.reshape(SWA_KV_HEADS, SWA_GROUP), WINDOW, axis=1) * LOG2E
    row = lambda bi, i: (bi, i, 0)

    def nxt(bi, i):
        n = jnp.minimum(bi * nt + i + 1, b * nt - 1)
        return n // nt, n % nt

    def nxt_prev_blk(bi, i):
        bn, tn = nxt(bi, i)
        return bn, jnp.maximum(tn * nsub - 1, 0)

    return pl.pallas_call(
        functools.partial(_post_kernel, ts=ts, ff_chunk=ff_chunk),
        grid=(b, nt),
        in_specs=[
            pl.BlockSpec((1, tm, d), row),
            pl.BlockSpec((1, tm, MLA_V_COLS), row),
            _mod_spec(),
            _const_spec(w_o.shape),
            _const_spec((1, d)),
            _const_spec(wup.shape),
            _const_spec(wdown.shape),
            _const_spec((1, d)),
            _const_spec(sink_rows.shape),
            _const_spec(bias.shape),
            pl.BlockSpec((1, SWA_GROUP, tm, LANES), lambda bi, i: (nxt(bi, i)[0], 0, nxt(bi, i)[1], 0)),
            pl.BlockSpec((1, WINDOW, SWA_KV_HEADS * LANES), lambda bi, i: nxt_prev_blk(bi, i) + (0,)),
            pl.BlockSpec((1, tm, SWA_KV_HEADS * LANES), lambda bi, i: nxt(bi, i) + (0,)),
            pl.BlockSpec((1, 1, SWA_KV_COLS, WINDOW), lambda bi, i: nxt_prev_blk(bi, i) + (0, 0)),
            pl.BlockSpec((1, nsub, SWA_KV_COLS, WINDOW), lambda bi, i: nxt(bi, i) + (0, 0)),
            pl.BlockSpec((1, SWA_GROUP, tm, LANES), lambda bi, i: (0, 0, 0, 0), pipeline_mode=pl.Buffered(1)),
            pl.BlockSpec((1, tm, SWA_KV_HEADS * LANES), lambda bi, i: (0, 0, 0), pipeline_mode=pl.Buffered(1)),
            pl.BlockSpec((1, nsub, SWA_KV_COLS, WINDOW), lambda bi, i: (0, 0, 0, 0), pipeline_mode=pl.Buffered(1)),
        ],
        out_specs=pl.BlockSpec((1, tm, d), row),
        out_shape=jax.ShapeDtypeStruct((b, s, d), F32),
        scratch_shapes=[pltpu.VMEM((tm, SWA_Q_COLS), BF16)],
        compiler_params=pltpu.CompilerParams(
            dimension_semantics=("arbitrary", "arbitrary"), vmem_limit_bytes=VMEM_LIMIT_BYTES),
        name="post_attn",
    )(x, mla, mod, w_o, gmlp, wup, wdown, gfin, sink_rows, bias, sq, sk, sk, svt, svt, sq, sk, svt)


def kernel(x, c, w_ada, b_ada, norm_mix_g, w_in, g_qa, w_qb, g_kva, w_kvb, sinks,
           w_o, norm_mlp_g, w_up, w_down, final_g):
    depth = w_ada.shape[0]
    b = x.shape[0]
    assert depth == 1, "the final rmsnorm is fused into the single layer's post-attention call"
    for l in range(depth):
        mod = _ada_call(c, w_ada[l], b_ada[l])
        qf, kf, vt, sq, sk, svt = _pre_call(
            x, mod, norm_mix_g[l][None], w_in[l].T, g_qa[l][None],
            w_qb[l].transpose(1, 2, 0).reshape(-1, Q_LORA), g_kva[l][None], w_kvb[l].reshape(KV_LORA, -1),
            tm=1024, ts=256, tk=256)
        mla, w_up_bf, w_down_bf, w_o_bf = _mla_call(qf, kf, vt, w_up[l], w_down[l], w_o[l], tq=256, hps=2)
        x = _post_call(x, mla, mod, w_o_bf, norm_mlp_g[l][None], w_up_bf, w_down_bf, final_g[None],
                       sinks[l], sq, sk, svt, tm=1024, ts=256, ff_chunk=1024)
    return x
```

```python
import functools

import numpy as np
import jax
import jax.numpy as jnp
from jax import lax
from jax.experimental import pallas as pl
from jax.experimental.pallas import tpu as pltpu

D_MODEL = 1024
SEQ = 2048
MLA_HEADS = 8
MLA_NOPE = 64
MLA_ROPE = 32
MLA_V = 64
Q_LORA = 384
KV_LORA = 256
ROPE_THETA = 10000.0
SWA_HEADS = 8
SWA_KV_HEADS = 2
SWA_GROUP = SWA_HEADS // SWA_KV_HEADS
SWA_HEAD_DIM = 64
WINDOW = 128
D_FF = 4 * D_MODEL
EPS = 1e-6
N_MOD = 6

LANES = 128
HALF = LANES // 2
BF16_SUBLANES = 16
MLA_QK_COLS = MLA_HEADS * LANES
MLA_V_COLS = MLA_HEADS * MLA_V
SWA_Q_COLS = SWA_HEADS * SWA_HEAD_DIM
SWA_KV_COLS = SWA_KV_HEADS * SWA_HEAD_DIM
C_QLAT = 0
C_KVLAT = C_QLAT + Q_LORA
C_KPE = C_KVLAT + KV_LORA
C_SQ = C_KPE + LANES
C_SK = C_SQ + SWA_Q_COLS
C_SV = C_SK + SWA_KV_COLS
IN_COLS_PAD = C_SV + SWA_KV_COLS

VMEM_LIMIT_BYTES = 62 * 1024 * 1024

LOG2E = float(np.log2(np.e))
MLA_Q_SCALE = (MLA_NOPE + MLA_ROPE) ** -0.5 * LOG2E
SWA_Q_SCALE = SWA_HEAD_DIM ** -0.5 * LOG2E

BF16 = jnp.bfloat16
F32 = jnp.float32


def _const_spec(shape):
    nd = len(shape)
    return pl.BlockSpec(shape, lambda *_: (0,) * nd, pipeline_mode=pl.Buffered(1))


def _rms(x):
    return x * lax.rsqrt(jnp.mean(x * x, axis=-1, keepdims=True) + EPS)


def _ada_kernel(c_ref, w_ref, b_ref, o_ref):
    c = c_ref[...]
    s = c / (1.0 + jnp.exp(-c))
    o_ref[0, :, 0, :] = jnp.dot(s.astype(BF16), w_ref[...].astype(BF16), preferred_element_type=F32) + b_ref[...]


def _ada_call(c, w_ada, b_ada):
    b = c.shape[0]
    n = w_ada.shape[1]
    tn = D_MODEL
    return pl.pallas_call(
        _ada_kernel,
        grid=(n // tn,),
        in_specs=[
            pl.BlockSpec((b, D_MODEL), lambda j: (0, 0)),
            pl.BlockSpec((D_MODEL, tn), lambda j: (0, j)),
            pl.BlockSpec((1, tn), lambda j: (0, j)),
        ],
        out_specs=pl.BlockSpec((1, b, 1, tn), lambda j: (j, 0, 0, 0)),
        out_shape=jax.ShapeDtypeStruct((n // tn, b, 1, tn), F32),
        compiler_params=pltpu.CompilerParams(dimension_semantics=("parallel",)),
        name="ada_mod",
    )(c, w_ada, b_ada.reshape(1, n))


def _mod_spec():
    return pl.BlockSpec((N_MOD, 1, 1, D_MODEL), lambda bi, i: (0, bi, 0, 0))


def _rope_group(xg, keep, swap):
    up = pltpu.roll(xg, LANES - MLA_ROPE // 2, 1)
    return xg * keep + up * swap


def _relayout_weights(w_in_ref, w_qb_ref, w_kvb_ref, win_ref, wq_ref, wk_ref, wv_ref):
    o2 = Q_LORA + KV_LORA
    o3 = o2 + MLA_ROPE
    o4 = o3 + SWA_Q_COLS
    half = MLA_ROPE // 2
    pad = LANES - MLA_NOPE - MLA_ROPE - half
    d_in = w_in_ref.shape[1]
    kpe = [jnp.zeros((MLA_NOPE, d_in), F32), w_in_ref[o2:o3, :], w_in_ref[o2:o2 + half, :],
           jnp.zeros((pad, d_in), F32)]
    sq = [w_in_ref[o3 + (kv * SWA_GROUP + g) * SWA_HEAD_DIM:o3 + (kv * SWA_GROUP + g + 1) * SWA_HEAD_DIM, :]
          for g in range(SWA_GROUP) for kv in range(SWA_KV_HEADS)]
    win_t = jnp.concatenate([w_in_ref[:o2, :]] + kpe + sq + [w_in_ref[o4:, :]], axis=0)
    chunk = 2 * LANES
    for c0 in range(0, IN_COLS_PAD, chunk):
        win_ref[:, c0:c0 + chunk] = win_t[c0:c0 + chunk, :].T.astype(BF16)
    hd_rows = MLA_NOPE + MLA_ROPE
    parts = []
    for hd in range(MLA_HEADS):
        r0 = hd * hd_rows
        parts += [w_qb_ref[r0:r0 + hd_rows, :], w_qb_ref[r0 + MLA_NOPE:r0 + MLA_NOPE + half, :],
                  jnp.zeros((pad, w_qb_ref.shape[1]), F32)]
    wq_ref[...] = jnp.concatenate(parts, axis=0).astype(BF16)
    wkvb = w_kvb_ref[...]
    wk_ref[...] = jnp.concatenate([wkvb[:, hd * LANES:hd * LANES + MLA_NOPE] for hd in range(MLA_HEADS)],
                                  axis=1).astype(BF16)
    wv = jnp.concatenate([wkvb[:, hd * LANES + MLA_NOPE:(hd + 1) * LANES] for hd in range(MLA_HEADS)], axis=1)
    wv_ref[...] = wv.T.astype(BF16)


def _pre_kernel(x_ref, mod_ref, gmix_ref, w_in_ref, gqa_ref, w_qb_ref, gkva_ref, w_kvb_ref,
                ropeq_ref, ropek_ref,
                qf_ref, kf_ref, v_ref, sq_ref, sk_ref, sv_ref,
                win_ref, wq_ref, wk_ref, wv_ref, *, ts):
    @pl.when((pl.program_id(0) == 0) & (pl.program_id(1) == 0))
    def _():
        _relayout_weights(w_in_ref, w_qb_ref, w_kvb_ref, win_ref, wq_ref, wk_ref, wv_ref)

    for r0 in range(0, x_ref.shape[1], ts):
        _pre_rows(r0, ts, x_ref, mod_ref, gmix_ref, win_ref, gqa_ref, wq_ref, gkva_ref, wk_ref, wv_ref,
                  ropeq_ref, ropek_ref, qf_ref, kf_ref, v_ref, sq_ref, sk_ref, sv_ref)


def _pre_rows(r0, ts, x_ref, mod_ref, gmix_ref, win_ref, gqa_ref, wq_ref, gkva_ref, wk_ref, wv_ref,
              ropeq_ref, ropek_ref, qf_ref, kf_ref, v_ref, sq_ref, sk_ref, sv_ref):
    rows = slice(r0, r0 + ts)
    x = x_ref[0, rows]
    mod = mod_ref[:, 0, 0, :]
    sh1 = mod[0:1]
    sc1 = mod[1:2]
    h = (_rms(x) * gmix_ref[...] * (1.0 + sc1) + sh1).astype(BF16)
    proj = jnp.dot(h, win_ref[...], preferred_element_type=F32)

    for g in range(SWA_GROUP):
        sq_ref[0, g, rows] = (proj[:, C_SQ + g * LANES:C_SQ + (g + 1) * LANES] * SWA_Q_SCALE).astype(BF16)
    swa_k = proj[:, C_SK:C_SV]
    low = lax.broadcasted_iota(jnp.int32, swa_k.shape, 1) < HALF
    sk_ref[0, rows, :LANES] = jnp.where(low, swa_k, 0.0).astype(BF16)
    sk_ref[0, rows, LANES:] = jnp.where(low, 0.0, swa_k).astype(BF16)
    svt = proj[:, C_SV:IN_COLS_PAD].T.astype(BF16)
    for jb in range(ts // WINDOW):
        sv_ref[0, r0 // WINDOW + jb] = svt[:, jb * WINDOW:(jb + 1) * WINDOW]

    qn = (_rms(proj[:, C_QLAT:C_KVLAT]) * gqa_ref[...]).astype(BF16)
    kvn = (_rms(proj[:, C_KVLAT:C_KPE]) * gkva_ref[...]).astype(BF16)
    qt = lax.dot_general(wq_ref[...], qn, (((1,), (1,)), ((), ())), preferred_element_type=F32)
    kn = jnp.dot(kvn, wk_ref[...], preferred_element_type=F32)
    vt = lax.dot_general(wv_ref[...], kvn, (((1,), (1,)), ((), ())), preferred_element_type=F32).astype(BF16)
    tk = v_ref.shape[-1]
    for hp in range(MLA_HEADS // 2):
        for jb in range(ts // tk):
            v_ref[0, hp, r0 // tk + jb] = vt[hp * LANES:(hp + 1) * LANES, jb * tk:(jb + 1) * tk]

    q_keep, q_swap = ropeq_ref[0, :, rows], ropeq_ref[1, :, rows]
    half = MLA_ROPE // 2
    kpe = _rope_group(proj[:, C_KPE:C_SQ], ropek_ref[0, rows], ropek_ref[1, rows])
    nope_lanes = lax.broadcasted_iota(jnp.int32, kpe.shape, 1) < MLA_NOPE
    for hd in range(MLA_HEADS):
        sl = slice(hd * LANES, (hd + 1) * LANES)
        xg = qt[sl, :]
        up = jnp.concatenate([xg[half:], xg[:half]], axis=0)
        qf_ref[0, sl, rows] = (xg * q_keep + up * q_swap).astype(BF16)
        kn_pair = kn[:, (hd // 2) * LANES:(hd // 2 + 1) * LANES]
        if hd % 2:
            kn_pair = pltpu.roll(kn_pair, HALF, 1)
        kf_ref[0, rows, sl] = jnp.where(nope_lanes, kn_pair, kpe).astype(BF16)


def _rope_tables(scale_q):
    half = MLA_ROPE // 2
    freqs = ROPE_THETA ** (-np.arange(0, MLA_ROPE, 2, dtype=np.float64) / MLA_ROPE)
    ang = np.arange(SEQ, dtype=np.float64)[:, None] * freqs[None, :]
    cos, sin = np.cos(ang), np.sin(ang)
    keep = np.zeros((SEQ, LANES))
    swap = np.zeros((SEQ, LANES))
    keep[:, :MLA_NOPE] = 1.0
    keep[:, MLA_NOPE:MLA_NOPE + half] = cos
    keep[:, MLA_NOPE + half:MLA_NOPE + MLA_ROPE] = cos
    swap[:, MLA_NOPE:MLA_NOPE + half] = -sin
    swap[:, MLA_NOPE + half:MLA_NOPE + MLA_ROPE] = sin
    return (np.stack([keep, swap]) * scale_q).astype(np.float32)


def _pre_call(x, mod, gmix, w_in, gqa, w_qb, gkva, w_kvb, tm, ts, tk):
    b, s, d = x.shape
    pairs = MLA_HEADS // 2
    ropeq = jnp.asarray(_rope_tables(MLA_Q_SCALE).transpose(0, 2, 1))
    ropek = jnp.asarray(_rope_tables(1.0))
    row = lambda bi, i: (bi, i, 0)
    out_shapes = (
        jax.ShapeDtypeStruct((b, MLA_QK_COLS, s), BF16),
        jax.ShapeDtypeStruct((b, s, MLA_QK_COLS), BF16),
        jax.ShapeDtypeStruct((b, pairs, s // tk, LANES, tk), BF16),
        jax.ShapeDtypeStruct((b, SWA_GROUP, s, LANES), BF16),
        jax.ShapeDtypeStruct((b, s, SWA_KV_HEADS * LANES), BF16),
        jax.ShapeDtypeStruct((b, s // WINDOW, SWA_KV_COLS, WINDOW), BF16),
    )
    out_specs = [
        pl.BlockSpec((1, MLA_QK_COLS, tm), lambda bi, i: (bi, 0, i)),
        pl.BlockSpec((1, tm, MLA_QK_COLS), row),
        pl.BlockSpec((1, pairs, tm // tk, LANES, tk), lambda bi, i: (bi, 0, i, 0, 0)),
        pl.BlockSpec((1, SWA_GROUP, tm, LANES), lambda bi, i: (bi, 0, i, 0)),
        pl.BlockSpec((1, tm, SWA_KV_HEADS * LANES), row),
        pl.BlockSpec((1, tm // WINDOW, SWA_KV_COLS, WINDOW), lambda bi, i: (bi, i, 0, 0)),
    ]
    return pl.pallas_call(
        functools.partial(_pre_kernel, ts=ts),
        grid=(b, s // tm),
        in_specs=[
            pl.BlockSpec((1, tm, d), row),
            _mod_spec(),
            _const_spec((1, d)),
            _const_spec(w_in.shape),
            _const_spec((1, Q_LORA)),
            _const_spec(w_qb.shape),
            _const_spec((1, KV_LORA)),
            _const_spec(w_kvb.shape),
            pl.BlockSpec((2, LANES, tm), lambda bi, i: (0, 0, i)),
            pl.BlockSpec((2, tm, LANES), lambda bi, i: (0, i, 0)),
        ],
        out_specs=tuple(out_specs),
        out_shape=out_shapes,
        scratch_shapes=[
            pltpu.VMEM((d, IN_COLS_PAD), BF16),
            pltpu.VMEM((MLA_QK_COLS, Q_LORA), BF16),
            pltpu.VMEM((KV_LORA, MLA_HEADS * MLA_NOPE), BF16),
            pltpu.VMEM((MLA_V_COLS, KV_LORA), BF16),
        ],
        compiler_params=pltpu.CompilerParams(
            dimension_semantics=("arbitrary", "arbitrary"), vmem_limit_bytes=VMEM_LIMIT_BYTES),
        name="pre_attn",
    )(x, mod, gmix, w_in, gqa, w_qb, gkva, w_kvb, ropeq, ropek)


MLA_LOOKAHEAD = 6


def _mla_kernel(q_ref, k_ref, vt_ref, wup_ref, wdown_ref, o_ref, wup_bf_ref, wdown_bf_ref, *, tq):
    wup_bf_ref[...] = wup_ref[...].astype(BF16)
    wdown_bf_ref[...] = wdown_ref[...].astype(BF16)
    ki = lax.broadcasted_iota(jnp.int32, (tq, tq), 0)
    qi = lax.broadcasted_iota(jnp.int32, (tq, tq), 1)
    causal = ki <= qi
    heads = range(q_ref.shape[1] // LANES)
    chains = [(i, j, h) for i in range(q_ref.shape[2] // tq) for j in range(i + 1) for h in heads]

    def scores(i, j, h):
        cols = slice(h * LANES, (h + 1) * LANES)
        qt = q_ref[0, cols, i * tq:(i + 1) * tq]
        k = k_ref[0, j * tq:(j + 1) * tq, cols]
        return jnp.dot(k, qt, preferred_element_type=F32)

    ones = jnp.ones((BF16_SUBLANES, tq), BF16)
    pending = [scores(*ch) for ch in chains[:MLA_LOOKAHEAD]]
    for n, (i, j, h) in enumerate(chains):
        st = pending.pop(0)
        if n + MLA_LOOKAHEAD < len(chains):
            pending.append(scores(*chains[n + MLA_LOOKAHEAD]))
        if j == 0 and h == 0:
            state = [(jnp.full((1, tq), -jnp.inf, F32), jnp.zeros((MLA_V + BF16_SUBLANES, tq), F32))
                     for _ in heads]
        m, acc = state[h]
        if j == i:
            st = jnp.where(causal, st, -jnp.inf)
        vt = vt_ref[0, h // 2, j, (h % 2) * MLA_V:(h % 2 + 1) * MLA_V, :]
        vt = jnp.concatenate([vt, ones], axis=0)
        m_new = jnp.maximum(m, jnp.max(st, axis=0, keepdims=True))
        alpha = jnp.exp2(m - m_new)
        pt = jnp.exp2(st - m_new).astype(BF16)
        acc_new = alpha * acc + jnp.dot(vt, pt, preferred_element_type=F32)
        state[h] = (m_new, acc_new)
        if j == i and h == len(heads) - 1:
            for p in range(len(heads) // 2):
                pair = [state[2 * p][1], state[2 * p + 1][1]]
                out_t = jnp.concatenate([acc[:MLA_V] / acc[MLA_V:MLA_V + 1] for acc in pair], axis=0)
                o_ref[0, i * tq:(i + 1) * tq, p * LANES:(p + 1) * LANES] = out_t.T.astype(BF16)


def _mla_call(qf, kf, vt, w_up, w_down, tq, hps):
    b, s, _ = kf.shape
    pairs = MLA_HEADS // 2
    assert vt.shape == (b, pairs, s // tq, LANES, tq) and pairs % hps == 0 and qf.shape == (b, MLA_QK_COLS, s)
    seq_cols = lambda bi, g: (bi, 0, g)
    n_steps = b * (pairs // hps)
    step_rows = lambda bi, g: (bi * (pairs // hps) + g, 0)
    w_specs = [pl.BlockSpec((w.shape[0] // n_steps, w.shape[1]), step_rows) for w in (w_up, w_down)]
    return pl.pallas_call(
        functools.partial(_mla_kernel, tq=tq),
        grid=(b, pairs // hps),
        in_specs=[
            pl.BlockSpec((1, 2 * hps * LANES, s), lambda bi, g: (bi, g, 0)),
            pl.BlockSpec((1, s, 2 * hps * LANES), seq_cols),
            pl.BlockSpec((1, hps, s // tq, LANES, tq), lambda bi, g: (bi, g, 0, 0, 0)),
        ] + w_specs,
        out_specs=[pl.BlockSpec((1, s, hps * LANES), seq_cols)] + w_specs,
        out_shape=[jax.ShapeDtypeStruct((b, s, MLA_V_COLS), BF16),
                   jax.ShapeDtypeStruct(w_up.shape, BF16), jax.ShapeDtypeStruct(w_down.shape, BF16)],
        compiler_params=pltpu.CompilerParams(
            dimension_semantics=("parallel", "parallel"), vmem_limit_bytes=VMEM_LIMIT_BYTES),
        name="mla_attn",
    )(qf, kf, vt, w_up, w_down)


def _alibi_slope(h):
    return 2.0 ** (-8.0 * (h + 1) / SWA_HEADS)


SWA_STACK = 2
SWA_LOOKAHEAD = 6


def _swa_bias():
    k = np.arange(2 * WINDOW)[:, None]
    q = np.arange(WINDOW)[None, :]
    dist = q + WINDOW - k
    valid = (dist >= 0) & (dist < WINDOW)
    out = np.empty((SWA_KV_HEADS, 2 * WINDOW, SWA_GROUP * WINDOW), np.float32)
    for kv in range(SWA_KV_HEADS):
        for g in range(SWA_GROUP):
            slope = np.float32(_alibi_slope(kv * SWA_GROUP + g))
            out[kv, :, g * WINDOW:(g + 1) * WINDOW] = np.where(valid, -slope * dist.astype(np.float32) * LOG2E, -np.inf)
    return out


def _swa_tile_chains(sink_ref, bias_ref, q_ref, k_prev, k_ref, vt_prev, vt_ref, gone, store):
    nsub = q_ref.shape[2] // WINDOW
    ones = jnp.ones((BF16_SUBLANES, 2 * WINDOW), BF16)
    width = SWA_STACK * WINDOW
    chains = [(c, g0, kv) for c in range(nsub) for g0 in range(0, SWA_GROUP, SWA_STACK)
              for kv in range(SWA_KV_HEADS)]

    def scores(c, g0, kv):
        cols = slice(kv * LANES, (kv + 1) * LANES)
        if c == 0:
            k_win = jnp.concatenate([k_prev(cols), k_ref[0, :WINDOW, cols]], axis=0)
        else:
            k_win = k_ref[0, (c - 1) * WINDOW:(c + 1) * WINDOW, cols]
        q_st = q_ref[0, g0:g0 + SWA_STACK, c * WINDOW:(c + 1) * WINDOW, :].reshape(width, LANES)
        st = lax.dot_general(k_win, q_st, (((1,), (1,)), ((), ())), preferred_element_type=F32)
        return st + bias_ref[kv, :, g0 * WINDOW:g0 * WINDOW + width]

    outs = {}
    pending = [scores(*ch) for ch in chains[:SWA_LOOKAHEAD]]
    for n, (c, g0, kv) in enumerate(chains):
        st = pending.pop(0)
        if n + SWA_LOOKAHEAD < len(chains):
            pending.append(scores(*chains[n + SWA_LOOKAHEAD]))
        rows = slice(kv * SWA_HEAD_DIM, (kv + 1) * SWA_HEAD_DIM)
        if c == 0:
            vt = jnp.concatenate([vt_prev(rows), vt_ref[0, 0, rows, :]], axis=1)
            st = jnp.concatenate([st[:WINDOW] + gone, st[WINDOW:]], axis=0)
        else:
            vt = jnp.concatenate([vt_ref[0, c - 1, rows, :], vt_ref[0, c, rows, :]], axis=1)
        sink = sink_ref[kv:kv + 1, g0 * WINDOW:g0 * WINDOW + width]
        m = jnp.maximum(jnp.max(st, axis=0, keepdims=True), sink)
        pt = jnp.exp2(st - m).astype(BF16)
        res = jnp.dot(jnp.concatenate([vt, ones], axis=0), pt, preferred_element_type=F32)
        denom = res[SWA_HEAD_DIM:SWA_HEAD_DIM + 1] + jnp.exp2(sink - m)
        outs[kv] = res[:SWA_HEAD_DIM] / denom
        if kv == SWA_KV_HEADS - 1:
            for gg in range(SWA_STACK):
                gs = slice(gg * WINDOW, (gg + 1) * WINDOW)
                out_t = jnp.concatenate([outs[0][:, gs], outs[1][:, gs]], axis=0)
                store(c, g0 + gg, out_t.T.astype(BF16))
        yield


def _post_kernel(x_ref, mla_ref, mod_ref, wo_ref, gmlp_ref, wup_ref, wdown_ref, gfin_ref,
                 sink_ref, bias_ref, qn_ref, kpn_ref, kn_ref, vtpn_ref, vtn_ref, q0_ref, k0_ref, vt0_ref,
                 o_ref, swa_ref, *, ts, ff_chunk):
    step = pl.program_id(0) * pl.num_programs(1) + pl.program_id(1)
    n_steps = pl.num_programs(0) * pl.num_programs(1)
    n_chunks = D_FF // ff_chunk

    def store(c, g, tile):
        swa_ref[c * WINDOW:(c + 1) * WINDOW, g * LANES:(g + 1) * LANES] = tile

    @pl.when(step == 0)
    def _():
        for _ in _swa_tile_chains(sink_ref, bias_ref, q0_ref, lambda cols: k0_ref[0, :WINDOW, cols], k0_ref,
                                  lambda rows: vt0_ref[0, 0, rows, :], vt0_ref, -jnp.inf, store):
            pass

    mod = mod_ref[:, 0, 0, :]
    g1, sh2, sc2, g2 = mod[2:3], mod[3:4], mod[4:5], mod[5:6]
    woa = wo_ref[:MLA_V_COLS, :].astype(BF16)
    wob = jnp.concatenate(
        [wo_ref[MLA_V_COLS + (kv * SWA_GROUP + g) * SWA_HEAD_DIM:MLA_V_COLS + (kv * SWA_GROUP + g + 1) * SWA_HEAD_DIM, :]
         for g in range(SWA_GROUP) for kv in range(SWA_KV_HEADS)], axis=0).astype(BF16)
    tiles = [slice(r0, r0 + ts) for r0 in range(0, x_ref.shape[1], ts)]
    mid = []
    for rows in tiles:
        attn = (jnp.dot(mla_ref[0, rows], woa, preferred_element_type=F32)
                + jnp.dot(swa_ref[rows, :], wob, preferred_element_type=F32))
        x1 = x_ref[0, rows] + g1 * attn
        h2 = (_rms(x1) * gmlp_ref[...] * (1.0 + sc2) + sh2).astype(BF16)
        mid.append((x1, h2))
    nxt = jnp.minimum(step + 1, n_steps - 1) % pl.num_programs(1)
    gone = jnp.where(nxt == 0, -jnp.inf, 0.0)
    chains = _swa_tile_chains(sink_ref, bias_ref, qn_ref, lambda cols: kpn_ref[0, :, cols], kn_ref,
                              lambda rows: vtpn_ref[0, 0, rows, :], vtn_ref, gone, store)
    n_slots = 2 * n_chunks * len(tiles)
    n_attn = (qn_ref.shape[2] // WINDOW) * (SWA_GROUP // SWA_STACK) * SWA_KV_HEADS
    emitted = 0
    accs = [jnp.zeros_like(x1) for x1, _ in mid]
    for c in range(n_chunks):
        cs = slice(c * ff_chunk, (c + 1) * ff_chunk)
        w_up_c = wup_ref[:, cs]
        w_down_c = wdown_ref[cs, :]
        us = []
        for t in range(2 * len(mid)):
            if t < len(mid):
                us.append(jnp.maximum(jnp.dot(mid[t][1], w_up_c, preferred_element_type=F32), 0.0))
            else:
                u = us[t - len(mid)]
                accs[t - len(mid)] = accs[t - len(mid)] + jnp.dot((u * u).astype(BF16), w_down_c,
                                                                  preferred_element_type=F32)
            slot = c * 2 * len(mid) + t + 1
            while emitted < slot * n_attn // n_slots:
                next(chains)
                emitted += 1
    for rows, (x1, _), acc in zip(tiles, mid, accs):
        x2 = x1 + g2 * acc
        o_ref[0, rows] = _rms(x2) * gfin_ref[...]


def _post_call(x, mla, mod, w_o, gmlp, wup, wdown, gfin, sinks, sq, sk, svt, tm, ts, ff_chunk):
    b, s, d = x.shape
    nt = s // tm
    nsub = tm // WINDOW
    bias = jnp.asarray(_swa_bias())
    sink_rows = jnp.repeat(sinks.reshape(SWA_KV_HEADS, SWA_GROUP), WINDOW, axis=1) * LOG2E
    row = lambda bi, i: (bi, i, 0)

    def nxt(bi, i):
        n = jnp.minimum(bi * nt + i + 1, b * nt - 1)
        return n // nt, n % nt

    def nxt_prev_blk(bi, i):
        bn, tn = nxt(bi, i)
        return bn, jnp.maximum(tn * nsub - 1, 0)

    return pl.pallas_call(
        functools.partial(_post_kernel, ts=ts, ff_chunk=ff_chunk),
        grid=(b, nt),
        in_specs=[
            pl.BlockSpec((1, tm, d), row),
            pl.BlockSpec((1, tm, MLA_V_COLS), row),
            _mod_spec(),
            _const_spec(w_o.shape),
            _const_spec((1, d)),
            _const_spec(wup.shape),
            _const_spec(wdown.shape),
            _const_spec((1, d)),
            _const_spec(sink_rows.shape),
            _const_spec(bias.shape),
            pl.BlockSpec((1, SWA_GROUP, tm, LANES), lambda bi, i: (nxt(bi, i)[0], 0, nxt(bi, i)[1], 0)),
            pl.BlockSpec((1, WINDOW, SWA_KV_HEADS * LANES), lambda bi, i: nxt_prev_blk(bi, i) + (0,)),
            pl.BlockSpec((1, tm, SWA_KV_HEADS * LANES), lambda bi, i: nxt(bi, i) + (0,)),
            pl.BlockSpec((1, 1, SWA_KV_COLS, WINDOW), lambda bi, i: nxt_prev_blk(bi, i) + (0, 0)),
            pl.BlockSpec((1, nsub, SWA_KV_COLS, WINDOW), lambda bi, i: nxt(bi, i) + (0, 0)),
            pl.BlockSpec((1, SWA_GROUP, tm, LANES), lambda bi, i: (0, 0, 0, 0), pipeline_mode=pl.Buffered(1)),
            pl.BlockSpec((1, tm, SWA_KV_HEADS * LANES), lambda bi, i: (0, 0, 0), pipeline_mode=pl.Buffered(1)),
            pl.BlockSpec((1, nsub, SWA_KV_COLS, WINDOW), lambda bi, i: (0, 0, 0, 0), pipeline_mode=pl.Buffered(1)),
        ],
        out_specs=pl.BlockSpec((1, tm, d), row),
        out_shape=jax.ShapeDtypeStruct((b, s, d), F32),
        scratch_shapes=[pltpu.VMEM((tm, SWA_Q_COLS), BF16)],
        compiler_params=pltpu.CompilerParams(
            dimension_semantics=("arbitrary", "arbitrary"), vmem_limit_bytes=VMEM_LIMIT_BYTES),
        name="post_attn",
    )(x, mla, mod, w_o, gmlp, wup, wdown, gfin, sink_rows, bias, sq, sk, sk, svt, svt, sq, sk, svt)


def kernel(x, c, w_ada, b_ada, norm_mix_g, w_in, g_qa, w_qb, g_kva, w_kvb, sinks,
           w_o, norm_mlp_g, w_up, w_down, final_g):
    depth = w_ada.shape[0]
    b = x.shape[0]
    assert depth == 1, "the final rmsnorm is fused into the single layer's post-attention call"
    for l in range(depth):
        mod = _ada_call(c, w_ada[l], b_ada[l])
        qf, kf, vt, sq, sk, svt = _pre_call(
            x, mod, norm_mix_g[l][None], w_in[l].T, g_qa[l][None],
            w_qb[l].transpose(1, 2, 0).reshape(-1, Q_LORA), g_kva[l][None], w_kvb[l].reshape(KV_LORA, -1),
            tm=1024, ts=256, tk=256)
        mla, w_up_bf, w_down_bf = _mla_call(qf, kf, vt, w_up[l], w_down[l], tq=256, hps=2)
        x = _post_call(x, mla, mod, w_o[l], norm_mlp_g[l][None], w_up_bf, w_down_bf, final_g[None],
                       sinks[l], sq, sk, svt, tm=1024, ts=256, ff_chunk=1024)
    return x
```

```python
import functools

import numpy as np
import jax
import jax.numpy as jnp
from jax import lax
from jax.experimental import pallas as pl
from jax.experimental.pallas import tpu as pltpu

D_MODEL = 1024
SEQ = 2048
MLA_HEADS = 8
MLA_NOPE = 64
MLA_ROPE = 32
MLA_V = 64
Q_LORA = 384
KV_LORA = 256
ROPE_THETA = 10000.0
SWA_HEADS = 8
SWA_KV_HEADS = 2
SWA_GROUP = SWA_HEADS // SWA_KV_HEADS
SWA_HEAD_DIM = 64
WINDOW = 128
D_FF = 4 * D_MODEL
EPS = 1e-6
N_MOD = 6

LANES = 128
HALF = LANES // 2
BF16_SUBLANES = 16
MLA_QK_COLS = MLA_HEADS * LANES
MLA_V_COLS = MLA_HEADS * MLA_V
SWA_Q_COLS = SWA_HEADS * SWA_HEAD_DIM
SWA_KV_COLS = SWA_KV_HEADS * SWA_HEAD_DIM
C_QLAT = 0
C_KVLAT = C_QLAT + Q_LORA
C_KPE = C_KVLAT + KV_LORA
C_SK = C_KPE + LANES
C_SV = C_SK + SWA_KV_COLS
IN_COLS_PAD = C_SV + SWA_KV_COLS

VMEM_LIMIT_BYTES = 62 * 1024 * 1024

LOG2E = float(np.log2(np.e))
MLA_Q_SCALE = (MLA_NOPE + MLA_ROPE) ** -0.5 * LOG2E
SWA_Q_SCALE = SWA_HEAD_DIM ** -0.5 * LOG2E

BF16 = jnp.bfloat16
F32 = jnp.float32


def _const_spec(shape):
    nd = len(shape)
    return pl.BlockSpec(shape, lambda *_: (0,) * nd, pipeline_mode=pl.Buffered(1))


def _rms(x):
    return x * lax.rsqrt(jnp.mean(x * x, axis=-1, keepdims=True) + EPS)


def _ada_kernel(c_ref, w_ref, b_ref, o_ref):
    c = c_ref[...]
    s = c / (1.0 + jnp.exp(-c))
    o_ref[0, :, 0, :] = jnp.dot(s.astype(BF16), w_ref[...].astype(BF16), preferred_element_type=F32) + b_ref[...]


def _ada_call(c, w_ada, b_ada):
    b = c.shape[0]
    n = w_ada.shape[1]
    tn = D_MODEL
    return pl.pallas_call(
        _ada_kernel,
        grid=(n // tn,),
        in_specs=[
            pl.BlockSpec((b, D_MODEL), lambda j: (0, 0)),
            pl.BlockSpec((D_MODEL, tn), lambda j: (0, j)),
            pl.BlockSpec((1, tn), lambda j: (0, j)),
        ],
        out_specs=pl.BlockSpec((1, b, 1, tn), lambda j: (j, 0, 0, 0)),
        out_shape=jax.ShapeDtypeStruct((n // tn, b, 1, tn), F32),
        compiler_params=pltpu.CompilerParams(dimension_semantics=("parallel",)),
        name="ada_mod",
    )(c, w_ada, b_ada.reshape(1, n))


def _mod_spec():
    return pl.BlockSpec((N_MOD, 1, 1, D_MODEL), lambda bi, i: (0, bi, 0, 0))


def _rope_group(xg, keep, swap):
    up = pltpu.roll(xg, LANES - MLA_ROPE // 2, 1)
    return xg * keep + up * swap


def _relayout_weights(w_in_ref, w_qb_ref, w_kvb_ref, win_ref, wsq_ref, wq_ref, wk_ref, wv_ref):
    o2 = Q_LORA + KV_LORA
    o3 = o2 + MLA_ROPE
    o4 = o3 + SWA_Q_COLS
    half = MLA_ROPE // 2
    pad = LANES - MLA_NOPE - MLA_ROPE - half
    d_in = w_in_ref.shape[1]
    kpe = [jnp.zeros((MLA_NOPE, d_in), F32), w_in_ref[o2:o3, :], w_in_ref[o2:o2 + half, :],
           jnp.zeros((pad, d_in), F32)]
    sq = [w_in_ref[o3 + (kv * SWA_GROUP + g) * SWA_HEAD_DIM:o3 + (kv * SWA_GROUP + g + 1) * SWA_HEAD_DIM, :]
          for g in range(SWA_GROUP) for kv in range(SWA_KV_HEADS)]
    wsq_ref[...] = jnp.concatenate(sq, axis=0).astype(BF16)
    win_t = jnp.concatenate([w_in_ref[:o2, :]] + kpe + [w_in_ref[o4:, :]], axis=0)
    chunk = 2 * LANES
    for c0 in range(0, IN_COLS_PAD, chunk):
        win_ref[:, c0:c0 + chunk] = win_t[c0:c0 + chunk, :].T.astype(BF16)
    hd_rows = MLA_NOPE + MLA_ROPE
    parts = []
    for hd in range(MLA_HEADS):
        r0 = hd * hd_rows
        parts += [w_qb_ref[r0:r0 + hd_rows, :], w_qb_ref[r0 + MLA_NOPE:r0 + MLA_NOPE + half, :],
                  jnp.zeros((pad, w_qb_ref.shape[1]), F32)]
    wq_ref[...] = jnp.concatenate(parts, axis=0).astype(BF16)
    wkvb = w_kvb_ref[...]
    wk_ref[...] = jnp.concatenate([wkvb[:, hd * LANES:hd * LANES + MLA_NOPE] for hd in range(MLA_HEADS)],
                                  axis=1).astype(BF16)
    wv = jnp.concatenate([wkvb[:, hd * LANES + MLA_NOPE:(hd + 1) * LANES] for hd in range(MLA_HEADS)], axis=1)
    wv_ref[...] = wv.T.astype(BF16)


def _pre_kernel(x_ref, mod_ref, gmix_ref, w_in_ref, gqa_ref, w_qb_ref, gkva_ref, w_kvb_ref,
                ropeq_ref, ropek_ref,
                qf_ref, kf_ref, v_ref, sq_ref, sk_ref, sv_ref,
                win_ref, wsq_ref, wq_ref, wk_ref, wv_ref, *, ts):
    @pl.when((pl.program_id(0) == 0) & (pl.program_id(1) == 0))
    def _():
        _relayout_weights(w_in_ref, w_qb_ref, w_kvb_ref, win_ref, wsq_ref, wq_ref, wk_ref, wv_ref)

    for r0 in range(0, x_ref.shape[1], ts):
        _pre_rows(r0, ts, x_ref, mod_ref, gmix_ref, win_ref, wsq_ref, gqa_ref, wq_ref, gkva_ref, wk_ref, wv_ref,
                  ropeq_ref, ropek_ref, qf_ref, kf_ref, v_ref, sq_ref, sk_ref, sv_ref)


def _pre_rows(r0, ts, x_ref, mod_ref, gmix_ref, win_ref, wsq_ref, gqa_ref, wq_ref, gkva_ref, wk_ref, wv_ref,
              ropeq_ref, ropek_ref, qf_ref, kf_ref, v_ref, sq_ref, sk_ref, sv_ref):
    rows = slice(r0, r0 + ts)
    x = x_ref[0, rows]
    mod = mod_ref[:, 0, 0, :]
    sh1 = mod[0:1]
    sc1 = mod[1:2]
    h = (_rms(x) * gmix_ref[...] * (1.0 + sc1) + sh1).astype(BF16)
    proj = jnp.dot(h, win_ref[...], preferred_element_type=F32)

    sqt = lax.dot_general(wsq_ref[...], h, (((1,), (1,)), ((), ())), preferred_element_type=F32) * SWA_Q_SCALE
    for g in range(SWA_GROUP):
        sq_ref[0, g, :, rows] = sqt[g * LANES:(g + 1) * LANES, :].astype(BF16)
    swa_k = proj[:, C_SK:C_SV]
    low = lax.broadcasted_iota(jnp.int32, swa_k.shape, 1) < HALF
    sk_ref[0, rows, :LANES] = jnp.where(low, swa_k, 0.0).astype(BF16)
    sk_ref[0, rows, LANES:] = jnp.where(low, 0.0, swa_k).astype(BF16)
    svt = proj[:, C_SV:IN_COLS_PAD].T.astype(BF16)
    for jb in range(ts // WINDOW):
        sv_ref[0, r0 // WINDOW + jb] = svt[:, jb * WINDOW:(jb + 1) * WINDOW]

    qn = (_rms(proj[:, C_QLAT:C_KVLAT]) * gqa_ref[...]).astype(BF16)
    kvn = (_rms(proj[:, C_KVLAT:C_KPE]) * gkva_ref[...]).astype(BF16)
    qt = lax.dot_general(wq_ref[...], qn, (((1,), (1,)), ((), ())), preferred_element_type=F32)
    kn = jnp.dot(kvn, wk_ref[...], preferred_element_type=F32)
    vt = lax.dot_general(wv_ref[...], kvn, (((1,), (1,)), ((), ())), preferred_element_type=F32).astype(BF16)
    tk = v_ref.shape[-1]
    for hp in range(MLA_HEADS // 2):
        for jb in range(ts // tk):
            v_ref[0, hp, r0 // tk + jb] = vt[hp * LANES:(hp + 1) * LANES, jb * tk:(jb + 1) * tk]

    q_keep, q_swap = ropeq_ref[0, :, rows], ropeq_ref[1, :, rows]
    half = MLA_ROPE // 2
    kpe = _rope_group(proj[:, C_KPE:C_SK], ropek_ref[0, rows], ropek_ref[1, rows])
    nope_lanes = lax.broadcasted_iota(jnp.int32, kpe.shape, 1) < MLA_NOPE
    for hd in range(MLA_HEADS):
        sl = slice(hd * LANES, (hd + 1) * LANES)
        xg = qt[sl, :]
        up = jnp.concatenate([xg[half:], xg[:half]], axis=0)
        qf_ref[0, sl, rows] = (xg * q_keep + up * q_swap).astype(BF16)
        kn_pair = kn[:, (hd // 2) * LANES:(hd // 2 + 1) * LANES]
        if hd % 2:
            kn_pair = pltpu.roll(kn_pair, HALF, 1)
        kf_ref[0, rows, sl] = jnp.where(nope_lanes, kn_pair, kpe).astype(BF16)


def _rope_tables(scale_q):
    half = MLA_ROPE // 2
    freqs = ROPE_THETA ** (-np.arange(0, MLA_ROPE, 2, dtype=np.float64) / MLA_ROPE)
    ang = np.arange(SEQ, dtype=np.float64)[:, None] * freqs[None, :]
    cos, sin = np.cos(ang), np.sin(ang)
    keep = np.zeros((SEQ, LANES))
    swap = np.zeros((SEQ, LANES))
    keep[:, :MLA_NOPE] = 1.0
    keep[:, MLA_NOPE:MLA_NOPE + half] = cos
    keep[:, MLA_NOPE + half:MLA_NOPE + MLA_ROPE] = cos
    swap[:, MLA_NOPE:MLA_NOPE + half] = -sin
    swap[:, MLA_NOPE + half:MLA_NOPE + MLA_ROPE] = sin
    return (np.stack([keep, swap]) * scale_q).astype(np.float32)


def _pre_call(x, mod, gmix, w_in, gqa, w_qb, gkva, w_kvb, tm, ts, tk):
    b, s, d = x.shape
    pairs = MLA_HEADS // 2
    ropeq = jnp.asarray(_rope_tables(MLA_Q_SCALE).transpose(0, 2, 1))
    ropek = jnp.asarray(_rope_tables(1.0))
    row = lambda bi, i: (bi, i, 0)
    out_shapes = (
        jax.ShapeDtypeStruct((b, MLA_QK_COLS, s), BF16),
        jax.ShapeDtypeStruct((b, s, MLA_QK_COLS), BF16),
        jax.ShapeDtypeStruct((b, pairs, s // tk, LANES, tk), BF16),
        jax.ShapeDtypeStruct((b, SWA_GROUP, LANES, s), BF16),
        jax.ShapeDtypeStruct((b, s, SWA_KV_HEADS * LANES), BF16),
        jax.ShapeDtypeStruct((b, s // WINDOW, SWA_KV_COLS, WINDOW), BF16),
    )
    out_specs = [
        pl.BlockSpec((1, MLA_QK_COLS, tm), lambda bi, i: (bi, 0, i)),
        pl.BlockSpec((1, tm, MLA_QK_COLS), row),
        pl.BlockSpec((1, pairs, tm // tk, LANES, tk), lambda bi, i: (bi, 0, i, 0, 0)),
        pl.BlockSpec((1, SWA_GROUP, LANES, tm), lambda bi, i: (bi, 0, 0, i)),
        pl.BlockSpec((1, tm, SWA_KV_HEADS * LANES), row),
        pl.BlockSpec((1, tm // WINDOW, SWA_KV_COLS, WINDOW), lambda bi, i: (bi, i, 0, 0)),
    ]
    return pl.pallas_call(
        functools.partial(_pre_kernel, ts=ts),
        grid=(b, s // tm),
        in_specs=[
            pl.BlockSpec((1, tm, d), row),
            _mod_spec(),
            _const_spec((1, d)),
            _const_spec(w_in.shape),
            _const_spec((1, Q_LORA)),
            _const_spec(w_qb.shape),
            _const_spec((1, KV_LORA)),
            _const_spec(w_kvb.shape),
            pl.BlockSpec((2, LANES, tm), lambda bi, i: (0, 0, i)),
            pl.BlockSpec((2, tm, LANES), lambda bi, i: (0, i, 0)),
        ],
        out_specs=tuple(out_specs),
        out_shape=out_shapes,
        scratch_shapes=[
            pltpu.VMEM((d, IN_COLS_PAD), BF16),
            pltpu.VMEM((SWA_Q_COLS, d), BF16),
            pltpu.VMEM((MLA_QK_COLS, Q_LORA), BF16),
            pltpu.VMEM((KV_LORA, MLA_HEADS * MLA_NOPE), BF16),
            pltpu.VMEM((MLA_V_COLS, KV_LORA), BF16),
        ],
        compiler_params=pltpu.CompilerParams(
            dimension_semantics=("arbitrary", "arbitrary"), vmem_limit_bytes=VMEM_LIMIT_BYTES),
        name="pre_attn",
    )(x, mod, gmix, w_in, gqa, w_qb, gkva, w_kvb, ropeq, ropek)


MLA_LOOKAHEAD = 6


def _mla_kernel(q_ref, k_ref, vt_ref, wup_ref, wdown_ref, o_ref, wup_bf_ref, wdown_bf_ref, *, tq):
    wup_bf_ref[...] = wup_ref[...].astype(BF16)
    wdown_bf_ref[...] = wdown_ref[...].astype(BF16)
    ki = lax.broadcasted_iota(jnp.int32, (tq, tq), 0)
    qi = lax.broadcasted_iota(jnp.int32, (tq, tq), 1)
    causal = ki <= qi
    heads = range(q_ref.shape[1] // LANES)
    chains = [(i, j, h) for i in range(q_ref.shape[2] // tq) for j in range(i + 1) for h in heads]

    def scores(i, j, h):
        cols = slice(h * LANES, (h + 1) * LANES)
        qt = q_ref[0, cols, i * tq:(i + 1) * tq]
        k = k_ref[0, j * tq:(j + 1) * tq, cols]
        return jnp.dot(k, qt, preferred_element_type=F32)

    ones = jnp.ones((BF16_SUBLANES, tq), BF16)
    pending = [scores(*ch) for ch in chains[:MLA_LOOKAHEAD]]
    for n, (i, j, h) in enumerate(chains):
        st = pending.pop(0)
        if n + MLA_LOOKAHEAD < len(chains):
            pending.append(scores(*chains[n + MLA_LOOKAHEAD]))
        if j == 0 and h == 0:
            state = [(jnp.full((1, tq), -jnp.inf, F32), jnp.zeros((MLA_V + BF16_SUBLANES, tq), F32))
                     for _ in heads]
        m, acc = state[h]
        if j == i:
            st = jnp.where(causal, st, -jnp.inf)
        vt = vt_ref[0, h // 2, j, (h % 2) * MLA_V:(h % 2 + 1) * MLA_V, :]
        vt = jnp.concatenate([vt, ones], axis=0)
        m_new = jnp.maximum(m, jnp.max(st, axis=0, keepdims=True))
        alpha = jnp.exp2(m - m_new)
        pt = jnp.exp2(st - m_new).astype(BF16)
        acc_new = alpha * acc + jnp.dot(vt, pt, preferred_element_type=F32)
        state[h] = (m_new, acc_new)
        if j == i and h == len(heads) - 1:
            for p in range(len(heads) // 2):
                pair = [state[2 * p][1], state[2 * p + 1][1]]
                out_t = jnp.concatenate([acc[:MLA_V] / acc[MLA_V:MLA_V + 1] for acc in pair], axis=0)
                o_ref[0, i * tq:(i + 1) * tq, p * LANES:(p + 1) * LANES] = out_t.T.astype(BF16)


def _mla_call(qf, kf, vt, w_up, w_down, tq, hps):
    b, s, _ = kf.shape
    pairs = MLA_HEADS // 2
    assert vt.shape == (b, pairs, s // tq, LANES, tq) and pairs % hps == 0 and qf.shape == (b, MLA_QK_COLS, s)
    seq_cols = lambda bi, g: (bi, 0, g)
    n_steps = b * (pairs // hps)
    step_rows = lambda bi, g: (bi * (pairs // hps) + g, 0)
    w_specs = [pl.BlockSpec((w.shape[0] // n_steps, w.shape[1]), step_rows) for w in (w_up, w_down)]
    return pl.pallas_call(
        functools.partial(_mla_kernel, tq=tq),
        grid=(b, pairs // hps),
        in_specs=[
            pl.BlockSpec((1, 2 * hps * LANES, s), lambda bi, g: (bi, g, 0)),
            pl.BlockSpec((1, s, 2 * hps * LANES), seq_cols),
            pl.BlockSpec((1, hps, s // tq, LANES, tq), lambda bi, g: (bi, g, 0, 0, 0)),
        ] + w_specs,
        out_specs=[pl.BlockSpec((1, s, hps * LANES), seq_cols)] + w_specs,
        out_shape=[jax.ShapeDtypeStruct((b, s, MLA_V_COLS), BF16),
                   jax.ShapeDtypeStruct(w_up.shape, BF16), jax.ShapeDtypeStruct(w_down.shape, BF16)],
        compiler_params=pltpu.CompilerParams(
            dimension_semantics=("parallel", "parallel"), vmem_limit_bytes=VMEM_LIMIT_BYTES),
        name="mla_attn",
    )(qf, kf, vt, w_up, w_down)


def _alibi_slope(h):
    return 2.0 ** (-8.0 * (h + 1) / SWA_HEADS)


SWA_STACK = 2
SWA_LOOKAHEAD = 6


def _swa_bias():
    k = np.arange(2 * WINDOW)[:, None]
    q = np.arange(WINDOW)[None, :]
    dist = q + WINDOW - k
    valid = (dist >= 0) & (dist < WINDOW)
    out = np.empty((SWA_KV_HEADS, 2 * WINDOW, SWA_GROUP * WINDOW), np.float32)
    for kv in range(SWA_KV_HEADS):
        for g in range(SWA_GROUP):
            slope = np.float32(_alibi_slope(kv * SWA_GROUP + g))
            out[kv, :, g * WINDOW:(g + 1) * WINDOW] = np.where(valid, -slope * dist.astype(np.float32) * LOG2E, -np.inf)
    return out


def _swa_tile_chains(sink_ref, bias_ref, q_ref, k_prev, k_ref, vt_prev, vt_ref, gone, store):
    nsub = q_ref.shape[3] // WINDOW
    ones = jnp.ones((BF16_SUBLANES, 2 * WINDOW), BF16)
    width = SWA_STACK * WINDOW
    chains = [(c, g0, kv) for c in range(nsub) for g0 in range(0, SWA_GROUP, SWA_STACK)
              for kv in range(SWA_KV_HEADS)]

    def scores(c, g0, kv):
        cols = slice(kv * LANES, (kv + 1) * LANES)
        if c == 0:
            k_win = jnp.concatenate([k_prev(cols), k_ref[0, :WINDOW, cols]], axis=0)
        else:
            k_win = k_ref[0, (c - 1) * WINDOW:(c + 1) * WINDOW, cols]
        qt = jnp.concatenate([q_ref[0, g0 + gg, :, c * WINDOW:(c + 1) * WINDOW] for gg in range(SWA_STACK)],
                             axis=1)
        st = jnp.dot(k_win, qt, preferred_element_type=F32)
        return st + bias_ref[kv, :, g0 * WINDOW:g0 * WINDOW + width]

    outs = {}
    pending = [scores(*ch) for ch in chains[:SWA_LOOKAHEAD]]
    for n, (c, g0, kv) in enumerate(chains):
        st = pending.pop(0)
        if n + SWA_LOOKAHEAD < len(chains):
            pending.append(scores(*chains[n + SWA_LOOKAHEAD]))
        rows = slice(kv * SWA_HEAD_DIM, (kv + 1) * SWA_HEAD_DIM)
        if c == 0:
            vt = jnp.concatenate([vt_prev(rows), vt_ref[0, 0, rows, :]], axis=1)
            st = jnp.concatenate([st[:WINDOW] + gone, st[WINDOW:]], axis=0)
        else:
            vt = jnp.concatenate([vt_ref[0, c - 1, rows, :], vt_ref[0, c, rows, :]], axis=1)
        sink = sink_ref[kv:kv + 1, g0 * WINDOW:g0 * WINDOW + width]
        m = jnp.maximum(jnp.max(st, axis=0, keepdims=True), sink)
        pt = jnp.exp2(st - m).astype(BF16)
        res = jnp.dot(jnp.concatenate([vt, ones], axis=0), pt, preferred_element_type=F32)
        denom = res[SWA_HEAD_DIM:SWA_HEAD_DIM + 1] + jnp.exp2(sink - m)
        outs[kv] = res[:SWA_HEAD_DIM] / denom
        if kv == SWA_KV_HEADS - 1:
            for gg in range(SWA_STACK):
                gs = slice(gg * WINDOW, (gg + 1) * WINDOW)
                out_t = jnp.concatenate([outs[0][:, gs], outs[1][:, gs]], axis=0)
                store(c, g0 + gg, out_t.T.astype(BF16))
        yield


def _post_kernel(x_ref, mla_ref, mod_ref, wo_ref, gmlp_ref, wup_ref, wdown_ref, gfin_ref,
                 sink_ref, bias_ref, qn_ref, kpn_ref, kn_ref, vtpn_ref, vtn_ref, q0_ref, k0_ref, vt0_ref,
                 o_ref, swa_ref, *, ts, ff_chunk):
    step = pl.program_id(0) * pl.num_programs(1) + pl.program_id(1)
    n_steps = pl.num_programs(0) * pl.num_programs(1)
    n_chunks = D_FF // ff_chunk

    def store(c, g, tile):
        swa_ref[c * WINDOW:(c + 1) * WINDOW, g * LANES:(g + 1) * LANES] = tile

    @pl.when(step == 0)
    def _():
        for _ in _swa_tile_chains(sink_ref, bias_ref, q0_ref, lambda cols: k0_ref[0, :WINDOW, cols], k0_ref,
                                  lambda rows: vt0_ref[0, 0, rows, :], vt0_ref, -jnp.inf, store):
            pass

    mod = mod_ref[:, 0, 0, :]
    g1, sh2, sc2, g2 = mod[2:3], mod[3:4], mod[4:5], mod[5:6]
    woa = wo_ref[:MLA_V_COLS, :].astype(BF16)
    wob = jnp.concatenate(
        [wo_ref[MLA_V_COLS + (kv * SWA_GROUP + g) * SWA_HEAD_DIM:MLA_V_COLS + (kv * SWA_GROUP + g + 1) * SWA_HEAD_DIM, :]
         for g in range(SWA_GROUP) for kv in range(SWA_KV_HEADS)], axis=0).astype(BF16)
    tiles = [slice(r0, r0 + ts) for r0 in range(0, x_ref.shape[1], ts)]
    mid = []
    for rows in tiles:
        attn = (jnp.dot(mla_ref[0, rows], woa, preferred_element_type=F32)
                + jnp.dot(swa_ref[rows, :], wob, preferred_element_type=F32))
        x1 = x_ref[0, rows] + g1 * attn
        h2 = (_rms(x1) * gmlp_ref[...] * (1.0 + sc2) + sh2).astype(BF16)
        mid.append((x1, h2))
    nxt = jnp.minimum(step + 1, n_steps - 1) % pl.num_programs(1)
    gone = jnp.where(nxt == 0, -jnp.inf, 0.0)
    chains = _swa_tile_chains(sink_ref, bias_ref, qn_ref, lambda cols: kpn_ref[0, :, cols], kn_ref,
                              lambda rows: vtpn_ref[0, 0, rows, :], vtn_ref, gone, store)
    n_slots = 2 * n_chunks * len(tiles)
    n_attn = (qn_ref.shape[3] // WINDOW) * (SWA_GROUP // SWA_STACK) * SWA_KV_HEADS
    emitted = 0
    accs = [jnp.zeros_like(x1) for x1, _ in mid]
    for c in range(n_chunks):
        cs = slice(c * ff_chunk, (c + 1) * ff_chunk)
        w_up_c = wup_ref[:, cs]
        w_down_c = wdown_ref[cs, :]
        us = []
        for t in range(2 * len(mid)):
            if t < len(mid):
                us.append(jnp.maximum(jnp.dot(mid[t][1], w_up_c, preferred_element_type=F32), 0.0))
            else:
                u = us[t - len(mid)]
                accs[t - len(mid)] = accs[t - len(mid)] + jnp.dot((u * u).astype(BF16), w_down_c,
                                                                  preferred_element_type=F32)
            slot = c * 2 * len(mid) + t + 1
            while emitted < slot * n_attn // n_slots:
                next(chains)
                emitted += 1
    for rows, (x1, _), acc in zip(tiles, mid, accs):
        x2 = x1 + g2 * acc
        o_ref[0, rows] = _rms(x2) * gfin_ref[...]


def _post_call(x, mla, mod, w_o, gmlp, wup, wdown, gfin, sinks, sq, sk, svt, tm, ts, ff_chunk):
    b, s, d = x.shape
    nt = s // tm
    nsub = tm // WINDOW
    bias = jnp.asarray(_swa_bias())
    sink_rows = jnp.repeat(sinks.reshape(SWA_KV_HEADS, SWA_GROUP), WINDOW, axis=1) * LOG2E
    row = lambda bi, i: (bi, i, 0)

    def nxt(bi, i):
        n = jnp.minimum(bi * nt + i + 1, b * nt - 1)
        return n // nt, n % nt

    def nxt_prev_blk(bi, i):
        bn, tn = nxt(bi, i)
        return bn, jnp.maximum(tn * nsub - 1, 0)

    return pl.pallas_call(
        functools.partial(_post_kernel, ts=ts, ff_chunk=ff_chunk),
        grid=(b, nt),
        in_specs=[
            pl.BlockSpec((1, tm, d), row),
            pl.BlockSpec((1, tm, MLA_V_COLS), row),
            _mod_spec(),
            _const_spec(w_o.shape),
            _const_spec((1, d)),
            _const_spec(wup.shape),
            _const_spec(wdown.shape),
            _const_spec((1, d)),
            _const_spec(sink_rows.shape),
            _const_spec(bias.shape),
            pl.BlockSpec((1, SWA_GROUP, LANES, tm), lambda bi, i: (nxt(bi, i)[0], 0, 0, nxt(bi, i)[1])),
            pl.BlockSpec((1, WINDOW, SWA_KV_HEADS * LANES), lambda bi, i: nxt_prev_blk(bi, i) + (0,)),
            pl.BlockSpec((1, tm, SWA_KV_HEADS * LANES), lambda bi, i: nxt(bi, i) + (0,)),
            pl.BlockSpec((1, 1, SWA_KV_COLS, WINDOW), lambda bi, i: nxt_prev_blk(bi, i) + (0, 0)),
            pl.BlockSpec((1, nsub, SWA_KV_COLS, WINDOW), lambda bi, i: nxt(bi, i) + (0, 0)),
            pl.BlockSpec((1, SWA_GROUP, LANES, tm), lambda bi, i: (0, 0, 0, 0), pipeline_mode=pl.Buffered(1)),
            pl.BlockSpec((1, tm, SWA_KV_HEADS * LANES), lambda bi, i: (0, 0, 0), pipeline_mode=pl.Buffered(1)),
            pl.BlockSpec((1, nsub, SWA_KV_COLS, WINDOW), lambda bi, i: (0, 0, 0, 0), pipeline_mode=pl.Buffered(1)),
        ],
        out_specs=pl.BlockSpec((1, tm, d), row),
        out_shape=jax.ShapeDtypeStruct((b, s, d), F32),
        scratch_shapes=[pltpu.VMEM((tm, SWA_Q_COLS), BF16)],
        compiler_params=pltpu.CompilerParams(
            dimension_semantics=("arbitrary", "arbitrary"), vmem_limit_bytes=VMEM_LIMIT_BYTES),
        name="post_attn",
    )(x, mla, mod, w_o, gmlp, wup, wdown, gfin, sink_rows, bias, sq, sk, sk, svt, svt, sq, sk, svt)


def kernel(x, c, w_ada, b_ada, norm_mix_g, w_in, g_qa, w_qb, g_kva, w_kvb, sinks,
           w_o, norm_mlp_g, w_up, w_down, final_g):
    depth = w_ada.shape[0]
    b = x.shape[0]
    assert depth == 1, "the final rmsnorm is fused into the single layer's post-attention call"
    for l in range(depth):
        mod = _ada_call(c, w_ada[l], b_ada[l])
        qf, kf, vt, sq, sk, svt = _pre_call(
            x, mod, norm_mix_g[l][None], w_in[l].T, g_qa[l][None],
            w_qb[l].transpose(1, 2, 0).reshape(-1, Q_LORA), g_kva[l][None], w_kvb[l].reshape(KV_LORA, -1),
            tm=1024, ts=256, tk=256)
        mla, w_up_bf, w_down_bf = _mla_call(qf, kf, vt, w_up[l], w_down[l], tq=256, hps=2)
        x = _post_call(x, mla, mod, w_o[l], norm_mlp_g[l][None], w_up_bf, w_down_bf, final_g[None],
                       sinks[l], sq, sk, svt, tm=1024, ts=256, ff_chunk=1024)
    return x
```

```python
import functools

import numpy as np
import jax
import jax.numpy as jnp
from jax import lax
from jax.experimental import pallas as pl
from jax.experimental.pallas import tpu as pltpu

D_MODEL = 1024
SEQ = 2048
MLA_HEADS = 8
MLA_NOPE = 64
MLA_ROPE = 32
MLA_V = 64
Q_LORA = 384
KV_LORA = 256
ROPE_THETA = 10000.0
SWA_HEADS = 8
SWA_KV_HEADS = 2
SWA_GROUP = SWA_HEADS // SWA_KV_HEADS
SWA_HEAD_DIM = 64
WINDOW = 128
D_FF = 4 * D_MODEL
EPS = 1e-6
N_MOD = 6

LANES = 128
HALF = LANES // 2
BF16_SUBLANES = 16
MLA_QK_COLS = MLA_HEADS * LANES
MLA_V_COLS = MLA_HEADS * MLA_V
SWA_Q_COLS = SWA_HEADS * SWA_HEAD_DIM
SWA_KV_COLS = SWA_KV_HEADS * SWA_HEAD_DIM
C_QLAT = 0
C_KVLAT = C_QLAT + Q_LORA
C_KPE = C_KVLAT + KV_LORA
C_SQ = C_KPE + LANES
C_SK = C_SQ + SWA_Q_COLS
C_SV = C_SK + SWA_KV_COLS
IN_COLS_PAD = C_SV + SWA_KV_COLS

VMEM_LIMIT_BYTES = 62 * 1024 * 1024

LOG2E = float(np.log2(np.e))
MLA_Q_SCALE = (MLA_NOPE + MLA_ROPE) ** -0.5 * LOG2E
SWA_Q_SCALE = SWA_HEAD_DIM ** -0.5 * LOG2E

BF16 = jnp.bfloat16
F32 = jnp.float32


def _const_spec(shape):
    nd = len(shape)
    return pl.BlockSpec(shape, lambda *_: (0,) * nd, pipeline_mode=pl.Buffered(1))


def _rms(x):
    return x * lax.rsqrt(jnp.mean(x * x, axis=-1, keepdims=True) + EPS)


def _ada_kernel(c_ref, w_ref, b_ref, o_ref):
    c = c_ref[...]
    s = c / (1.0 + jnp.exp(-c))
    o_ref[0, :, 0, :] = jnp.dot(s.astype(BF16), w_ref[...].astype(BF16), preferred_element_type=F32) + b_ref[...]


def _ada_call(c, w_ada, b_ada):
    b = c.shape[0]
    n = w_ada.shape[1]
    tn = D_MODEL
    return pl.pallas_call(
        _ada_kernel,
        grid=(n // tn,),
        in_specs=[
            pl.BlockSpec((b, D_MODEL), lambda j: (0, 0)),
            pl.BlockSpec((D_MODEL, tn), lambda j: (0, j)),
            pl.BlockSpec((1, tn), lambda j: (0, j)),
        ],
        out_specs=pl.BlockSpec((1, b, 1, tn), lambda j: (j, 0, 0, 0)),
        out_shape=jax.ShapeDtypeStruct((n // tn, b, 1, tn), F32),
        compiler_params=pltpu.CompilerParams(dimension_semantics=("parallel",)),
        name="ada_mod",
    )(c, w_ada, b_ada.reshape(1, n))


def _mod_spec():
    return pl.BlockSpec((N_MOD, 1, 1, D_MODEL), lambda bi, i: (0, bi, 0, 0))


def _rope_group(xg, keep, swap):
    up = pltpu.roll(xg, LANES - MLA_ROPE // 2, 1)
    return xg * keep + up * swap


def _relayout_weights(w_in_ref, w_qb_ref, w_kvb_ref, win_ref, wq_ref, wk_ref, wv_ref):
    o2 = Q_LORA + KV_LORA
    o3 = o2 + MLA_ROPE
    o4 = o3 + SWA_Q_COLS
    half = MLA_ROPE // 2
    pad = LANES - MLA_NOPE - MLA_ROPE - half
    d_in = w_in_ref.shape[1]
    kpe = [jnp.zeros((MLA_NOPE, d_in), F32), w_in_ref[o2:o3, :], w_in_ref[o2:o2 + half, :],
           jnp.zeros((pad, d_in), F32)]
    sq = [w_in_ref[o3 + (kv * SWA_GROUP + g) * SWA_HEAD_DIM:o3 + (kv * SWA_GROUP + g + 1) * SWA_HEAD_DIM, :]
          for g in range(SWA_GROUP) for kv in range(SWA_KV_HEADS)]
    win_t = jnp.concatenate([w_in_ref[:o2, :]] + kpe + sq + [w_in_ref[o4:, :]], axis=0)
    chunk = 2 * LANES
    for c0 in range(0, IN_COLS_PAD, chunk):
        win_ref[:, c0:c0 + chunk] = win_t[c0:c0 + chunk, :].T.astype(BF16)
    hd_rows = MLA_NOPE + MLA_ROPE
    parts = []
    for hd in range(MLA_HEADS):
        r0 = hd * hd_rows
        parts += [w_qb_ref[r0:r0 + hd_rows, :], w_qb_ref[r0 + MLA_NOPE:r0 + MLA_NOPE + half, :],
                  jnp.zeros((pad, w_qb_ref.shape[1]), F32)]
    wq_ref[...] = jnp.concatenate(parts, axis=0).astype(BF16)
    wkvb = w_kvb_ref[...]
    wk_ref[...] = jnp.concatenate([wkvb[:, hd * LANES:hd * LANES + MLA_NOPE] for hd in range(MLA_HEADS)],
                                  axis=1).astype(BF16)
    wv = jnp.concatenate([wkvb[:, hd * LANES + MLA_NOPE:(hd + 1) * LANES] for hd in range(MLA_HEADS)], axis=1)
    wv_ref[...] = wv.T.astype(BF16)


def _pre_kernel(x_ref, mod_ref, gmix_ref, w_in_ref, gqa_ref, w_qb_ref, gkva_ref, w_kvb_ref,
                ropeq_ref, ropek_ref,
                qf_ref, kf_ref, v_ref, sq_ref, sk_ref, sv_ref,
                win_ref, wq_ref, wk_ref, wv_ref, *, ts):
    @pl.when((pl.program_id(0) == 0) & (pl.program_id(1) == 0))
    def _():
        _relayout_weights(w_in_ref, w_qb_ref, w_kvb_ref, win_ref, wq_ref, wk_ref, wv_ref)

    for r0 in range(0, x_ref.shape[1], ts):
        _pre_rows(r0, ts, x_ref, mod_ref, gmix_ref, win_ref, gqa_ref, wq_ref, gkva_ref, wk_ref, wv_ref,
                  ropeq_ref, ropek_ref, qf_ref, kf_ref, v_ref, sq_ref, sk_ref, sv_ref)


def _pre_rows(r0, ts, x_ref, mod_ref, gmix_ref, win_ref, gqa_ref, wq_ref, gkva_ref, wk_ref, wv_ref,
              ropeq_ref, ropek_ref, qf_ref, kf_ref, v_ref, sq_ref, sk_ref, sv_ref):
    rows = slice(r0, r0 + ts)
    x = x_ref[0, rows]
    mod = mod_ref[:, 0, 0, :]
    sh1 = mod[0:1]
    sc1 = mod[1:2]
    h = (_rms(x) * gmix_ref[...] * (1.0 + sc1) + sh1).astype(BF16)
    proj = jnp.dot(h, win_ref[...], preferred_element_type=F32)

    for g in range(SWA_GROUP):
        sq_ref[0, g, rows] = (proj[:, C_SQ + g * LANES:C_SQ + (g + 1) * LANES] * SWA_Q_SCALE).astype(BF16)
    swa_k = proj[:, C_SK:C_SV]
    low = lax.broadcasted_iota(jnp.int32, swa_k.shape, 1) < HALF
    sk_ref[0, rows, :LANES] = jnp.where(low, swa_k, 0.0).astype(BF16)
    sk_ref[0, rows, LANES:] = jnp.where(low, 0.0, swa_k).astype(BF16)
    svt = proj[:, C_SV:IN_COLS_PAD].T.astype(BF16)
    for jb in range(ts // WINDOW):
        sv_ref[0, r0 // WINDOW + jb] = svt[:, jb * WINDOW:(jb + 1) * WINDOW]

    qn = (_rms(proj[:, C_QLAT:C_KVLAT]) * gqa_ref[...]).astype(BF16)
    kvn = (_rms(proj[:, C_KVLAT:C_KPE]) * gkva_ref[...]).astype(BF16)
    qt = lax.dot_general(wq_ref[...], qn, (((1,), (1,)), ((), ())), preferred_element_type=F32)
    kn = jnp.dot(kvn, wk_ref[...], preferred_element_type=F32)
    vt = lax.dot_general(wv_ref[...], kvn, (((1,), (1,)), ((), ())), preferred_element_type=F32).astype(BF16)
    tk = v_ref.shape[-1]
    for hp in range(MLA_HEADS // 2):
        for jb in range(ts // tk):
            v_ref[0, hp, r0 // tk + jb] = vt[hp * LANES:(hp + 1) * LANES, jb * tk:(jb + 1) * tk]

    q_keep, q_swap = ropeq_ref[0, :, rows], ropeq_ref[1, :, rows]
    half = MLA_ROPE // 2
    kpe = _rope_group(proj[:, C_KPE:C_SQ], ropek_ref[0, rows], ropek_ref[1, rows])
    nope_lanes = lax.broadcasted_iota(jnp.int32, kpe.shape, 1) < MLA_NOPE
    for hd in range(MLA_HEADS):
        sl = slice(hd * LANES, (hd + 1) * LANES)
        xg = qt[sl, :]
        up = jnp.concatenate([xg[half:], xg[:half]], axis=0)
        qf_ref[0, sl, rows] = (xg * q_keep + up * q_swap).astype(BF16)
        kn_pair = kn[:, (hd // 2) * LANES:(hd // 2 + 1) * LANES]
        if hd % 2:
            kn_pair = pltpu.roll(kn_pair, HALF, 1)
        kf_ref[0, rows, sl] = jnp.where(nope_lanes, kn_pair, kpe).astype(BF16)


def _rope_tables(scale_q):
    half = MLA_ROPE // 2
    freqs = ROPE_THETA ** (-np.arange(0, MLA_ROPE, 2, dtype=np.float64) / MLA_ROPE)
    ang = np.arange(SEQ, dtype=np.float64)[:, None] * freqs[None, :]
    cos, sin = np.cos(ang), np.sin(ang)
    keep = np.zeros((SEQ, LANES))
    swap = np.zeros((SEQ, LANES))
    keep[:, :MLA_NOPE] = 1.0
    keep[:, MLA_NOPE:MLA_NOPE + half] = cos
    keep[:, MLA_NOPE + half:MLA_NOPE + MLA_ROPE] = cos
    swap[:, MLA_NOPE:MLA_NOPE + half] = -sin
    swap[:, MLA_NOPE + half:MLA_NOPE + MLA_ROPE] = sin
    return (np.stack([keep, swap]) * scale_q).astype(np.float32)


def _pre_call(x, mod, gmix, w_in, gqa, w_qb, gkva, w_kvb, tm, ts, tk):
    b, s, d = x.shape
    pairs = MLA_HEADS // 2
    ropeq = jnp.asarray(_rope_tables(MLA_Q_SCALE).transpose(0, 2, 1))
    ropek = jnp.asarray(_rope_tables(1.0))
    row = lambda bi, i: (bi, i, 0)
    out_shapes = (
        jax.ShapeDtypeStruct((b, MLA_QK_COLS, s), BF16),
        jax.ShapeDtypeStruct((b, s, MLA_QK_COLS), BF16),
        jax.ShapeDtypeStruct((b, pairs, s // tk, LANES, tk), BF16),
        jax.ShapeDtypeStruct((b, SWA_GROUP, s, LANES), BF16),
        jax.ShapeDtypeStruct((b, s, SWA_KV_HEADS * LANES), BF16),
        jax.ShapeDtypeStruct((b, s // WINDOW, SWA_KV_COLS, WINDOW), BF16),
    )
    out_specs = [
        pl.BlockSpec((1, MLA_QK_COLS, tm), lambda bi, i: (bi, 0, i)),
        pl.BlockSpec((1, tm, MLA_QK_COLS), row),
        pl.BlockSpec((1, pairs, tm // tk, LANES, tk), lambda bi, i: (bi, 0, i, 0, 0)),
        pl.BlockSpec((1, SWA_GROUP, tm, LANES), lambda bi, i: (bi, 0, i, 0)),
        pl.BlockSpec((1, tm, SWA_KV_HEADS * LANES), row),
        pl.BlockSpec((1, tm // WINDOW, SWA_KV_COLS, WINDOW), lambda bi, i: (bi, i, 0, 0)),
    ]
    return pl.pallas_call(
        functools.partial(_pre_kernel, ts=ts),
        grid=(b, s // tm),
        in_specs=[
            pl.BlockSpec((1, tm, d), row),
            _mod_spec(),
            _const_spec((1, d)),
            _const_spec(w_in.shape),
            _const_spec((1, Q_LORA)),
            _const_spec(w_qb.shape),
            _const_spec((1, KV_LORA)),
            _const_spec(w_kvb.shape),
            pl.BlockSpec((2, LANES, tm), lambda bi, i: (0, 0, i)),
            pl.BlockSpec((2, tm, LANES), lambda bi, i: (0, i, 0)),
        ],
        out_specs=tuple(out_specs),
        out_shape=out_shapes,
        scratch_shapes=[
            pltpu.VMEM((d, IN_COLS_PAD), BF16),
            pltpu.VMEM((MLA_QK_COLS, Q_LORA), BF16),
            pltpu.VMEM((KV_LORA, MLA_HEADS * MLA_NOPE), BF16),
            pltpu.VMEM((MLA_V_COLS, KV_LORA), BF16),
        ],
        compiler_params=pltpu.CompilerParams(
            dimension_semantics=("arbitrary", "arbitrary"), vmem_limit_bytes=VMEM_LIMIT_BYTES),
        name="pre_attn",
    )(x, mod, gmix, w_in, gqa, w_qb, gkva, w_kvb, ropeq, ropek)


MLA_LOOKAHEAD = 6


def _mla_kernel(q_ref, k_ref, vt_ref, wup_ref, wdown_ref, o_ref, wup_bf_ref, wdown_bf_ref, *, tq):
    wup_bf_ref[...] = wup_ref[...].astype(BF16)
    wdown_bf_ref[...] = wdown_ref[...].astype(BF16)
    ki = lax.broadcasted_iota(jnp.int32, (tq, tq), 0)
    qi = lax.broadcasted_iota(jnp.int32, (tq, tq), 1)
    causal = ki <= qi
    heads = range(q_ref.shape[1] // LANES)
    chains = [(i, j, h) for i in range(q_ref.shape[2] // tq) for h in heads for j in range(i + 1)]

    def scores(i, j, h):
        cols = slice(h * LANES, (h + 1) * LANES)
        qt = q_ref[0, cols, i * tq:(i + 1) * tq]
        k = k_ref[0, j * tq:(j + 1) * tq, cols]
        return jnp.dot(k, qt, preferred_element_type=F32)

    ones = jnp.ones((BF16_SUBLANES, tq), BF16)
    state = {}
    pending = [scores(*ch) for ch in chains[:MLA_LOOKAHEAD]]
    for n, (i, j, h) in enumerate(chains):
        st = pending.pop(0)
        if n + MLA_LOOKAHEAD < len(chains):
            pending.append(scores(*chains[n + MLA_LOOKAHEAD]))
        if j == 0:
            state[h] = (jnp.full((1, tq), -jnp.inf, F32), jnp.zeros((MLA_V + BF16_SUBLANES, tq), F32))
        m, acc = state[h]
        if j == i:
            st = jnp.where(causal, st, -jnp.inf)
        vt = vt_ref[0, h // 2, j, (h % 2) * MLA_V:(h % 2 + 1) * MLA_V, :]
        vt = jnp.concatenate([vt, ones], axis=0)
        m_new = jnp.maximum(m, jnp.max(st, axis=0, keepdims=True))
        alpha = jnp.exp2(m - m_new)
        pt = jnp.exp2(st - m_new).astype(BF16)
        acc_new = alpha * acc + jnp.dot(vt, pt, preferred_element_type=F32)
        state[h] = (m_new, acc_new)
        if j == i and h % 2 == 1:
            p = h // 2
            pair = [state[2 * p][1], state[2 * p + 1][1]]
            out_t = jnp.concatenate([acc[:MLA_V] / acc[MLA_V:MLA_V + 1] for acc in pair], axis=0)
            o_ref[0, i * tq:(i + 1) * tq, p * LANES:(p + 1) * LANES] = out_t.T.astype(BF16)


def _mla_call(qf, kf, vt, w_up, w_down, tq, hps):
    b, s, _ = kf.shape
    pairs = MLA_HEADS // 2
    assert vt.shape == (b, pairs, s // tq, LANES, tq) and pairs % hps == 0 and qf.shape == (b, MLA_QK_COLS, s)
    seq_cols = lambda bi, g: (bi, 0, g)
    n_steps = b * (pairs // hps)
    step_rows = lambda bi, g: (bi * (pairs // hps) + g, 0)
    w_specs = [pl.BlockSpec((w.shape[0] // n_steps, w.shape[1]), step_rows) for w in (w_up, w_down)]
    return pl.pallas_call(
        functools.partial(_mla_kernel, tq=tq),
        grid=(b, pairs // hps),
        in_specs=[
            pl.BlockSpec((1, 2 * hps * LANES, s), lambda bi, g: (bi, g, 0)),
            pl.BlockSpec((1, s, 2 * hps * LANES), seq_cols),
            pl.BlockSpec((1, hps, s // tq, LANES, tq), lambda bi, g: (bi, g, 0, 0, 0)),
        ] + w_specs,
        out_specs=[pl.BlockSpec((1, s, hps * LANES), seq_cols)] + w_specs,
        out_shape=[jax.ShapeDtypeStruct((b, s, MLA_V_COLS), BF16),
                   jax.ShapeDtypeStruct(w_up.shape, BF16), jax.ShapeDtypeStruct(w_down.shape, BF16)],
        compiler_params=pltpu.CompilerParams(
            dimension_semantics=("parallel", "parallel"), vmem_limit_bytes=VMEM_LIMIT_BYTES),
        name="mla_attn",
    )(qf, kf, vt, w_up, w_down)


def _alibi_slope(h):
    return 2.0 ** (-8.0 * (h + 1) / SWA_HEADS)


SWA_STACK = 2
SWA_LOOKAHEAD = 6


def _swa_bias():
    k = np.arange(2 * WINDOW)[:, None]
    q = np.arange(WINDOW)[None, :]
    dist = q + WINDOW - k
    valid = (dist >= 0) & (dist < WINDOW)
    out = np.empty((SWA_KV_HEADS, 2 * WINDOW, SWA_GROUP * WINDOW), np.float32)
    for kv in range(SWA_KV_HEADS):
        for g in range(SWA_GROUP):
            slope = np.float32(_alibi_slope(kv * SWA_GROUP + g))
            out[kv, :, g * WINDOW:(g + 1) * WINDOW] = np.where(valid, -slope * dist.astype(np.float32) * LOG2E, -np.inf)
    return out


def _swa_tile_chains(sink_ref, bias_ref, q_ref, k_prev, k_ref, vt_prev, vt_ref, gone, store):
    nsub = q_ref.shape[2] // WINDOW
    ones = jnp.ones((BF16_SUBLANES, 2 * WINDOW), BF16)
    width = SWA_STACK * WINDOW
    chains = [(c, g0, kv) for c in range(nsub) for g0 in range(0, SWA_GROUP, SWA_STACK)
              for kv in range(SWA_KV_HEADS)]

    def scores(c, g0, kv):
        cols = slice(kv * LANES, (kv + 1) * LANES)
        if c == 0:
            k_win = jnp.concatenate([k_prev(cols), k_ref[0, :WINDOW, cols]], axis=0)
        else:
            k_win = k_ref[0, (c - 1) * WINDOW:(c + 1) * WINDOW, cols]
        q_st = q_ref[0, g0:g0 + SWA_STACK, c * WINDOW:(c + 1) * WINDOW, :].reshape(width, LANES)
        st = lax.dot_general(k_win, q_st, (((1,), (1,)), ((), ())), preferred_element_type=F32)
        return st + bias_ref[kv, :, g0 * WINDOW:g0 * WINDOW + width]

    outs = {}
    pending = [scores(*ch) for ch in chains[:SWA_LOOKAHEAD]]
    for n, (c, g0, kv) in enumerate(chains):
        st = pending.pop(0)
        if n + SWA_LOOKAHEAD < len(chains):
            pending.append(scores(*chains[n + SWA_LOOKAHEAD]))
        rows = slice(kv * SWA_HEAD_DIM, (kv + 1) * SWA_HEAD_DIM)
        if c == 0:
            vt = jnp.concatenate([vt_prev(rows), vt_ref[0, 0, rows, :]], axis=1)
            st = jnp.concatenate([st[:WINDOW] + gone, st[WINDOW:]], axis=0)
        else:
            vt = jnp.concatenate([vt_ref[0, c - 1, rows, :], vt_ref[0, c, rows, :]], axis=1)
        sink = sink_ref[kv:kv + 1, g0 * WINDOW:g0 * WINDOW + width]
        m = jnp.maximum(jnp.max(st, axis=0, keepdims=True), sink)
        pt = jnp.exp2(st - m).astype(BF16)
        res = jnp.dot(jnp.concatenate([vt, ones], axis=0), pt, preferred_element_type=F32)
        denom = res[SWA_HEAD_DIM:SWA_HEAD_DIM + 1] + jnp.exp2(sink - m)
        outs[kv] = res[:SWA_HEAD_DIM] / denom
        if kv == SWA_KV_HEADS - 1:
            for gg in range(SWA_STACK):
                gs = slice(gg * WINDOW, (gg + 1) * WINDOW)
                out_t = jnp.concatenate([outs[0][:, gs], outs[1][:, gs]], axis=0)
                store(c, g0 + gg, out_t.T.astype(BF16))
        yield


def _post_kernel(x_ref, mla_ref, mod_ref, wo_ref, gmlp_ref, wup_ref, wdown_ref, gfin_ref,
                 sink_ref, bias_ref, qn_ref, kpn_ref, kn_ref, vtpn_ref, vtn_ref, q0_ref, k0_ref, vt0_ref,
                 o_ref, swa_ref, *, ts, ff_chunk):
    step = pl.program_id(0) * pl.num_programs(1) + pl.program_id(1)
    n_steps = pl.num_programs(0) * pl.num_programs(1)
    n_chunks = D_FF // ff_chunk

    def store(c, g, tile):
        swa_ref[c * WINDOW:(c + 1) * WINDOW, g * LANES:(g + 1) * LANES] = tile

    @pl.when(step == 0)
    def _():
        for _ in _swa_tile_chains(sink_ref, bias_ref, q0_ref, lambda cols: k0_ref[0, :WINDOW, cols], k0_ref,
                                  lambda rows: vt0_ref[0, 0, rows, :], vt0_ref, -jnp.inf, store):
            pass

    mod = mod_ref[:, 0, 0, :]
    g1, sh2, sc2, g2 = mod[2:3], mod[3:4], mod[4:5], mod[5:6]
    woa = wo_ref[:MLA_V_COLS, :].astype(BF16)
    wob = jnp.concatenate(
        [wo_ref[MLA_V_COLS + (kv * SWA_GROUP + g) * SWA_HEAD_DIM:MLA_V_COLS + (kv * SWA_GROUP + g + 1) * SWA_HEAD_DIM, :]
         for g in range(SWA_GROUP) for kv in range(SWA_KV_HEADS)], axis=0).astype(BF16)
    tiles = [slice(r0, r0 + ts) for r0 in range(0, x_ref.shape[1], ts)]
    mid = []
    for rows in tiles:
        attn = (jnp.dot(mla_ref[0, rows], woa, preferred_element_type=F32)
                + jnp.dot(swa_ref[rows, :], wob, preferred_element_type=F32))
        x1 = x_ref[0, rows] + g1 * attn
        h2 = (_rms(x1) * gmlp_ref[...] * (1.0 + sc2) + sh2).astype(BF16)
        mid.append((x1, h2))
    nxt = jnp.minimum(step + 1, n_steps - 1) % pl.num_programs(1)
    gone = jnp.where(nxt == 0, -jnp.inf, 0.0)
    chains = _swa_tile_chains(sink_ref, bias_ref, qn_ref, lambda cols: kpn_ref[0, :, cols], kn_ref,
                              lambda rows: vtpn_ref[0, 0, rows, :], vtn_ref, gone, store)
    n_slots = 2 * n_chunks * len(tiles)
    n_attn = (qn_ref.shape[2] // WINDOW) * (SWA_GROUP // SWA_STACK) * SWA_KV_HEADS
    emitted = 0
    accs = [jnp.zeros_like(x1) for x1, _ in mid]
    for c in range(n_chunks):
        cs = slice(c * ff_chunk, (c + 1) * ff_chunk)
        w_up_c = wup_ref[:, cs]
        w_down_c = wdown_ref[cs, :]
        us = []
        for t in range(2 * len(mid)):
            if t < len(mid):
                us.append(jnp.maximum(jnp.dot(mid[t][1], w_up_c, preferred_element_type=F32), 0.0))
            else:
                u = us[t - len(mid)]
                accs[t - len(mid)] = accs[t - len(mid)] + jnp.dot((u * u).astype(BF16), w_down_c,
                                                                  preferred_element_type=F32)
            slot = c * 2 * len(mid) + t + 1
            while emitted < slot * n_attn // n_slots:
                next(chains)
                emitted += 1
    for rows, (x1, _), acc in zip(tiles, mid, accs):
        x2 = x1 + g2 * acc
        o_ref[0, rows] = _rms(x2) * gfin_ref[...]


def _post_call(x, mla, mod, w_o, gmlp, wup, wdown, gfin, sinks, sq, sk, svt, tm, ts, ff_chunk):
    b, s, d = x.shape
    nt = s // tm
    nsub = tm // WINDOW
    bias = jnp.asarray(_swa_bias())
    sink_rows = jnp.repeat(sinks.reshape(SWA_KV_HEADS, SWA_GROUP), WINDOW, axis=1) * LOG2E
    row = lambda bi, i: (bi, i, 0)

    def nxt(bi, i):
        n = jnp.minimum(bi * nt + i + 1, b * nt - 1)
        return n // nt, n % nt

    def nxt_prev_blk(bi, i):
        bn, tn = nxt(bi, i)
        return bn, jnp.maximum(tn * nsub - 1, 0)

    return pl.pallas_call(
        functools.partial(_post_kernel, ts=ts, ff_chunk=ff_chunk),
        grid=(b, nt),
        in_specs=[
            pl.BlockSpec((1, tm, d), row),
            pl.BlockSpec((1, tm, MLA_V_COLS), row),
            _mod_spec(),
            _const_spec(w_o.shape),
            _const_spec((1, d)),
            _const_spec(wup.shape),
            _const_spec(wdown.shape),
            _const_spec((1, d)),
            _const_spec(sink_rows.shape),
            _const_spec(bias.shape),
            pl.BlockSpec((1, SWA_GROUP, tm, LANES), lambda bi, i: (nxt(bi, i)[0], 0, nxt(bi, i)[1], 0)),
            pl.BlockSpec((1, WINDOW, SWA_KV_HEADS * LANES), lambda bi, i: nxt_prev_blk(bi, i) + (0,)),
            pl.BlockSpec((1, tm, SWA_KV_HEADS * LANES), lambda bi, i: nxt(bi, i) + (0,)),
            pl.BlockSpec((1, 1, SWA_KV_COLS, WINDOW), lambda bi, i: nxt_prev_blk(bi, i) + (0, 0)),
            pl.BlockSpec((1, nsub, SWA_KV_COLS, WINDOW), lambda bi, i: nxt(bi, i) + (0, 0)),
            pl.BlockSpec((1, SWA_GROUP, tm, LANES), lambda bi, i: (0, 0, 0, 0), pipeline_mode=pl.Buffered(1)),
            pl.BlockSpec((1, tm, SWA_KV_HEADS * LANES), lambda bi, i: (0, 0, 0), pipeline_mode=pl.Buffered(1)),
            pl.BlockSpec((1, nsub, SWA_KV_COLS, WINDOW), lambda bi, i: (0, 0, 0, 0), pipeline_mode=pl.Buffered(1)),
        ],
        out_specs=pl.BlockSpec((1, tm, d), row),
        out_shape=jax.ShapeDtypeStruct((b, s, d), F32),
        scratch_shapes=[pltpu.VMEM((tm, SWA_Q_COLS), BF16)],
        compiler_params=pltpu.CompilerParams(
            dimension_semantics=("arbitrary", "arbitrary"), vmem_limit_bytes=VMEM_LIMIT_BYTES),
        name="post_attn",
    )(x, mla, mod, w_o, gmlp, wup, wdown, gfin, sink_rows, bias, sq, sk, sk, svt, svt, sq, sk, svt)


def kernel(x, c, w_ada, b_ada, norm_mix_g, w_in, g_qa, w_qb, g_kva, w_kvb, sinks,
           w_o, norm_mlp_g, w_up, w_down, final_g):
    depth = w_ada.shape[0]
    b = x.shape[0]
    assert depth == 1, "the final rmsnorm is fused into the single layer's post-attention call"
    for l in range(depth):
        mod = _ada_call(c, w_ada[l], b_ada[l])
        qf, kf, vt, sq, sk, svt = _pre_call(
            x, mod, norm_mix_g[l][None], w_in[l].T, g_qa[l][None],
            w_qb[l].transpose(1, 2, 0).reshape(-1, Q_LORA), g_kva[l][None], w_kvb[l].reshape(KV_LORA, -1),
            tm=1024, ts=256, tk=256)
        mla, w_up_bf, w_down_bf = _mla_call(qf, kf, vt, w_up[l], w_down[l], tq=256, hps=2)
        x = _post_call(x, mla, mod, w_o[l], norm_mlp_g[l][None], w_up_bf, w_down_bf, final_g[None],
                       sinks[l], sq, sk, svt, tm=1024, ts=256, ff_chunk=1024)
    return x
```

```python
import functools

import numpy as np
import jax
import jax.numpy as jnp
from jax import lax
from jax.experimental import pallas as pl
from jax.experimental.pallas import tpu as pltpu

D_MODEL = 1024
SEQ = 2048
MLA_HEADS = 8
MLA_NOPE = 64
MLA_ROPE = 32
MLA_V = 64
Q_LORA = 384
KV_LORA = 256
ROPE_THETA = 10000.0
SWA_HEADS = 8
SWA_KV_HEADS = 2
SWA_GROUP = SWA_HEADS // SWA_KV_HEADS
SWA_HEAD_DIM = 64
WINDOW = 128
D_FF = 4 * D_MODEL
EPS = 1e-6
N_MOD = 6

LANES = 128
HALF = LANES // 2
BF16_SUBLANES = 16
MLA_QK_COLS = MLA_HEADS * LANES
MLA_V_COLS = MLA_HEADS * MLA_V
SWA_Q_COLS = SWA_HEADS * SWA_HEAD_DIM
SWA_KV_COLS = SWA_KV_HEADS * SWA_HEAD_DIM
C_QLAT = 0
C_KVLAT = C_QLAT + Q_LORA
C_KPE = C_KVLAT + KV_LORA
C_SQ = C_KPE + LANES
C_SK = C_SQ + SWA_Q_COLS
C_SV = C_SK + SWA_KV_COLS
IN_COLS_PAD = C_SV + SWA_KV_COLS

VMEM_LIMIT_BYTES = 62 * 1024 * 1024

LOG2E = float(np.log2(np.e))
MLA_Q_SCALE = (MLA_NOPE + MLA_ROPE) ** -0.5 * LOG2E
SWA_Q_SCALE = SWA_HEAD_DIM ** -0.5 * LOG2E

BF16 = jnp.bfloat16
F32 = jnp.float32


def _const_spec(shape):
    nd = len(shape)
    return pl.BlockSpec(shape, lambda *_: (0,) * nd, pipeline_mode=pl.Buffered(1))


def _rms(x):
    return x * lax.rsqrt(jnp.mean(x * x, axis=-1, keepdims=True) + EPS)


def _ada_kernel(c_ref, w_ref, b_ref, o_ref):
    c = c_ref[...]
    s = c / (1.0 + jnp.exp(-c))
    o_ref[0, :, 0, :] = jnp.dot(s.astype(BF16), w_ref[...].astype(BF16), preferred_element_type=F32) + b_ref[...]


def _ada_call(c, w_ada, b_ada):
    b = c.shape[0]
    n = w_ada.shape[1]
    tn = D_MODEL
    return pl.pallas_call(
        _ada_kernel,
        grid=(n // tn,),
        in_specs=[
            pl.BlockSpec((b, D_MODEL), lambda j: (0, 0)),
            pl.BlockSpec((D_MODEL, tn), lambda j: (0, j)),
            pl.BlockSpec((1, tn), lambda j: (0, j)),
        ],
        out_specs=pl.BlockSpec((1, b, 1, tn), lambda j: (j, 0, 0, 0)),
        out_shape=jax.ShapeDtypeStruct((n // tn, b, 1, tn), F32),
        compiler_params=pltpu.CompilerParams(dimension_semantics=("parallel",)),
        name="ada_mod",
    )(c, w_ada, b_ada.reshape(1, n))


def _mod_spec():
    return pl.BlockSpec((N_MOD, 1, 1, D_MODEL), lambda bi, i: (0, bi, 0, 0))


def _rope_group(xg, keep, swap):
    up = pltpu.roll(xg, LANES - MLA_ROPE // 2, 1)
    return xg * keep + up * swap


def _relayout_weights(w_in_ref, w_qb_ref, w_kvb_ref, win_ref, wq_ref, wk_ref, wv_ref):
    o2 = Q_LORA + KV_LORA
    o3 = o2 + MLA_ROPE
    o4 = o3 + SWA_Q_COLS
    half = MLA_ROPE // 2
    pad = LANES - MLA_NOPE - MLA_ROPE - half
    d_in = w_in_ref.shape[1]
    kpe = [jnp.zeros((MLA_NOPE, d_in), F32), w_in_ref[o2:o3, :], w_in_ref[o2:o2 + half, :],
           jnp.zeros((pad, d_in), F32)]
    sq = [w_in_ref[o3 + (kv * SWA_GROUP + g) * SWA_HEAD_DIM:o3 + (kv * SWA_GROUP + g + 1) * SWA_HEAD_DIM, :]
          for g in range(SWA_GROUP) for kv in range(SWA_KV_HEADS)]
    win_t = jnp.concatenate([w_in_ref[:o2, :]] + kpe + sq + [w_in_ref[o4:, :]], axis=0)
    chunk = 2 * LANES
    for c0 in range(0, IN_COLS_PAD, chunk):
        win_ref[:, c0:c0 + chunk] = win_t[c0:c0 + chunk, :].T.astype(BF16)
    hd_rows = MLA_NOPE + MLA_ROPE
    parts = []
    for hd in range(MLA_HEADS):
        r0 = hd * hd_rows
        parts += [w_qb_ref[r0:r0 + hd_rows, :], w_qb_ref[r0 + MLA_NOPE:r0 + MLA_NOPE + half, :],
                  jnp.zeros((pad, w_qb_ref.shape[1]), F32)]
    wq_ref[...] = jnp.concatenate(parts, axis=0).astype(BF16)
    wkvb = w_kvb_ref[...]
    wk_ref[...] = jnp.concatenate([wkvb[:, hd * LANES:hd * LANES + MLA_NOPE] for hd in range(MLA_HEADS)],
                                  axis=1).astype(BF16)
    wv = jnp.concatenate([wkvb[:, hd * LANES + MLA_NOPE:(hd + 1) * LANES] for hd in range(MLA_HEADS)], axis=1)
    wv_ref[...] = wv.T.astype(BF16)


def _pre_kernel(x_ref, mod_ref, gmix_ref, w_in_ref, gqa_ref, w_qb_ref, gkva_ref, w_kvb_ref,
                ropeq_ref, ropek_ref,
                qf_ref, kf_ref, v_ref, sq_ref, sk_ref, sv_ref,
                win_ref, wq_ref, wk_ref, wv_ref, *, ts):
    @pl.when((pl.program_id(0) == 0) & (pl.program_id(1) == 0))
    def _():
        _relayout_weights(w_in_ref, w_qb_ref, w_kvb_ref, win_ref, wq_ref, wk_ref, wv_ref)

    for r0 in range(0, x_ref.shape[1], ts):
        _pre_rows(r0, ts, x_ref, mod_ref, gmix_ref, win_ref, gqa_ref, wq_ref, gkva_ref, wk_ref, wv_ref,
                  ropeq_ref, ropek_ref, qf_ref, kf_ref, v_ref, sq_ref, sk_ref, sv_ref)


def _pre_rows(r0, ts, x_ref, mod_ref, gmix_ref, win_ref, gqa_ref, wq_ref, gkva_ref, wk_ref, wv_ref,
              ropeq_ref, ropek_ref, qf_ref, kf_ref, v_ref, sq_ref, sk_ref, sv_ref):
    rows = slice(r0, r0 + ts)
    x = x_ref[0, rows]
    mod = mod_ref[:, 0, 0, :]
    sh1 = mod[0:1]
    sc1 = mod[1:2]
    h = (_rms(x) * gmix_ref[...] * (1.0 + sc1) + sh1).astype(BF16)
    proj = jnp.dot(h, win_ref[...], preferred_element_type=F32)

    for g in range(SWA_GROUP):
        sq_ref[0, g, rows] = (proj[:, C_SQ + g * LANES:C_SQ + (g + 1) * LANES] * SWA_Q_SCALE).astype(BF16)
    swa_k = proj[:, C_SK:C_SV]
    low = lax.broadcasted_iota(jnp.int32, swa_k.shape, 1) < HALF
    sk_ref[0, rows, :LANES] = jnp.where(low, swa_k, 0.0).astype(BF16)
    sk_ref[0, rows, LANES:] = jnp.where(low, 0.0, swa_k).astype(BF16)
    svt = proj[:, C_SV:IN_COLS_PAD].T.astype(BF16)
    for jb in range(ts // WINDOW):
        sv_ref[0, r0 // WINDOW + jb] = svt[:, jb * WINDOW:(jb + 1) * WINDOW]

    qn = (_rms(proj[:, C_QLAT:C_KVLAT]) * gqa_ref[...]).astype(BF16)
    kvn = (_rms(proj[:, C_KVLAT:C_KPE]) * gkva_ref[...]).astype(BF16)
    qt = lax.dot_general(wq_ref[...], qn, (((1,), (1,)), ((), ())), preferred_element_type=F32)
    kn = jnp.dot(kvn, wk_ref[...], preferred_element_type=F32)
    vt = lax.dot_general(wv_ref[...], kvn, (((1,), (1,)), ((), ())), preferred_element_type=F32).astype(BF16)
    tk = v_ref.shape[-1]
    for hp in range(MLA_HEADS // 2):
        for jb in range(ts // tk):
            v_ref[0, hp, r0 // tk + jb] = vt[hp * LANES:(hp + 1) * LANES, jb * tk:(jb + 1) * tk]

    q_keep, q_swap = ropeq_ref[0, :, rows], ropeq_ref[1, :, rows]
    half = MLA_ROPE // 2
    kpe = _rope_group(proj[:, C_KPE:C_SQ], ropek_ref[0, rows], ropek_ref[1, rows])
    nope_lanes = lax.broadcasted_iota(jnp.int32, kpe.shape, 1) < MLA_NOPE
    for hd in range(MLA_HEADS):
        sl = slice(hd * LANES, (hd + 1) * LANES)
        xg = qt[sl, :]
        up = jnp.concatenate([xg[half:], xg[:half]], axis=0)
        qf_ref[0, sl, rows] = (xg * q_keep + up * q_swap).astype(BF16)
        kn_pair = kn[:, (hd // 2) * LANES:(hd // 2 + 1) * LANES]
        if hd % 2:
            kn_pair = pltpu.roll(kn_pair, HALF, 1)
        kf_ref[0, rows, sl] = jnp.where(nope_lanes, kn_pair, kpe).astype(BF16)


def _rope_tables(scale_q):
    half = MLA_ROPE // 2
    freqs = ROPE_THETA ** (-np.arange(0, MLA_ROPE, 2, dtype=np.float64) / MLA_ROPE)
    ang = np.arange(SEQ, dtype=np.float64)[:, None] * freqs[None, :]
    cos, sin = np.cos(ang), np.sin(ang)
    keep = np.zeros((SEQ, LANES))
    swap = np.zeros((SEQ, LANES))
    keep[:, :MLA_NOPE] = 1.0
    keep[:, MLA_NOPE:MLA_NOPE + half] = cos
    keep[:, MLA_NOPE + half:MLA_NOPE + MLA_ROPE] = cos
    swap[:, MLA_NOPE:MLA_NOPE + half] = -sin
    swap[:, MLA_NOPE + half:MLA_NOPE + MLA_ROPE] = sin
    return (np.stack([keep, swap]) * scale_q).astype(np.float32)


def _pre_call(x, mod, gmix, w_in, gqa, w_qb, gkva, w_kvb, tm, ts, tk):
    b, s, d = x.shape
    pairs = MLA_HEADS // 2
    ropeq = jnp.asarray(_rope_tables(MLA_Q_SCALE).transpose(0, 2, 1))
    ropek = jnp.asarray(_rope_tables(1.0))
    row = lambda bi, i: (bi, i, 0)
    out_shapes = (
        jax.ShapeDtypeStruct((b, MLA_QK_COLS, s), BF16),
        jax.ShapeDtypeStruct((b, s, MLA_QK_COLS), BF16),
        jax.ShapeDtypeStruct((b, pairs, s // tk, LANES, tk), BF16),
        jax.ShapeDtypeStruct((b, SWA_GROUP, s, LANES), BF16),
        jax.ShapeDtypeStruct((b, s, SWA_KV_HEADS * LANES), BF16),
        jax.ShapeDtypeStruct((b, s // WINDOW, SWA_KV_COLS, WINDOW), BF16),
    )
    out_specs = [
        pl.BlockSpec((1, MLA_QK_COLS, tm), lambda bi, i: (bi, 0, i)),
        pl.BlockSpec((1, tm, MLA_QK_COLS), row),
        pl.BlockSpec((1, pairs, tm // tk, LANES, tk), lambda bi, i: (bi, 0, i, 0, 0)),
        pl.BlockSpec((1, SWA_GROUP, tm, LANES), lambda bi, i: (bi, 0, i, 0)),
        pl.BlockSpec((1, tm, SWA_KV_HEADS * LANES), row),
        pl.BlockSpec((1, tm // WINDOW, SWA_KV_COLS, WINDOW), lambda bi, i: (bi, i, 0, 0)),
    ]
    return pl.pallas_call(
        functools.partial(_pre_kernel, ts=ts),
        grid=(b, s // tm),
        in_specs=[
            pl.BlockSpec((1, tm, d), row),
            _mod_spec(),
            _const_spec((1, d)),
            _const_spec(w_in.shape),
            _const_spec((1, Q_LORA)),
            _const_spec(w_qb.shape),
            _const_spec((1, KV_LORA)),
            _const_spec(w_kvb.shape),
            pl.BlockSpec((2, LANES, tm), lambda bi, i: (0, 0, i)),
            pl.BlockSpec((2, tm, LANES), lambda bi, i: (0, i, 0)),
        ],
        out_specs=tuple(out_specs),
        out_shape=out_shapes,
        scratch_shapes=[
            pltpu.VMEM((d, IN_COLS_PAD), BF16),
            pltpu.VMEM((MLA_QK_COLS, Q_LORA), BF16),
            pltpu.VMEM((KV_LORA, MLA_HEADS * MLA_NOPE), BF16),
            pltpu.VMEM((MLA_V_COLS, KV_LORA), BF16),
        ],
        compiler_params=pltpu.CompilerParams(
            dimension_semantics=("arbitrary", "arbitrary"), vmem_limit_bytes=VMEM_LIMIT_BYTES),
        name="pre_attn",
    )(x, mod, gmix, w_in, gqa, w_qb, gkva, w_kvb, ropeq, ropek)


MLA_LOOKAHEAD = 6


def _mla_kernel(q_ref, k_ref, vt_ref, wup_ref, wdown_ref, o_ref, wup_bf_ref, wdown_bf_ref, *, tq):
    wup_bf_ref[...] = wup_ref[...].astype(BF16)
    wdown_bf_ref[...] = wdown_ref[...].astype(BF16)
    ki = lax.broadcasted_iota(jnp.int32, (tq, tq), 0)
    qi = lax.broadcasted_iota(jnp.int32, (tq, tq), 1)
    causal = ki <= qi
    heads = range(q_ref.shape[1] // LANES)
    chains = [(i, j, h) for i in reversed(range(q_ref.shape[2] // tq)) for h in heads for j in range(i + 1)]

    def scores(i, j, h):
        cols = slice(h * LANES, (h + 1) * LANES)
        qt = q_ref[0, cols, i * tq:(i + 1) * tq]
        k = k_ref[0, j * tq:(j + 1) * tq, cols]
        return jnp.dot(k, qt, preferred_element_type=F32)

    ones = jnp.ones((BF16_SUBLANES, tq), BF16)
    state = {}
    pending = [scores(*ch) for ch in chains[:MLA_LOOKAHEAD]]
    for n, (i, j, h) in enumerate(chains):
        st = pending.pop(0)
        if n + MLA_LOOKAHEAD < len(chains):
            pending.append(scores(*chains[n + MLA_LOOKAHEAD]))
        if j == 0:
            state[h] = (jnp.full((1, tq), -jnp.inf, F32), jnp.zeros((MLA_V + BF16_SUBLANES, tq), F32))
        m, acc = state[h]
        if j == i:
            st = jnp.where(causal, st, -jnp.inf)
        vt = vt_ref[0, h // 2, j, (h % 2) * MLA_V:(h % 2 + 1) * MLA_V, :]
        vt = jnp.concatenate([vt, ones], axis=0)
        m_new = jnp.maximum(m, jnp.max(st, axis=0, keepdims=True))
        alpha = jnp.exp2(m - m_new)
        pt = jnp.exp2(st - m_new).astype(BF16)
        acc_new = alpha * acc + jnp.dot(vt, pt, preferred_element_type=F32)
        state[h] = (m_new, acc_new)
        if j == i and h % 2 == 1:
            p = h // 2
            pair = [state[2 * p][1], state[2 * p + 1][1]]
            out_t = jnp.concatenate([acc[:MLA_V] / acc[MLA_V:MLA_V + 1] for acc in pair], axis=0)
            o_ref[0, i * tq:(i + 1) * tq, p * LANES:(p + 1) * LANES] = out_t.T.astype(BF16)


def _mla_call(qf, kf, vt, w_up, w_down, tq, hps):
    b, s, _ = kf.shape
    pairs = MLA_HEADS // 2
    assert vt.shape == (b, pairs, s // tq, LANES, tq) and pairs % hps == 0 and qf.shape == (b, MLA_QK_COLS, s)
    seq_cols = lambda bi, g: (bi, 0, g)
    n_steps = b * (pairs // hps)
    step_rows = lambda bi, g: (bi * (pairs // hps) + g, 0)
    w_specs = [pl.BlockSpec((w.shape[0] // n_steps, w.shape[1]), step_rows) for w in (w_up, w_down)]
    return pl.pallas_call(
        functools.partial(_mla_kernel, tq=tq),
        grid=(b, pairs // hps),
        in_specs=[
            pl.BlockSpec((1, 2 * hps * LANES, s), lambda bi, g: (bi, g, 0)),
            pl.BlockSpec((1, s, 2 * hps * LANES), seq_cols),
            pl.BlockSpec((1, hps, s // tq, LANES, tq), lambda bi, g: (bi, g, 0, 0, 0)),
        ] + w_specs,
        out_specs=[pl.BlockSpec((1, s, hps * LANES), seq_cols)] + w_specs,
        out_shape=[jax.ShapeDtypeStruct((b, s, MLA_V_COLS), BF16),
                   jax.ShapeDtypeStruct(w_up.shape, BF16), jax.ShapeDtypeStruct(w_down.shape, BF16)],
        compiler_params=pltpu.CompilerParams(
            dimension_semantics=("parallel", "parallel"), vmem_limit_bytes=VMEM_LIMIT_BYTES),
        name="mla_attn",
    )(qf, kf, vt, w_up, w_down)


def _alibi_slope(h):
    return 2.0 ** (-8.0 * (h + 1) / SWA_HEADS)


SWA_STACK = 2
SWA_LOOKAHEAD = 6


def _swa_bias():
    k = np.arange(2 * WINDOW)[:, None]
    q = np.arange(WINDOW)[None, :]
    dist = q + WINDOW - k
    valid = (dist >= 0) & (dist < WINDOW)
    out = np.empty((SWA_KV_HEADS, 2 * WINDOW, SWA_GROUP * WINDOW), np.float32)
    for kv in range(SWA_KV_HEADS):
        for g in range(SWA_GROUP):
            slope = np.float32(_alibi_slope(kv * SWA_GROUP + g))
            out[kv, :, g * WINDOW:(g + 1) * WINDOW] = np.where(valid, -slope * dist.astype(np.float32) * LOG2E, -np.inf)
    return out


def _swa_tile_chains(sink_ref, bias_ref, q_ref, k_prev, k_ref, vt_prev, vt_ref, gone, store):
    nsub = q_ref.shape[2] // WINDOW
    ones = jnp.ones((BF16_SUBLANES, 2 * WINDOW), BF16)
    width = SWA_STACK * WINDOW
    chains = [(c, g0, kv) for c in range(nsub) for g0 in range(0, SWA_GROUP, SWA_STACK)
              for kv in range(SWA_KV_HEADS)]

    def scores(c, g0, kv):
        cols = slice(kv * LANES, (kv + 1) * LANES)
        if c == 0:
            k_win = jnp.concatenate([k_prev(cols), k_ref[0, :WINDOW, cols]], axis=0)
        else:
            k_win = k_ref[0, (c - 1) * WINDOW:(c + 1) * WINDOW, cols]
        q_st = q_ref[0, g0:g0 + SWA_STACK, c * WINDOW:(c + 1) * WINDOW, :].reshape(width, LANES)
        st = lax.dot_general(k_win, q_st, (((1,), (1,)), ((), ())), preferred_element_type=F32)
        return st + bias_ref[kv, :, g0 * WINDOW:g0 * WINDOW + width]

    outs = {}
    pending = [scores(*ch) for ch in chains[:SWA_LOOKAHEAD]]
    for n, (c, g0, kv) in enumerate(chains):
        st = pending.pop(0)
        if n + SWA_LOOKAHEAD < len(chains):
            pending.append(scores(*chains[n + SWA_LOOKAHEAD]))
        rows = slice(kv * SWA_HEAD_DIM, (kv + 1) * SWA_HEAD_DIM)
        if c == 0:
            vt = jnp.concatenate([vt_prev(rows), vt_ref[0, 0, rows, :]], axis=1)
            st = jnp.concatenate([st[:WINDOW] + gone, st[WINDOW:]], axis=0)
        else:
            vt = jnp.concatenate([vt_ref[0, c - 1, rows, :], vt_ref[0, c, rows, :]], axis=1)
        sink = sink_ref[kv:kv + 1, g0 * WINDOW:g0 * WINDOW + width]
        m = jnp.maximum(jnp.max(st, axis=0, keepdims=True), sink)
        pt = jnp.exp2(st - m).astype(BF16)
        res = jnp.dot(jnp.concatenate([vt, ones], axis=0), pt, preferred_element_type=F32)
        denom = res[SWA_HEAD_DIM:SWA_HEAD_DIM + 1] + jnp.exp2(sink - m)
        outs[kv] = res[:SWA_HEAD_DIM] / denom
        if kv == SWA_KV_HEADS - 1:
            for gg in range(SWA_STACK):
                gs = slice(gg * WINDOW, (gg + 1) * WINDOW)
                out_t = jnp.concatenate([outs[0][:, gs], outs[1][:, gs]], axis=0)
                store(c, g0 + gg, out_t.T.astype(BF16))
        yield


def _post_kernel(x_ref, mla_ref, mod_ref, wo_ref, gmlp_ref, wup_ref, wdown_ref, gfin_ref,
                 sink_ref, bias_ref, qn_ref, kpn_ref, kn_ref, vtpn_ref, vtn_ref, q0_ref, k0_ref, vt0_ref,
                 o_ref, swa_ref, *, ts, ff_chunk):
    step = pl.program_id(0) * pl.num_programs(1) + pl.program_id(1)
    n_steps = pl.num_programs(0) * pl.num_programs(1)
    n_chunks = D_FF // ff_chunk

    def store(c, g, tile):
        swa_ref[c * WINDOW:(c + 1) * WINDOW, g * LANES:(g + 1) * LANES] = tile

    @pl.when(step == 0)
    def _():
        for _ in _swa_tile_chains(sink_ref, bias_ref, q0_ref, lambda cols: k0_ref[0, :WINDOW, cols], k0_ref,
                                  lambda rows: vt0_ref[0, 0, rows, :], vt0_ref, -jnp.inf, store):
            pass

    mod = mod_ref[:, 0, 0, :]
    g1, sh2, sc2, g2 = mod[2:3], mod[3:4], mod[4:5], mod[5:6]
    woa = wo_ref[:MLA_V_COLS, :].astype(BF16)
    wob = jnp.concatenate(
        [wo_ref[MLA_V_COLS + (kv * SWA_GROUP + g) * SWA_HEAD_DIM:MLA_V_COLS + (kv * SWA_GROUP + g + 1) * SWA_HEAD_DIM, :]
         for g in range(SWA_GROUP) for kv in range(SWA_KV_HEADS)], axis=0).astype(BF16)
    tiles = [slice(r0, r0 + ts) for r0 in range(0, x_ref.shape[1], ts)]
    mid = []
    for rows in tiles:
        attn = (jnp.dot(mla_ref[0, rows], woa, preferred_element_type=F32)
                + jnp.dot(swa_ref[rows, :], wob, preferred_element_type=F32))
        x1 = x_ref[0, rows] + g1 * attn
        h2 = (_rms(x1) * gmlp_ref[...] * (1.0 + sc2) + sh2).astype(BF16)
        mid.append((x1, h2))
    nxt = jnp.minimum(step + 1, n_steps - 1) % pl.num_programs(1)
    gone = jnp.where(nxt == 0, -jnp.inf, 0.0)
    chains = _swa_tile_chains(sink_ref, bias_ref, qn_ref, lambda cols: kpn_ref[0, :, cols], kn_ref,
                              lambda rows: vtpn_ref[0, 0, rows, :], vtn_ref, gone, store)
    n_slots = 2 * n_chunks * len(tiles)
    n_attn = (qn_ref.shape[2] // WINDOW) * (SWA_GROUP // SWA_STACK) * SWA_KV_HEADS
    emitted = 0
    accs = [jnp.zeros_like(x1) for x1, _ in mid]
    for c in range(n_chunks):
        cs = slice(c * ff_chunk, (c + 1) * ff_chunk)
        w_up_c = wup_ref[:, cs]
        w_down_c = wdown_ref[cs, :]
        us = []
        for t in range(2 * len(mid)):
            if t < len(mid):
                us.append(jnp.maximum(jnp.dot(mid[t][1], w_up_c, preferred_element_type=F32), 0.0))
            else:
                u = us[t - len(mid)]
                accs[t - len(mid)] = accs[t - len(mid)] + jnp.dot((u * u).astype(BF16), w_down_c,
                                                                  preferred_element_type=F32)
            slot = c * 2 * len(mid) + t + 1
            while emitted < slot * n_attn // n_slots:
                next(chains)
                emitted += 1
    for rows, (x1, _), acc in zip(tiles, mid, accs):
        x2 = x1 + g2 * acc
        o_ref[0, rows] = _rms(x2) * gfin_ref[...]


def _post_call(x, mla, mod, w_o, gmlp, wup, wdown, gfin, sinks, sq, sk, svt, tm, ts, ff_chunk):
    b, s, d = x.shape
    nt = s // tm
    nsub = tm // WINDOW
    bias = jnp.asarray(_swa_bias())
    sink_rows = jnp.repeat(sinks.reshape(SWA_KV_HEADS, SWA_GROUP), WINDOW, axis=1) * LOG2E
    row = lambda bi, i: (bi, i, 0)

    def nxt(bi, i):
        n = jnp.minimum(bi * nt + i + 1, b * nt - 1)
        return n // nt, n % nt

    def nxt_prev_blk(bi, i):
        bn, tn = nxt(bi, i)
        return bn, jnp.maximum(tn * nsub - 1, 0)

    return pl.pallas_call(
        functools.partial(_post_kernel, ts=ts, ff_chunk=ff_chunk),
        grid=(b, nt),
        in_specs=[
            pl.BlockSpec((1, tm, d), row),
            pl.BlockSpec((1, tm, MLA_V_COLS), row),
            _mod_spec(),
            _const_spec(w_o.shape),
            _const_spec((1, d)),
            _const_spec(wup.shape),
            _const_spec(wdown.shape),
            _const_spec((1, d)),
            _const_spec(sink_rows.shape),
            _const_spec(bias.shape),
            pl.BlockSpec((1, SWA_GROUP, tm, LANES), lambda bi, i: (nxt(bi, i)[0], 0, nxt(bi, i)[1], 0)),
            pl.BlockSpec((1, WINDOW, SWA_KV_HEADS * LANES), lambda bi, i: nxt_prev_blk(bi, i) + (0,)),
            pl.BlockSpec((1, tm, SWA_KV_HEADS * LANES), lambda bi, i: nxt(bi, i) + (0,)),
            pl.BlockSpec((1, 1, SWA_KV_COLS, WINDOW), lambda bi, i: nxt_prev_blk(bi, i) + (0, 0)),
            pl.BlockSpec((1, nsub, SWA_KV_COLS, WINDOW), lambda bi, i: nxt(bi, i) + (0, 0)),
            pl.BlockSpec((1, SWA_GROUP, tm, LANES), lambda bi, i: (0, 0, 0, 0), pipeline_mode=pl.Buffered(1)),
            pl.BlockSpec((1, tm, SWA_KV_HEADS * LANES), lambda bi, i: (0, 0, 0), pipeline_mode=pl.Buffered(1)),
            pl.BlockSpec((1, nsub, SWA_KV_COLS, WINDOW), lambda bi, i: (0, 0, 0, 0), pipeline_mode=pl.Buffered(1)),
        ],
        out_specs=pl.BlockSpec((1, tm, d), row),
        out_shape=jax.ShapeDtypeStruct((b, s, d), F32),
        scratch_shapes=[pltpu.VMEM((tm, SWA_Q_COLS), BF16)],
        compiler_params=pltpu.CompilerParams(
            dimension_semantics=("arbitrary", "arbitrary"), vmem_limit_bytes=VMEM_LIMIT_BYTES),
        name="post_attn",
    )(x, mla, mod, w_o, gmlp, wup, wdown, gfin, sink_rows, bias, sq, sk, sk, svt, svt, sq, sk, svt)


def kernel(x, c, w_ada, b_ada, norm_mix_g, w_in, g_qa, w_qb, g_kva, w_kvb, sinks,
           w_o, norm_mlp_g, w_up, w_down, final_g):
    depth = w_ada.shape[0]
    b = x.shape[0]
    assert depth == 1, "the final rmsnorm is fused into the single layer's post-attention call"
    for l in range(depth):
        mod = _ada_call(c, w_ada[l], b_ada[l])
        qf, kf, vt, sq, sk, svt = _pre_call(
            x, mod, norm_mix_g[l][None], w_in[l].T, g_qa[l][None],
            w_qb[l].transpose(1, 2, 0).reshape(-1, Q_LORA), g_kva[l][None], w_kvb[l].reshape(KV_LORA, -1),
            tm=1024, ts=256, tk=256)
        mla, w_up_bf, w_down_bf = _mla_call(qf, kf, vt, w_up[l], w_down[l], tq=256, hps=2)
        x = _post_call(x, mla, mod, w_o[l], norm_mlp_g[l][None], w_up_bf, w_down_bf, final_g[None],
                       sinks[l], sq, sk, svt, tm=1024, ts=256, ff_chunk=1024)
    return x
```
